```python
import math
import jax, jax.numpy as jnp
from jax import lax
import numpy as np

D_MODEL = 1024
BATCH = 8
SEQ = 4096
DEPTH = 1

A_HEADS = 8
A_HEAD_DIM = 64
A_WIDTH = A_HEADS * A_HEAD_DIM
DECAY_LORA = 64
ICLR_LORA = 64
DECAY_SCALE = 0.6065306597
GN_EPS = 64e-5
B_HEADS = 8
QK_NOPE_DIM = 64
QK_ROPE_DIM = 32
V_HEAD_DIM = 64
Q_LORA_RANK = 256
KV_LORA_RANK = 128
B_WIDTH = B_HEADS * V_HEAD_DIM
ROPE_THETA = 10000.0
Q_BLOCK = 128
NORM_EPS = 1e-6
N_BRANCHES = 2
SHIFT_COLS = 3 * A_WIDTH + DECAY_LORA + ICLR_LORA
IN_COLS = SHIFT_COLS + A_WIDTH + Q_LORA_RANK + KV_LORA_RANK + QK_ROPE_DIM + B_WIDTH + N_BRANCHES * D_MODEL

kernel_name = 'hybrid_rwkv7_mla_gated_block'


def rmsnorm(x, g, eps=NORM_EPS):
    xf = x.astype(jnp.float32)
    y = xf * lax.rsqrt(jnp.mean(xf * xf, axis=-1, keepdims=True) + eps)
    return (y * g.astype(jnp.float32)).astype(x.dtype)


def split_cols(t, sizes):
    idx = np.cumsum(sizes)[:-1].tolist()
    return jnp.split(t, idx, axis=-1)


def token_shift(feat, mu):
    prev = jnp.pad(feat, ((0, 0), (1, 0), (0, 0)))[:, :-1, :]
    return feat + mu * (prev - feat)


def rwkv7_branch(feat, z, mu_shift, w0, w_decay_up, a0, w_iclr_up, k_k, k_a, r_k, gn_gain, gn_bias, w_out_a):
    bsz, seq, _ = feat.shape
    dt = feat.dtype
    f = token_shift(feat, mu_shift)
    r, k, v, wd, ad = split_cols(f, [A_WIDTH, A_WIDTH, A_WIDTH, DECAY_LORA, ICLR_LORA])
    w_logit = (w0 + jnp.tanh(wd) @ w_decay_up).astype(jnp.float32)
    w = jnp.exp(-DECAY_SCALE * jax.nn.sigmoid(w_logit))
    a = jax.nn.sigmoid((a0 + ad @ w_iclr_up).astype(jnp.float32))
    hs = lambda t: t.reshape(bsz, seq, A_HEADS, A_HEAD_DIM).astype(jnp.float32)
    r, k, v, w, a = hs(r), hs(k), hs(v), hs(w), hs(a)
    kk = k * k_k.reshape(A_HEADS, A_HEAD_DIM)
    kk = kk / jnp.maximum(jnp.linalg.norm(kk, axis=-1, keepdims=True), 1e-12)
    k = k * (1.0 + (a - 1.0) * k_a.reshape(A_HEADS, A_HEAD_DIM))

    def step(S, inp):
        r_t, w_t, k_t, v_t, kk_t, a_t = inp
        sa = jnp.einsum('bhvk,bhk->bhv', S, -kk_t)
        S = S * w_t[:, :, None, :] + sa[..., None] * (kk_t * a_t)[:, :, None, :] + v_t[..., None] * k_t[:, :, None, :]
        return S, jnp.einsum('bhvk,bhk->bhv', S, r_t)

    tm = lambda t: jnp.moveaxis(t, 1, 0)
    S0 = jnp.zeros((bsz, A_HEADS, A_HEAD_DIM, A_HEAD_DIM), jnp.float32)
    _, y = lax.scan(step, S0, (tm(r), tm(w), tm(k), tm(v), tm(kk), tm(a)))
    y = jnp.moveaxis(y, 0, 1)
    mean = jnp.mean(y, axis=-1, keepdims=True)
    var = jnp.mean(jnp.square(y - mean), axis=-1, keepdims=True)
    y = (y - mean) * lax.rsqrt(var + GN_EPS)
    y = y * gn_gain.reshape(A_HEADS, A_HEAD_DIM) + gn_bias.reshape(A_HEADS, A_HEAD_DIM)
    y = y + jnp.sum(r * k * r_k, axis=-1, keepdims=True) * v
    y = y.reshape(bsz, seq, A_WIDTH).astype(dt) * jax.nn.silu(z)
    return y @ w_out_a


def rope(t, cos, sin):
    t1, t2 = jnp.split(t, 2, axis=-1)
    return jnp.concatenate([t1 * cos - t2 * sin, t2 * cos + t1 * sin], axis=-1)


def mla_branch(c_q, c_kv, k_pe, z, positions, g_q, w_uq, g_kv, w_ukv, w_out_b):
    bsz, seq, _ = c_q.shape
    dt = c_q.dtype
    q = (rmsnorm(c_q, g_q) @ w_uq).reshape(bsz, seq, B_HEADS, QK_NOPE_DIM + QK_ROPE_DIM)
    q_nope, q_pe = q[..., :QK_NOPE_DIM], q[..., QK_NOPE_DIM:]
    kv = (rmsnorm(c_kv, g_kv) @ w_ukv).reshape(bsz, seq, B_HEADS, QK_NOPE_DIM + V_HEAD_DIM)
    k_nope, v = kv[..., :QK_NOPE_DIM], kv[..., QK_NOPE_DIM:]
    inv_freq = ROPE_THETA ** (-jnp.arange(0, QK_ROPE_DIM, 2, dtype=jnp.float32) / QK_ROPE_DIM)
    ang = positions.astype(jnp.float32)[..., None] * inv_freq
    cos, sin = jnp.cos(ang).astype(dt), jnp.sin(ang).astype(dt)
    q_pe = rope(q_pe, cos[:, :, None, :], sin[:, :, None, :])
    k_pe = rope(k_pe, cos, sin)
    scale = 1.0 / math.sqrt(QK_NOPE_DIM + QK_ROPE_DIM)
    nb = seq // Q_BLOCK
    blk = lambda t: jnp.moveaxis(t.reshape(bsz, nb, Q_BLOCK, *t.shape[2:]), 1, 0)
    key_idx = jnp.arange(seq)

    def attend(args):
        qn, qp, b = args
        s = jnp.einsum('bqhd,bkhd->bhqk', qn, k_nope) + jnp.einsum('bqhr,bkr->bhqk', qp, k_pe)
        s = s.astype(jnp.float32) * scale
        q_idx = b * Q_BLOCK + jnp.arange(Q_BLOCK)
        s = jnp.where(key_idx[None, :] <= q_idx[:, None], s, jnp.finfo(jnp.float32).min)
        p = jax.nn.softmax(s, axis=-1).astype(dt)
        return jnp.einsum('bhqk,bkhd->bqhd', p, v)

    o = lax.map(attend, (blk(q_nope), blk(q_pe), jnp.arange(nb)))
    o = jnp.moveaxis(o, 0, 1).reshape(bsz, seq, B_WIDTH)
    return (o * jax.nn.silu(z)) @ w_out_b


def setup_inputs(seed: int = 0) -> dict:
    key = jax.random.key(seed)
    ks = jax.random.split(key, 24)
    n = lambda i, shape, s=1.0: s * jax.random.normal(ks[i], shape, jnp.float32)
    x = n(0, (BATCH, SEQ, D_MODEL))
    positions = (jnp.arange(SEQ, dtype=jnp.int32)[None, :]
                 + jax.random.randint(ks[1], (BATCH, 1), 0, 1024, dtype=jnp.int32))
    return {
        'x': x,
        'positions': positions,
        'g_pre': 1.0 + n(2, (D_MODEL,), 0.05),
        'w_in': n(3, (D_MODEL, IN_COLS), D_MODEL ** -0.5),
        'b_gate': n(4, (N_BRANCHES * D_MODEL,), 0.01),
        'mu_shift': jax.random.uniform(ks[5], (SHIFT_COLS,), jnp.float32),
        'w0': n(6, (A_WIDTH,), 0.5),
        'w_decay_up': n(7, (DECAY_LORA, A_WIDTH), 0.1),
        'a0': n(8, (A_WIDTH,), 0.5),
        'w_iclr_up': n(9, (ICLR_LORA, A_WIDTH), 0.5 * ICLR_LORA ** -0.5),
        'k_k': 0.85 + n(10, (A_WIDTH,), 0.05),
        'k_a': 1.0 + n(11, (A_WIDTH,), 0.05),
        'r_k': n(12, (A_HEADS, A_HEAD_DIM), 0.1),
        'gn_gain': 1.0 + n(13, (A_WIDTH,), 0.05),
        'gn_bias': n(14, (A_WIDTH,), 0.01),
        'w_out_a': n(15, (A_WIDTH, D_MODEL), A_WIDTH ** -0.5),
        'g_q': 1.0 + n(16, (Q_LORA_RANK,), 0.05),
        'w_uq': n(17, (Q_LORA_RANK, B_HEADS * (QK_NOPE_DIM + QK_ROPE_DIM)), Q_LORA_RANK ** -0.5),
        'g_kv': 1.0 + n(18, (KV_LORA_RANK,), 0.05),
        'w_ukv': n(19, (KV_LORA_RANK, B_HEADS * (QK_NOPE_DIM + V_HEAD_DIM)), KV_LORA_RANK ** -0.5),
        'w_out_b': n(20, (B_WIDTH, D_MODEL), B_WIDTH ** -0.5),
        'w_o': n(21, (D_MODEL, D_MODEL), D_MODEL ** -0.5),
        'g_post': 1.0 + n(22, (D_MODEL,), 0.05),
    }


def reference(x, positions, g_pre, w_in, b_gate, mu_shift, w0, w_decay_up, a0, w_iclr_up, k_k, k_a, r_k,
              gn_gain, gn_bias, w_out_a, g_q, w_uq, g_kv, w_ukv, w_out_b, w_o, g_post):
    h = x
    for _ in range(DEPTH):
        u = rmsnorm(h, g_pre)
        proj = u @ w_in
        feat_a, z_a, c_q, c_kv, k_pe, z_b, gate_logits = split_cols(
            proj, [SHIFT_COLS, A_WIDTH, Q_LORA_RANK, KV_LORA_RANK, QK_ROPE_DIM, B_WIDTH, N_BRANCHES * D_MODEL])
        y_a = rwkv7_branch(feat_a, z_a, mu_shift, w0, w_decay_up, a0, w_iclr_up, k_k, k_a, r_k,
                           gn_gain, gn_bias, w_out_a)
        y_b = mla_branch(c_q, c_kv, k_pe, z_b, positions, g_q, w_uq, g_kv, w_ukv, w_out_b)
        gates = jax.nn.sigmoid(gate_logits + b_gate)
        g_a, g_b = gates[..., :D_MODEL], gates[..., D_MODEL:]
        merged = g_a * y_a + g_b * y_b
        h = h + rmsnorm(merged @ w_o, g_post)
    return h
```

```python
import functools
import math

import jax
import jax.numpy as jnp
import numpy as np
from jax import lax
from jax.experimental import pallas as pl
from jax.experimental.pallas import tpu as pltpu

D_MODEL = 1024
A_HEADS = 8
A_HEAD_DIM = 64
A_WIDTH = A_HEADS * A_HEAD_DIM
DECAY_LORA = 64
ICLR_LORA = 64
DECAY_SCALE = 0.6065306597
GN_EPS = 64e-5
B_HEADS = 8
QK_NOPE_DIM = 64
QK_ROPE_DIM = 32
V_HEAD_DIM = 64
Q_LORA_RANK = 256
KV_LORA_RANK = 128
B_WIDTH = B_HEADS * V_HEAD_DIM
ROPE_THETA = 10000.0
NORM_EPS = 1e-6
SHIFT_COLS = 3 * A_WIDTH + DECAY_LORA + ICLR_LORA

LANES = 128
PAIR = 2 * A_HEAD_DIM
N_PAIRS = A_HEADS // 2
CHUNK = 64
LAT_COLS = 512
PROJ_COLS = SHIFT_COLS + A_WIDTH + LAT_COLS + B_WIDTH + 2 * D_MODEL
VMEM_LIMIT = 52 * 1024 * 1024

F32 = jnp.float32
BF16 = jnp.bfloat16
HI = lax.Precision.HIGHEST


def _dot(a, b):
    return jnp.dot(a.astype(BF16), b.astype(BF16), preferred_element_type=F32)


def _dot_nt(a, b):
    return lax.dot_general(a.astype(BF16), b.astype(BF16), (((1,), (1,)), ((), ())),
                           preferred_element_type=F32)


def _sigmoid(x):
    return 1.0 / (1.0 + jnp.exp(-x))


def _inproj_kernel(x_ref, g_ref, w_ref, feat_ref, za_ref, lat_ref, zb_ref, gl_ref):
    x = x_ref[...]
    ms = jnp.mean(x * x, axis=-1, keepdims=True)
    u = (x * lax.rsqrt(ms + NORM_EPS) * g_ref[...]).astype(BF16)
    off = 0
    for ref in (feat_ref, za_ref, lat_ref, zb_ref, gl_ref):
        n = ref.shape[-1]
        ref[...] = jnp.dot(u, w_ref[:, off:off + n], preferred_element_type=F32)
        off += n


def _inproj(x2, g_pre, w_p, tm):
    t = x2.shape[0]
    widths = (SHIFT_COLS, A_WIDTH, LAT_COLS, B_WIDTH, 2 * D_MODEL)
    return pl.pallas_call(
        _inproj_kernel,
        grid=(t // tm,),
        in_specs=[
            pl.BlockSpec((tm, D_MODEL), lambda i: (i, 0)),
            pl.BlockSpec((1, D_MODEL), lambda i: (0, 0)),
            pl.BlockSpec((D_MODEL, PROJ_COLS), lambda i: (0, 0)),
        ],
        out_specs=[pl.BlockSpec((tm, n), lambda i: (i, 0)) for n in widths],
        out_shape=[jax.ShapeDtypeStruct((t, n), F32) for n in widths],
        compiler_params=pltpu.CompilerParams(
            dimension_semantics=("arbitrary",), vmem_limit_bytes=VMEM_LIMIT),
        name="inproj",
    )(x2, g_pre, w_p)


def _rwkv_kernel(fr_ref, fk_ref, fv_ref, fl_ref, za_ref,
                 mur_ref, muk_ref, muv_ref, mul_ref,
                 w0_ref, a0_ref, kk_ref, ka_ref, rk_ref, gg_ref, gb_ref,
                 wd_ref, wa_ref, out_ref, carry_ref, state_ref):
    tt = fr_ref.shape[0]

    @pl.when(pl.program_id(2) == 0)
    def _():
        carry_ref[...] = jnp.zeros_like(carry_ref)
        state_ref[...] = jnp.zeros_like(state_ref)

    row_t = lax.broadcasted_iota(jnp.int32, (tt, LANES), 0)

    def shifted(ref, slot, mu_ref):
        cur = ref[...]
        prev = pltpu.roll(cur, 1, 0)
        prev = jnp.where(row_t == 0, carry_ref[8 * slot:8 * slot + 1, :], prev)
        return cur + mu_ref[...] * (prev - cur)

    r = shifted(fr_ref, 0, mur_ref)
    k = shifted(fk_ref, 1, muk_ref)
    v = shifted(fv_ref, 2, muv_ref)
    la = shifted(fl_ref, 3, mul_ref)
    for slot, ref in enumerate((fr_ref, fk_ref, fv_ref, fl_ref)):
        carry_ref[8 * slot:8 * slot + 1, :] = ref[tt - 1:tt, :]

    ri = lax.broadcasted_iota(jnp.int32, (LANES, LANES), 0)
    ci = lax.broadcasted_iota(jnp.int32, (LANES, LANES), 1)
    same_head = (ri // A_HEAD_DIM) == (ci // A_HEAD_DIM)
    strict_bd = same_head & (ci < ri)
    incl_bd = same_head & (ci <= ri)
    eye = (ri == ci).astype(F32)
    head_ones = same_head.astype(F32)
    lane = lax.broadcasted_iota(jnp.int32, (1, LANES), 1)
    m_a = (lane < A_HEAD_DIM).astype(F32)
    m_b = 1.0 - m_a
    tr = lax.broadcasted_iota(jnp.int32, (CHUNK, CHUNK), 0)
    tc = lax.broadcasted_iota(jnp.int32, (CHUNK, CHUNK), 1)
    tri = (tc <= tr).astype(F32)

    def head_sum(x):
        return jnp.dot(x, head_ones, precision=HI, preferred_element_type=F32)

    w_logit = w0_ref[...] + _dot(jnp.tanh(la), wd_ref[...])
    lw = -DECAY_SCALE * _sigmoid(w_logit)
    a = _sigmoid(a0_ref[...] + _dot(la, wa_ref[...]))
    kk = k * kk_ref[...]
    kk = kk / jnp.maximum(jnp.sqrt(head_sum(kk * kk)), 1e-12)
    k2 = k * (1.0 + (a - 1.0) * ka_ref[...])
    b = kk * a
    c = -kk
    bonus = head_sum(r * k2 * rk_ref[...]) * v

    def stack(z):
        return jnp.concatenate([z * m_a, z * m_b], axis=0)

    def unstack(z):
        return z[:CHUNK] + z[CHUNK:]

    s = state_ref[...]
    ys = []
    for ch in range(tt // CHUNK):
        sl = slice(ch * CHUNK, (ch + 1) * CHUNK)
        lwc, rc, kc, vc, bc, cc = lw[sl], r[sl], k2[sl], v[sl], b[sl], c[sl]
        cum = jnp.dot(tri, lwc, precision=HI, preferred_element_type=F32)
        last = cum[CHUNK - 1:CHUNK, :]
        p_inv = jnp.exp(-cum)
        ct = cc * jnp.exp(cum - lwc)
        rt = rc * jnp.exp(cum)
        bt = bc * p_inv
        kt = kc * p_inv
        to_end = jnp.exp(last - cum)
        g = _dot_nt(jnp.concatenate([stack(ct), stack(rt)], axis=0),
                    jnp.concatenate([bt, bt, kt, kt], axis=0))
        nb = jnp.where(strict_bd, g[:PAIR, :PAIR], 0.0)
        nk = jnp.where(strict_bd, g[:PAIR, PAIR:], 0.0)
        mb = jnp.where(incl_bd, g[PAIR:, :PAIR], 0.0)
        mk = jnp.where(incl_bd, g[PAIR:, PAIR:], 0.0)
        t_inv = eye + nb
        npow = nb
        for _ in range(int(math.log2(CHUNK)) - 1):
            npow = _dot(npow, npow)
            t_inv = t_inv + _dot(npow, t_inv)
        vs = stack(vc)
        x = _dot_nt(ct, s) + unstack(_dot(nk, vs))
        u = unstack(_dot(t_inv, stack(x)))
        y = _dot_nt(rt, s) + unstack(
            _dot(jnp.concatenate([mb, mk], axis=1), jnp.concatenate([stack(u), vs], axis=0)))
        uv_t = jnp.concatenate([u, vc], axis=0).T
        s = s * jnp.exp(last) + _dot(uv_t, jnp.concatenate([bc * to_end, kc * to_end], axis=0))
        s = jnp.where(same_head, s, 0.0)
        ys.append(y)
    state_ref[...] = s

    y = jnp.concatenate(ys, axis=0)
    mean = head_sum(y) * (1.0 / A_HEAD_DIM)
    d = y - mean
    var = head_sum(d * d) * (1.0 / A_HEAD_DIM)
    y = d * lax.rsqrt(var + GN_EPS) * gg_ref[...] + gb_ref[...] + bonus
    z = za_ref[...]
    out_ref[...] = (y * (z * _sigmoid(z))).astype(out_ref.dtype)


def _rwkv(feat, za, mu, w0, a0, k_k, k_a, r_k, gn_gain, gn_bias, wd_p, wa_p, bsz, seq, tt):
    nt = seq // tt
    row = lambda b, p, t: b * nt + t
    fspec = lambda col0: pl.BlockSpec((tt, LANES), lambda b, p, t: (row(b, p, t), col0 + p))
    pspec = pl.BlockSpec((1, LANES), lambda b, p, t: (0, p))
    mspec = lambda col0: pl.BlockSpec((1, LANES), lambda b, p, t: (0, col0 + p))
    return pl.pallas_call(
        _rwkv_kernel,
        grid=(bsz, N_PAIRS, nt),
        in_specs=[
            fspec(0), fspec(N_PAIRS), fspec(2 * N_PAIRS),
            pl.BlockSpec((tt, LANES), lambda b, p, t: (row(b, p, t), 3 * N_PAIRS)),
            pl.BlockSpec((tt, LANES), lambda b, p, t: (row(b, p, t), p)),
            mspec(0), mspec(N_PAIRS), mspec(2 * N_PAIRS),
            pl.BlockSpec((1, LANES), lambda b, p, t: (0, 3 * N_PAIRS)),
            pspec, pspec, pspec, pspec, pspec, pspec, pspec,
            pl.BlockSpec((LANES, LANES), lambda b, p, t: (0, p)),
            pl.BlockSpec((LANES, LANES), lambda b, p, t: (0, p)),
        ],
        out_specs=pl.BlockSpec((tt, LANES), lambda b, p, t: (row(b, p, t), p)),
        out_shape=jax.ShapeDtypeStruct((bsz * seq, A_WIDTH), BF16),
        scratch_shapes=[pltpu.VMEM((32, LANES), F32), pltpu.VMEM((LANES, LANES), F32)],
        compiler_params=pltpu.CompilerParams(
            dimension_semantics=("arbitrary", "arbitrary", "arbitrary"),
            vmem_limit_bytes=VMEM_LIMIT),
        name="rwkv7",
    )(feat, feat, feat, feat, za, mu, mu, mu, mu, w0, a0, k_k, k_a, r_k, gn_gain, gn_bias,
      wd_p, wa_p)


def _mla_prep_kernel(lat_ref, pos_ref, invf_ref, gq_ref, gkv_ref, wq_ref, wk_ref, wv_ref,
                     q_ref, k_ref, v_ref):
    tm = lat_ref.shape[0]
    lat = lat_ref[...]
    pos = pos_ref[0].astype(F32)
    pos_col = jnp.broadcast_to(pos, (LANES, tm)).T
    ang = pos_col * invf_ref[...]
    cos, sin = jnp.cos(ang), jnp.sin(ang)
    lane = lax.broadcasted_iota(jnp.int32, (1, LANES), 1)
    rope_lane = (lane >= QK_NOPE_DIM) & (lane < QK_NOPE_DIM + QK_ROPE_DIM)
    cos_q = jnp.where(lane < QK_NOPE_DIM + QK_ROPE_DIM, cos, 0.0)
    cos_k = jnp.where(rope_lane, cos, 0.0)

    def rms(x, g):
        return x * lax.rsqrt(jnp.mean(x * x, axis=-1, keepdims=True) + NORM_EPS) * g

    def rot(blk, cos_t):
        return blk * cos_t + pltpu.roll(blk, LANES - QK_ROPE_DIM, 1) * sin

    scale = 1.0 / math.sqrt(QK_NOPE_DIM + QK_ROPE_DIM)
    nq = rms(lat[:, :Q_LORA_RANK], gq_ref[...])
    q_raw = _dot(nq, wq_ref[...])
    nkv = rms(lat[:, Q_LORA_RANK:Q_LORA_RANK + KV_LORA_RANK], gkv_ref[...])
    k_nope = _dot(nkv, wk_ref[...])
    v_ref[...] = _dot(nkv, wv_ref[...]).astype(v_ref.dtype)
    k_pe = rot(lat[:, Q_LORA_RANK + KV_LORA_RANK:], cos_k)
    for h in range(B_HEADS):
        sl = slice(h * LANES, (h + 1) * LANES)
        q_ref[:, sl] = (rot(q_raw[:, sl], cos_q) * scale).astype(q_ref.dtype)
        k_ref[:, sl] = (k_nope[:, sl] + k_pe).astype(k_ref.dtype)


def _mla_prep(lat, pos3, invf, g_q, g_kv, wq_p, wk_p, wv_p, tm):
    t = lat.shape[0]
    const = lambda shape: pl.BlockSpec(shape, lambda i: (0,) * len(shape))
    return pl.pallas_call(
        _mla_prep_kernel,
        grid=(t // tm,),
        in_specs=[
            pl.BlockSpec((tm, LAT_COLS), lambda i: (i, 0)),
            pl.BlockSpec((1, 1, tm), lambda i: (i, 0, 0)),
            const((1, LANES)), const((1, Q_LORA_RANK)), const((1, KV_LORA_RANK)),
            const((Q_LORA_RANK, B_HEADS * LANES)), const((KV_LORA_RANK, B_HEADS * LANES)),
            const((KV_LORA_RANK, B_WIDTH)),
        ],
        out_specs=[
            pl.BlockSpec((tm, B_HEADS * LANES), lambda i: (i, 0)),
            pl.BlockSpec((tm, B_HEADS * LANES), lambda i: (i, 0)),
            pl.BlockSpec((tm, B_WIDTH), lambda i: (i, 0)),
        ],
        out_shape=[
            jax.ShapeDtypeStruct((t, B_HEADS * LANES), BF16),
            jax.ShapeDtypeStruct((t, B_HEADS * LANES), BF16),
            jax.ShapeDtypeStruct((t, B_WIDTH), BF16),
        ],
        compiler_params=pltpu.CompilerParams(
            dimension_semantics=("arbitrary",), vmem_limit_bytes=VMEM_LIMIT),
        name="mla_prep",
    )(lat, pos3, invf, g_q, g_kv, wq_p, wk_p, wv_p)


NEG_BIG = -1e30


def _attn_kernel(q_ref, k_ref, v_ref, zb_ref, out_ref):
    tq = q_ref.shape[1]
    i = pl.program_id(2)
    q0 = q_ref[0, :, :LANES]
    q1 = q_ref[0, :, LANES:]
    lane = lax.broadcasted_iota(jnp.int32, (1, LANES), 1)
    first = lane < V_HEAD_DIM

    def step(j, carry, masked):
        m0, l0, m1, l1, acc = carry
        start = pl.multiple_of(j * tq, tq)
        kb = k_ref[0, pl.ds(start, tq), :]
        vb = v_ref[0, pl.ds(start, tq), :]
        outs = []
        stats = []
        for qh, kh, m, l in ((q0, kb[:, :LANES], m0, l0), (q1, kb[:, LANES:], m1, l1)):
            s = lax.dot_general(qh, kh, (((1,), (1,)), ((), ())), preferred_element_type=F32)
            if masked:
                qi = lax.broadcasted_iota(jnp.int32, (tq, tq), 0)
                ki = lax.broadcasted_iota(jnp.int32, (tq, tq), 1)
                s = jnp.where(ki <= qi, s, NEG_BIG)
            m_new = jnp.maximum(m, jnp.max(s, axis=-1, keepdims=True))
            alpha = jnp.exp(m - m_new)
            p = jnp.exp(s - m_new)
            l_new = alpha * l + jnp.sum(p, axis=-1, keepdims=True)
            outs.append(jnp.dot(p.astype(BF16), vb, preferred_element_type=F32))
            stats.append((m_new, l_new, alpha))
        (m0, l0, al0), (m1, l1, al1) = stats
        acc = acc * jnp.where(first, al0, al1) + jnp.where(first, outs[0], outs[1])
        return m0, l0, m1, l1, acc

    init = (jnp.full((tq, 1), NEG_BIG, F32), jnp.zeros((tq, 1), F32),
            jnp.full((tq, 1), NEG_BIG, F32), jnp.zeros((tq, 1), F32),
            jnp.zeros((tq, LANES), F32))
    carry = lax.fori_loop(0, i, functools.partial(step, masked=False), init)
    m0, l0, m1, l1, acc = step(i, carry, masked=True)
    o = acc / jnp.where(first, l0, l1)
    z = zb_ref[0]
    out_ref[0] = (o * (z * _sigmoid(z))).astype(out_ref.dtype)


def _attention(q, k, v, zb, tq):
    bsz, seq, _ = q.shape
    return pl.pallas_call(
        _attn_kernel,
        grid=(bsz, B_HEADS // 2, seq // tq),
        in_specs=[
            pl.BlockSpec((1, tq, 2 * LANES), lambda b, p, i: (b, i, p)),
            pl.BlockSpec((1, seq, 2 * LANES), lambda b, p, i: (b, 0, p)),
            pl.BlockSpec((1, seq, LANES), lambda b, p, i: (b, 0, p)),
            pl.BlockSpec((1, tq, LANES), lambda b, p, i: (b, i, p)),
        ],
        out_specs=pl.BlockSpec((1, tq, LANES), lambda b, p, i: (b, i, p)),
        out_shape=jax.ShapeDtypeStruct((bsz, seq, B_WIDTH), BF16),
        compiler_params=pltpu.CompilerParams(
            dimension_semantics=("arbitrary", "arbitrary", "arbitrary"),
            vmem_limit_bytes=VMEM_LIMIT),
        name="mla_attn",
    )(q, k, v, zb)


def _out_kernel(ya_ref, yb_ref, gl_ref, x_ref, bg_ref, woa_ref, wob_ref, wo_ref, gp_ref, out_ref):
    y_a = jnp.dot(ya_ref[...], woa_ref[...], preferred_element_type=F32)
    y_b = jnp.dot(yb_ref[...], wob_ref[...], preferred_element_type=F32)
    gates = _sigmoid(gl_ref[...] + bg_ref[...])
    merged = gates[:, :D_MODEL] * y_a + gates[:, D_MODEL:] * y_b
    o = _dot(merged, wo_ref[...])
    ms = jnp.mean(o * o, axis=-1, keepdims=True)
    out_ref[...] = x_ref[...] + o * lax.rsqrt(ms + NORM_EPS) * gp_ref[...]


def _out_proj(ya, yb, gl, x2, b_gate, woa, wob, wo, g_post, tm):
    t = x2.shape[0]
    const = lambda shape: pl.BlockSpec(shape, lambda i: (0,) * len(shape))
    rows = lambda n: pl.BlockSpec((tm, n), lambda i: (i, 0))
    return pl.pallas_call(
        _out_kernel,
        grid=(t // tm,),
        in_specs=[
            rows(A_WIDTH), rows(B_WIDTH), rows(2 * D_MODEL), rows(D_MODEL),
            const((1, 2 * D_MODEL)), const((A_WIDTH, D_MODEL)), const((B_WIDTH, D_MODEL)),
            const((D_MODEL, D_MODEL)), const((1, D_MODEL)),
        ],
        out_specs=rows(D_MODEL),
        out_shape=jax.ShapeDtypeStruct((t, D_MODEL), F32),
        compiler_params=pltpu.CompilerParams(
            dimension_semantics=("arbitrary",), vmem_limit_bytes=VMEM_LIMIT),
        name="out_proj",
    )(ya, yb, gl, x2, b_gate, woa, wob, wo, g_post)


def _rope_partner(w_pe):
    half = QK_ROPE_DIM // 2
    return jnp.concatenate([-w_pe[..., half:], w_pe[..., :half]], axis=-1)


def _prep_weights(w_in, w_decay_up, w_iclr_up, w_uq, w_ukv):
    o = np.cumsum([0, SHIFT_COLS, A_WIDTH, Q_LORA_RANK, KV_LORA_RANK, QK_ROPE_DIM, B_WIDTH])
    w_pe = w_in[:, o[4]:o[5]]
    kpe_blk = jnp.concatenate(
        [jnp.zeros((D_MODEL, QK_NOPE_DIM), F32), w_pe, _rope_partner(w_pe)], axis=-1)
    w_p = jnp.concatenate(
        [w_in[:, :o[4]], kpe_blk, w_in[:, o[5]:]], axis=-1).astype(BF16)
    zeros_lora = jnp.zeros((DECAY_LORA, A_WIDTH), F32)
    wd_p = jnp.concatenate([w_decay_up, zeros_lora], axis=0).astype(BF16)
    wa_p = jnp.concatenate([zeros_lora, w_iclr_up], axis=0).astype(BF16)
    wq = w_uq.reshape(Q_LORA_RANK, B_HEADS, QK_NOPE_DIM + QK_ROPE_DIM)
    wq_p = jnp.concatenate(
        [wq, _rope_partner(wq[..., QK_NOPE_DIM:])], axis=-1).reshape(Q_LORA_RANK, -1).astype(BF16)
    wkv = w_ukv.reshape(KV_LORA_RANK, B_HEADS, QK_NOPE_DIM + V_HEAD_DIM)
    wk_p = jnp.concatenate(
        [wkv[..., :QK_NOPE_DIM], jnp.zeros_like(wkv[..., :QK_NOPE_DIM])],
        axis=-1).reshape(KV_LORA_RANK, -1).astype(BF16)
    wv_p = wkv[..., QK_NOPE_DIM:].reshape(KV_LORA_RANK, -1).astype(BF16)
    return w_p, wd_p, wa_p, wq_p, wk_p, wv_p


def kernel(x, positions, g_pre, w_in, b_gate, mu_shift, w0, w_decay_up, a0, w_iclr_up, k_k, k_a,
           r_k, gn_gain, gn_bias, w_out_a, g_q, w_uq, g_kv, w_ukv, w_out_b, w_o, g_post):
    bsz, seq, _ = x.shape
    t = bsz * seq
    tm = 256
    row = lambda a: a.reshape(1, -1).astype(F32)
    w_p, wd_p, wa_p, wq_p, wk_p, wv_p = _prep_weights(w_in, w_decay_up, w_iclr_up, w_uq, w_ukv)
    x2 = x.reshape(t, D_MODEL)

    feat, za, lat, zb, gl = _inproj(x2, row(g_pre), w_p, tm)

    ya = _rwkv(feat, za, row(mu_shift), row(w0), row(a0), row(k_k), row(k_a), row(r_k),
               row(gn_gain), row(gn_bias), wd_p, wa_p, bsz, seq, 256)

    lane = np.arange(LANES)
    freq = np.where((lane >= QK_NOPE_DIM) & (lane < QK_NOPE_DIM + QK_ROPE_DIM),
                    ROPE_THETA ** (-((lane - QK_NOPE_DIM) % (QK_ROPE_DIM // 2)) * 2.0 / QK_ROPE_DIM),
                    0.0)
    invf = jnp.asarray(freq.reshape(1, LANES), F32)
    pos3 = positions.reshape(t // tm, 1, tm)
    q, k, v = _mla_prep(lat, pos3, invf, row(g_q), row(g_kv), wq_p, wk_p, wv_p, tm)
    yb = _attention(q.reshape(bsz, seq, -1), k.reshape(bsz, seq, -1), v.reshape(bsz, seq, -1),
                    zb.reshape(bsz, seq, -1), 256)

    out = _out_proj(ya, yb.reshape(t, B_WIDTH), gl, x2, row(b_gate), w_out_a.astype(BF16),
                    w_out_b.astype(BF16), w_o.astype(BF16), row(g_post), tm)
    return out.reshape(bsz, seq, D_MODEL)
```

```python
import functools
import math

import jax
import jax.numpy as jnp
import numpy as np
from jax import lax
from jax.experimental import pallas as pl
from jax.experimental.pallas import tpu as pltpu

D_MODEL = 1024
A_HEADS = 8
A_HEAD_DIM = 64
A_WIDTH = A_HEADS * A_HEAD_DIM
DECAY_LORA = 64
ICLR_LORA = 64
DECAY_SCALE = 0.6065306597
GN_EPS = 64e-5
B_HEADS = 8
QK_NOPE_DIM = 64
QK_ROPE_DIM = 32
V_HEAD_DIM = 64
Q_LORA_RANK = 256
KV_LORA_RANK = 128
B_WIDTH = B_HEADS * V_HEAD_DIM
ROPE_THETA = 10000.0
NORM_EPS = 1e-6
SHIFT_COLS = 3 * A_WIDTH + DECAY_LORA + ICLR_LORA

LANES = 128
PAIR = 2 * A_HEAD_DIM
N_PAIRS = A_HEADS // 2
CHUNK = 64
LAT_COLS = 512
PROJ_COLS = SHIFT_COLS + A_WIDTH + LAT_COLS + B_WIDTH + 2 * D_MODEL
VMEM_LIMIT = 52 * 1024 * 1024

F32 = jnp.float32
BF16 = jnp.bfloat16
HI = lax.Precision.HIGHEST


def _dot(a, b):
    return jnp.dot(a.astype(BF16), b.astype(BF16), preferred_element_type=F32)


def _dot_nt(a, b):
    return lax.dot_general(a.astype(BF16), b.astype(BF16), (((1,), (1,)), ((), ())),
                           preferred_element_type=F32)


def _sigmoid(x):
    return 1.0 / (1.0 + jnp.exp(-x))


def _inproj_kernel(x_ref, g_ref, w_ref, feat_ref, za_ref, lat_ref, zb_ref, gl_ref):
    x = x_ref[...]
    ms = jnp.mean(x * x, axis=-1, keepdims=True)
    u = (x * lax.rsqrt(ms + NORM_EPS) * g_ref[...]).astype(BF16)
    off = 0
    for ref in (feat_ref, za_ref, lat_ref, zb_ref, gl_ref):
        n = ref.shape[-1]
        ref[...] = jnp.dot(u, w_ref[:, off:off + n], preferred_element_type=F32)
        off += n


def _inproj(x2, g_pre, w_p, tm):
    t = x2.shape[0]
    widths = (SHIFT_COLS, A_WIDTH, LAT_COLS, B_WIDTH, 2 * D_MODEL)
    return pl.pallas_call(
        _inproj_kernel,
        grid=(t // tm,),
        in_specs=[
            pl.BlockSpec((tm, D_MODEL), lambda i: (i, 0)),
            pl.BlockSpec((1, D_MODEL), lambda i: (0, 0)),
            pl.BlockSpec((D_MODEL, PROJ_COLS), lambda i: (0, 0)),
        ],
        out_specs=[pl.BlockSpec((tm, n), lambda i: (i, 0)) for n in widths],
        out_shape=[jax.ShapeDtypeStruct((t, n), F32) for n in widths],
        compiler_params=pltpu.CompilerParams(
            dimension_semantics=("arbitrary",), vmem_limit_bytes=VMEM_LIMIT),
        name="inproj",
    )(x2, g_pre, w_p)


def _rwkv_kernel(fr_ref, fk_ref, fv_ref, fl_ref, za_ref,
                 mur_ref, muk_ref, muv_ref, mul_ref,
                 w0_ref, a0_ref, kk_ref, ka_ref, rk_ref, gg_ref, gb_ref,
                 wd_ref, wa_ref, out_ref, carry_ref, state_ref):
    tt = fr_ref.shape[0]

    @pl.when(pl.program_id(2) == 0)
    def _():
        carry_ref[...] = jnp.zeros_like(carry_ref)
        state_ref[...] = jnp.zeros_like(state_ref)

    row_t = lax.broadcasted_iota(jnp.int32, (tt, LANES), 0)

    def shifted(ref, slot, mu_ref):
        cur = ref[...]
        prev = pltpu.roll(cur, 1, 0)
        prev = jnp.where(row_t == 0, carry_ref[8 * slot:8 * slot + 1, :], prev)
        return cur + mu_ref[...] * (prev - cur)

    r = shifted(fr_ref, 0, mur_ref)
    k = shifted(fk_ref, 1, muk_ref)
    v = shifted(fv_ref, 2, muv_ref)
    la = shifted(fl_ref, 3, mul_ref)
    for slot, ref in enumerate((fr_ref, fk_ref, fv_ref, fl_ref)):
        carry_ref[8 * slot:8 * slot + 1, :] = ref[tt - 1:tt, :]

    ri = lax.broadcasted_iota(jnp.int32, (LANES, LANES), 0)
    ci = lax.broadcasted_iota(jnp.int32, (LANES, LANES), 1)
    same_head = (ri // A_HEAD_DIM) == (ci // A_HEAD_DIM)
    strict_bd = same_head & (ci < ri)
    incl_bd = same_head & (ci <= ri)
    eye = (ri == ci).astype(F32)
    head_ones = same_head.astype(F32)
    lane = lax.broadcasted_iota(jnp.int32, (1, LANES), 1)
    m_a = (lane < A_HEAD_DIM).astype(F32)
    m_b = 1.0 - m_a
    tr = lax.broadcasted_iota(jnp.int32, (CHUNK, CHUNK), 0)
    tc = lax.broadcasted_iota(jnp.int32, (CHUNK, CHUNK), 1)
    tri = (tc <= tr).astype(F32)

    def head_sum(x):
        return jnp.dot(x, head_ones, precision=HI, preferred_element_type=F32)

    w_logit = w0_ref[...] + _dot(jnp.tanh(la), wd_ref[...])
    lw = -DECAY_SCALE * _sigmoid(w_logit)
    a = _sigmoid(a0_ref[...] + _dot(la, wa_ref[...]))
    kk = k * kk_ref[...]
    kk = kk / jnp.maximum(jnp.sqrt(head_sum(kk * kk)), 1e-12)
    k2 = k * (1.0 + (a - 1.0) * ka_ref[...])
    b = kk * a
    c = -kk
    bonus = head_sum(r * k2 * rk_ref[...]) * v

    def stack(z):
        return jnp.concatenate([z * m_a, z * m_b], axis=0)

    def unstack(z):
        return z[:CHUNK] + z[CHUNK:]

    s = state_ref[...]
    ys = []
    for ch in range(tt // CHUNK):
        sl = slice(ch * CHUNK, (ch + 1) * CHUNK)
        lwc, rc, kc, vc, bc, cc = lw[sl], r[sl], k2[sl], v[sl], b[sl], c[sl]
        cum = jnp.dot(tri, lwc, precision=HI, preferred_element_type=F32)
        last = cum[CHUNK - 1:CHUNK, :]
        p_inv = jnp.exp(-cum)
        ct = cc * jnp.exp(cum - lwc)
        rt = rc * jnp.exp(cum)
        bt = bc * p_inv
        kt = kc * p_inv
        to_end = jnp.exp(last - cum)
        g = _dot_nt(jnp.concatenate([stack(ct), stack(rt)], axis=0),
                    jnp.concatenate([bt, bt, kt, kt], axis=0))
        nb = jnp.where(strict_bd, g[:PAIR, :PAIR], 0.0)
        nk = jnp.where(strict_bd, g[:PAIR, PAIR:], 0.0)
        mb = jnp.where(incl_bd, g[PAIR:, :PAIR], 0.0)
        mk = jnp.where(incl_bd, g[PAIR:, PAIR:], 0.0)
        t_inv = eye + nb
        npow = nb
        for _ in range(int(math.log2(CHUNK)) - 1):
            npow = _dot(npow, npow)
            t_inv = t_inv + _dot(npow, t_inv)
        vs = stack(vc)
        x = _dot_nt(ct, s) + unstack(_dot(nk, vs))
        u = unstack(_dot(t_inv, stack(x)))
        y = _dot_nt(rt, s) + unstack(
            _dot(jnp.concatenate([mb, mk], axis=1), jnp.concatenate([stack(u), vs], axis=0)))
        uv_t = jnp.concatenate([u, vc], axis=0).T
        s = s * jnp.exp(last) + _dot(uv_t, jnp.concatenate([bc * to_end, kc * to_end], axis=0))
        s = jnp.where(same_head, s, 0.0)
        ys.append(y)
    state_ref[...] = s

    y = jnp.concatenate(ys, axis=0)
    mean = head_sum(y) * (1.0 / A_HEAD_DIM)
    d = y - mean
    var = head_sum(d * d) * (1.0 / A_HEAD_DIM)
    y = d * lax.rsqrt(var + GN_EPS) * gg_ref[...] + gb_ref[...] + bonus
    z = za_ref[...]
    out_ref[...] = (y * (z * _sigmoid(z))).astype(out_ref.dtype)


def _rwkv(feat, za, mu, w0, a0, k_k, k_a, r_k, gn_gain, gn_bias, wd_p, wa_p, bsz, seq, tt):
    nt = seq // tt
    row = lambda b, p, t: b * nt + t
    fspec = lambda col0: pl.BlockSpec((tt, LANES), lambda b, p, t: (row(b, p, t), col0 + p))
    pspec = pl.BlockSpec((1, LANES), lambda b, p, t: (0, p))
    mspec = lambda col0: pl.BlockSpec((1, LANES), lambda b, p, t: (0, col0 + p))
    return pl.pallas_call(
        _rwkv_kernel,
        grid=(bsz, N_PAIRS, nt),
        in_specs=[
            fspec(0), fspec(N_PAIRS), fspec(2 * N_PAIRS),
            pl.BlockSpec((tt, LANES), lambda b, p, t: (row(b, p, t), 3 * N_PAIRS)),
            pl.BlockSpec((tt, LANES), lambda b, p, t: (row(b, p, t), p)),
            mspec(0), mspec(N_PAIRS), mspec(2 * N_PAIRS),
            pl.BlockSpec((1, LANES), lambda b, p, t: (0, 3 * N_PAIRS)),
            pspec, pspec, pspec, pspec, pspec, pspec, pspec,
            pl.BlockSpec((LANES, LANES), lambda b, p, t: (0, p)),
            pl.BlockSpec((LANES, LANES), lambda b, p, t: (0, p)),
        ],
        out_specs=pl.BlockSpec((tt, LANES), lambda b, p, t: (row(b, p, t), p)),
        out_shape=jax.ShapeDtypeStruct((bsz * seq, A_WIDTH), BF16),
        scratch_shapes=[pltpu.VMEM((32, LANES), F32), pltpu.VMEM((LANES, LANES), F32)],
        compiler_params=pltpu.CompilerParams(
            dimension_semantics=("arbitrary", "arbitrary", "arbitrary"),
            vmem_limit_bytes=VMEM_LIMIT),
        name="rwkv7",
    )(feat, feat, feat, feat, za, mu, mu, mu, mu, w0, a0, k_k, k_a, r_k, gn_gain, gn_bias,
      wd_p, wa_p)


QK_SCALE = math.log2(math.e) / math.sqrt(QK_NOPE_DIM + QK_ROPE_DIM)


def _mla_prep_kernel(lat_ref, pos_ref, invf_ref, gq_ref, gkv_ref, wqt_ref, wk_ref, wvt_ref,
                     qt_ref, k_ref, vt_ref):
    lat = lat_ref[...]
    pos = pos_ref[0].astype(F32)
    ang_t = invf_ref[...] * pos
    cos_t, sin_t = jnp.cos(ang_t), jnp.sin(ang_t)
    feat = lax.broadcasted_iota(jnp.int32, (LANES, 1), 0)
    cos_q = jnp.where(feat < QK_NOPE_DIM + QK_ROPE_DIM, cos_t, 0.0)

    cq_t = jnp.concatenate([lat[:, j * LANES:(j + 1) * LANES].T
                            for j in range(Q_LORA_RANK // LANES)], axis=0)
    nq_t = cq_t * lax.rsqrt(jnp.mean(cq_t * cq_t, axis=0, keepdims=True) + NORM_EPS) * gq_ref[...]
    q_raw = _dot(wqt_ref[...], nq_t)
    for h in range(B_HEADS):
        sl = slice(h * LANES, (h + 1) * LANES)
        blk = q_raw[sl]
        rot = blk * cos_q + pltpu.roll(blk, LANES - QK_ROPE_DIM, 0) * sin_t
        qt_ref[0, sl, :] = (rot * QK_SCALE).astype(qt_ref.dtype)

    ckv = lat[:, Q_LORA_RANK:Q_LORA_RANK + KV_LORA_RANK]
    nkv = ckv * lax.rsqrt(jnp.mean(ckv * ckv, axis=-1, keepdims=True) + NORM_EPS) * gkv_ref[...]
    vt_ref[0] = _dot(wvt_ref[...], nkv.T).astype(vt_ref.dtype)
    k_nope = _dot(nkv, wk_ref[...])
    lane = lax.broadcasted_iota(jnp.int32, (1, LANES), 1)
    rope_lane = (lane >= QK_NOPE_DIM) & (lane < QK_NOPE_DIM + QK_ROPE_DIM)
    pe = lat[:, Q_LORA_RANK + KV_LORA_RANK:]
    k_pe = pe * jnp.where(rope_lane, cos_t.T, 0.0) + pltpu.roll(pe, LANES - QK_ROPE_DIM, 1) * sin_t.T
    for h in range(B_HEADS):
        sl = slice(h * LANES, (h + 1) * LANES)
        k_ref[:, sl] = (k_nope[:, sl] + k_pe).astype(k_ref.dtype)


def _mla_prep(lat, pos3, invf, g_q, g_kv, wqt_p, wk_p, wvt_p, bsz, seq, tm):
    t = lat.shape[0]
    nt = seq // tm
    const = lambda shape: pl.BlockSpec(shape, lambda i: (0,) * len(shape))
    return pl.pallas_call(
        _mla_prep_kernel,
        grid=(t // tm,),
        in_specs=[
            pl.BlockSpec((tm, LAT_COLS), lambda i: (i, 0)),
            pl.BlockSpec((1, 1, tm), lambda i: (i, 0, 0)),
            const((LANES, 1)), const((Q_LORA_RANK, 1)), const((1, KV_LORA_RANK)),
            const((B_HEADS * LANES, Q_LORA_RANK)), const((KV_LORA_RANK, B_HEADS * LANES)),
            const((B_WIDTH, KV_LORA_RANK)),
        ],
        out_specs=[
            pl.BlockSpec((1, B_HEADS * LANES, tm), lambda i: (i // nt, 0, i % nt)),
            pl.BlockSpec((tm, B_HEADS * LANES), lambda i: (i, 0)),
            pl.BlockSpec((1, B_WIDTH, tm), lambda i: (i // nt, 0, i % nt)),
        ],
        out_shape=[
            jax.ShapeDtypeStruct((bsz, B_HEADS * LANES, seq), BF16),
            jax.ShapeDtypeStruct((t, B_HEADS * LANES), BF16),
            jax.ShapeDtypeStruct((bsz, B_WIDTH, seq), BF16),
        ],
        compiler_params=pltpu.CompilerParams(
            dimension_semantics=("arbitrary",), vmem_limit_bytes=VMEM_LIMIT),
        name="mla_prep",
    )(lat, pos3, invf, g_q, g_kv, wqt_p, wk_p, wvt_p)


NEG_BIG = -1e30
ATT_TK = 512


def _attn_kernel(qt_ref, k_ref, vt_ref, zb_ref, out_ref):
    tq = qt_ref.shape[2]
    tk = ATT_TK
    i = pl.program_id(2)
    q_t = (qt_ref[0, :LANES, :], qt_ref[0, LANES:, :])

    def step(j, carry, q0=None):
        start = pl.multiple_of(j * tk, tk)
        kb = k_ref[0, pl.ds(start, tk), :]
        vb = vt_ref[0, :, pl.ds(start, tk)]
        new = []
        for h in range(2):
            m, l, acc = carry[h] if q0 is None else [c[:, q0:] for c in carry[h]]
            q_h = q_t[h] if q0 is None else q_t[h][:, q0:]
            s = jnp.dot(kb[:, h * LANES:(h + 1) * LANES], q_h, preferred_element_type=F32)
            if q0 is not None:
                key = lax.broadcasted_iota(jnp.int32, s.shape, 0)
                query = lax.broadcasted_iota(jnp.int32, s.shape, 1)
                s = jnp.where(key <= query, s, NEG_BIG)
            m_new = jnp.maximum(m, jnp.max(s, axis=0, keepdims=True))
            alpha = jnp.exp2(m - m_new)
            p = jnp.exp2(s - m_new)
            l_new = alpha * l + jnp.sum(p, axis=0, keepdims=True)
            pv = jnp.dot(vb[h * V_HEAD_DIM:(h + 1) * V_HEAD_DIM, :], p.astype(BF16),
                         preferred_element_type=F32)
            upd = (m_new, l_new, acc * alpha + pv)
            if q0:
                upd = tuple(jnp.concatenate([c[:, :q0], u], axis=1) for c, u in zip(carry[h], upd))
            new.append(upd)
        return tuple(new)

    init = tuple((jnp.full((1, tq), NEG_BIG, F32), jnp.zeros((1, tq), F32),
                  jnp.zeros((V_HEAD_DIM, tq), F32)) for _ in range(2))
    n_full = i * (tq // tk)
    carry = lax.fori_loop(0, n_full, step, init)
    for d in range(tq // tk):
        carry = step(n_full + d, carry, q0=d * tk)
    o_t = jnp.concatenate([acc / l for (_, l, acc) in carry], axis=0)
    z = zb_ref[0]
    out_ref[0] = (o_t.T * (z * _sigmoid(z))).astype(out_ref.dtype)


def _attention(q_t, k, v_t, zb, tq):
    bsz, seq, _ = k.shape
    return pl.pallas_call(
        _attn_kernel,
        grid=(bsz, B_HEADS // 2, seq // tq),
        in_specs=[
            pl.BlockSpec((1, 2 * LANES, tq), lambda b, p, i: (b, p, i)),
            pl.BlockSpec((1, seq, 2 * LANES), lambda b, p, i: (b, 0, p)),
            pl.BlockSpec((1, 2 * V_HEAD_DIM, seq), lambda b, p, i: (b, p, 0)),
            pl.BlockSpec((1, tq, LANES), lambda b, p, i: (b, i, p)),
        ],
        out_specs=pl.BlockSpec((1, tq, LANES), lambda b, p, i: (b, i, p)),
        out_shape=jax.ShapeDtypeStruct((bsz, seq, B_WIDTH), BF16),
        compiler_params=pltpu.CompilerParams(
            dimension_semantics=("arbitrary", "arbitrary", "arbitrary"),
            vmem_limit_bytes=VMEM_LIMIT),
        name="mla_attn",
    )(q_t, k, v_t, zb)


def _out_kernel(ya_ref, yb_ref, gl_ref, x_ref, bg_ref, woa_ref, wob_ref, wo_ref, gp_ref, out_ref):
    y_a = jnp.dot(ya_ref[...], woa_ref[...], preferred_element_type=F32)
    y_b = jnp.dot(yb_ref[...], wob_ref[...], preferred_element_type=F32)
    gates = _sigmoid(gl_ref[...] + bg_ref[...])
    merged = gates[:, :D_MODEL] * y_a + gates[:, D_MODEL:] * y_b
    o = _dot(merged, wo_ref[...])
    ms = jnp.mean(o * o, axis=-1, keepdims=True)
    out_ref[...] = x_ref[...] + o * lax.rsqrt(ms + NORM_EPS) * gp_ref[...]


def _out_proj(ya, yb, gl, x2, b_gate, woa, wob, wo, g_post, tm):
    t = x2.shape[0]
    const = lambda shape: pl.BlockSpec(shape, lambda i: (0,) * len(shape))
    rows = lambda n: pl.BlockSpec((tm, n), lambda i: (i, 0))
    return pl.pallas_call(
        _out_kernel,
        grid=(t // tm,),
        in_specs=[
            rows(A_WIDTH), rows(B_WIDTH), rows(2 * D_MODEL), rows(D_MODEL),
            const((1, 2 * D_MODEL)), const((A_WIDTH, D_MODEL)), const((B_WIDTH, D_MODEL)),
            const((D_MODEL, D_MODEL)), const((1, D_MODEL)),
        ],
        out_specs=rows(D_MODEL),
        out_shape=jax.ShapeDtypeStruct((t, D_MODEL), F32),
        compiler_params=pltpu.CompilerParams(
            dimension_semantics=("arbitrary",), vmem_limit_bytes=VMEM_LIMIT),
        name="out_proj",
    )(ya, yb, gl, x2, b_gate, woa, wob, wo, g_post)


def _rope_partner(w_pe):
    half = QK_ROPE_DIM // 2
    return jnp.concatenate([-w_pe[..., half:], w_pe[..., :half]], axis=-1)


def _prep_weights(w_in, w_decay_up, w_iclr_up, w_uq, w_ukv):
    o = np.cumsum([0, SHIFT_COLS, A_WIDTH, Q_LORA_RANK, KV_LORA_RANK, QK_ROPE_DIM, B_WIDTH])
    w_pe = w_in[:, o[4]:o[5]]
    kpe_blk = jnp.concatenate(
        [jnp.zeros((D_MODEL, QK_NOPE_DIM), F32), w_pe, _rope_partner(w_pe)], axis=-1)
    w_p = jnp.concatenate(
        [w_in[:, :o[4]], kpe_blk, w_in[:, o[5]:]], axis=-1).astype(BF16)
    zeros_lora = jnp.zeros((DECAY_LORA, A_WIDTH), F32)
    wd_p = jnp.concatenate([w_decay_up, zeros_lora], axis=0).astype(BF16)
    wa_p = jnp.concatenate([zeros_lora, w_iclr_up], axis=0).astype(BF16)
    wq = w_uq.reshape(Q_LORA_RANK, B_HEADS, QK_NOPE_DIM + QK_ROPE_DIM)
    wqt_p = jnp.concatenate(
        [wq, _rope_partner(wq[..., QK_NOPE_DIM:])], axis=-1).reshape(Q_LORA_RANK, -1).T.astype(BF16)
    wkv = w_ukv.reshape(KV_LORA_RANK, B_HEADS, QK_NOPE_DIM + V_HEAD_DIM)
    wk_p = jnp.concatenate(
        [wkv[..., :QK_NOPE_DIM], jnp.zeros_like(wkv[..., :QK_NOPE_DIM])],
        axis=-1).reshape(KV_LORA_RANK, -1).astype(BF16)
    wvt_p = wkv[..., QK_NOPE_DIM:].reshape(KV_LORA_RANK, -1).T.astype(BF16)
    return w_p, wd_p, wa_p, wqt_p, wk_p, wvt_p


def kernel(x, positions, g_pre, w_in, b_gate, mu_shift, w0, w_decay_up, a0, w_iclr_up, k_k, k_a,
           r_k, gn_gain, gn_bias, w_out_a, g_q, w_uq, g_kv, w_ukv, w_out_b, w_o, g_post):
    bsz, seq, _ = x.shape
    t = bsz * seq
    tm = 256
    row = lambda a: a.reshape(1, -1).astype(F32)
    w_p, wd_p, wa_p, wqt_p, wk_p, wvt_p = _prep_weights(w_in, w_decay_up, w_iclr_up, w_uq, w_ukv)
    x2 = x.reshape(t, D_MODEL)

    feat, za, lat, zb, gl = _inproj(x2, row(g_pre), w_p, tm)

    ya = _rwkv(feat, za, row(mu_shift), row(w0), row(a0), row(k_k), row(k_a), row(r_k),
               row(gn_gain), row(gn_bias), wd_p, wa_p, bsz, seq, 256)

    lane = np.arange(LANES)
    freq = np.where((lane >= QK_NOPE_DIM) & (lane < QK_NOPE_DIM + QK_ROPE_DIM),
                    ROPE_THETA ** (-((lane - QK_NOPE_DIM) % (QK_ROPE_DIM // 2)) * 2.0 / QK_ROPE_DIM),
                    0.0)
    invf = jnp.asarray(freq.reshape(LANES, 1), F32)
    pos3 = positions.reshape(t // tm, 1, tm)
    q_t, k, v_t = _mla_prep(lat, pos3, invf, g_q.reshape(-1, 1).astype(F32), row(g_kv),
                            wqt_p, wk_p, wvt_p, bsz, seq, tm)
    yb = _attention(q_t, k.reshape(bsz, seq, -1), v_t, zb.reshape(bsz, seq, -1), 1024)

    out = _out_proj(ya, yb.reshape(t, B_WIDTH), gl, x2, row(b_gate), w_out_a.astype(BF16),
                    w_out_b.astype(BF16), w_o.astype(BF16), row(g_post), tm)
    return out.reshape(bsz, seq, D_MODEL)
```

```python
import functools
import math

import jax
import jax.numpy as jnp
import numpy as np
from jax import lax
from jax.experimental import pallas as pl
from jax.experimental.pallas import tpu as pltpu

D_MODEL = 1024
A_HEADS = 8
A_HEAD_DIM = 64
A_WIDTH = A_HEADS * A_HEAD_DIM
DECAY_LORA = 64
ICLR_LORA = 64
DECAY_SCALE = 0.6065306597
GN_EPS = 64e-5
B_HEADS = 8
QK_NOPE_DIM = 64
QK_ROPE_DIM = 32
V_HEAD_DIM = 64
Q_LORA_RANK = 256
KV_LORA_RANK = 128
B_WIDTH = B_HEADS * V_HEAD_DIM
ROPE_THETA = 10000.0
NORM_EPS = 1e-6
SHIFT_COLS = 3 * A_WIDTH + DECAY_LORA + ICLR_LORA

LANES = 128
PAIR = 2 * A_HEAD_DIM
N_PAIRS = A_HEADS // 2
CHUNK = 64
LAT_COLS = 512
PROJ_COLS = SHIFT_COLS + A_WIDTH + LAT_COLS + B_WIDTH + 2 * D_MODEL
VMEM_LIMIT = 52 * 1024 * 1024

F32 = jnp.float32
BF16 = jnp.bfloat16
HI = lax.Precision.HIGHEST


def _dot(a, b):
    return jnp.dot(a.astype(BF16), b.astype(BF16), preferred_element_type=F32)


def _dot_nt(a, b):
    return lax.dot_general(a.astype(BF16), b.astype(BF16), (((1,), (1,)), ((), ())),
                           preferred_element_type=F32)


def _sigmoid(x):
    return 1.0 / (1.0 + jnp.exp(-x))


def _inproj_kernel(x_ref, g_ref, w_ref, feat_ref, za_ref, lat_ref, zb_ref, gl_ref):
    x = x_ref[...]
    ms = jnp.mean(x * x, axis=-1, keepdims=True)
    u = (x * lax.rsqrt(ms + NORM_EPS) * g_ref[...]).astype(BF16)
    off = 0
    for ref in (feat_ref, za_ref, lat_ref, zb_ref, gl_ref):
        n = ref.shape[-1]
        ref[...] = jnp.dot(u, w_ref[:, off:off + n], preferred_element_type=F32)
        off += n


def _inproj(x2, g_pre, w_p, tm):
    t = x2.shape[0]
    widths = (SHIFT_COLS, A_WIDTH, LAT_COLS, B_WIDTH, 2 * D_MODEL)
    return pl.pallas_call(
        _inproj_kernel,
        grid=(t // tm,),
        in_specs=[
            pl.BlockSpec((tm, D_MODEL), lambda i: (i, 0)),
            pl.BlockSpec((1, D_MODEL), lambda i: (0, 0)),
            pl.BlockSpec((D_MODEL, PROJ_COLS), lambda i: (0, 0)),
        ],
        out_specs=[pl.BlockSpec((tm, n), lambda i: (i, 0)) for n in widths],
        out_shape=[jax.ShapeDtypeStruct((t, n), F32) for n in widths],
        compiler_params=pltpu.CompilerParams(
            dimension_semantics=("arbitrary",), vmem_limit_bytes=VMEM_LIMIT),
        name="inproj",
    )(x2, g_pre, w_p)


def _split_bf16(x):
    hi = x.astype(BF16)
    return hi, (x - hi.astype(F32)).astype(BF16)


def _rwkv_kernel(feat_ref, za_ref, mu_ref, w0_ref, a0_ref, kk_ref, ka_ref, rk_ref, gg_ref, gb_ref,
                 wd_ref, wa_ref, out_ref, carry_ref, state_ref):
    tt = feat_ref.shape[0]
    n_chunks = tt // CHUNK

    @pl.when(pl.program_id(1) == 0)
    def _():
        carry_ref[...] = jnp.zeros_like(carry_ref)
        state_ref[...] = jnp.zeros_like(state_ref)

    row_t = lax.broadcasted_iota(jnp.int32, (tt, LANES), 0)
    pos_in_chunk = row_t % CHUNK

    def shifted(blk):
        cols = slice(blk * LANES, (blk + 1) * LANES)
        cur = feat_ref[:, cols]
        prev = jnp.where(row_t == 0, carry_ref[0:1, cols], pltpu.roll(cur, 1, 0))
        return cur + mu_ref[:, cols] * (prev - cur)

    ri = lax.broadcasted_iota(jnp.int32, (LANES, LANES), 0)
    ci = lax.broadcasted_iota(jnp.int32, (LANES, LANES), 1)
    same_head = (ri // A_HEAD_DIM) == (ci // A_HEAD_DIM)
    strict_bd = same_head & (ci < ri)
    incl_bd = same_head & (ci <= ri)
    eye = (ri == ci).astype(F32)
    ones2 = jnp.concatenate([same_head, same_head], axis=0).astype(BF16)
    zeros_sq = jnp.zeros((LANES, LANES), F32)
    lane = lax.broadcasted_iota(jnp.int32, (1, LANES), 1)
    m_a = (lane < A_HEAD_DIM).astype(F32)
    m_b = 1.0 - m_a

    def head_sum(x):
        hi, lo = _split_bf16(x)
        return jnp.dot(jnp.concatenate([hi, lo], axis=1), ones2, preferred_element_type=F32)

    def chunk_cumsum(x):
        for sh in (1, 2, 4, 8, 16, 32):
            x = x + jnp.where(pos_in_chunk >= sh, pltpu.roll(x, sh, 0), 0.0)
        return x

    def stack(z):
        return jnp.concatenate([z * m_a, z * m_b], axis=0)

    def unstack(z):
        return z[:CHUNK] + z[CHUNK:]

    la = shifted(3 * N_PAIRS)
    tanh_la = jnp.tanh(la).astype(BF16)
    la_bf = la.astype(BF16)

    pairs = range(N_PAIRS)
    items = [(p, ch) for p in pairs for ch in range(n_chunks)]
    each = lambda fn: {it: fn(it) for it in items}
    pcols = lambda p: slice(p * LANES, (p + 1) * LANES)

    r = {p: shifted(p) for p in pairs}
    k = {p: shifted(N_PAIRS + p) for p in pairs}
    v = {p: shifted(2 * N_PAIRS + p) for p in pairs}
    carry_ref[0:1, :] = feat_ref[tt - 1:tt, :]
    w_logit = {p: w0_ref[:, pcols(p)] + jnp.dot(tanh_la, wd_ref[:, pcols(p)],
                                                preferred_element_type=F32) for p in pairs}
    a = {p: _sigmoid(a0_ref[:, pcols(p)] + jnp.dot(la_bf, wa_ref[:, pcols(p)],
                                                   preferred_element_type=F32)) for p in pairs}
    lw = {p: -DECAY_SCALE * _sigmoid(w_logit[p]) for p in pairs}
    kk = {p: k[p] * kk_ref[:, pcols(p)] for p in pairs}
    k2 = {p: k[p] * (1.0 + (a[p] - 1.0) * ka_ref[:, pcols(p)]) for p in pairs}
    sums = {p: head_sum(jnp.concatenate([kk[p] * kk[p], r[p] * k2[p] * rk_ref[:, pcols(p)]],
                                        axis=0)) for p in pairs}
    kk = {p: kk[p] / jnp.maximum(jnp.sqrt(sums[p][:tt]), 1e-12) for p in pairs}
    bonus = {p: sums[p][tt:] * v[p] for p in pairs}
    b = {p: kk[p] * a[p] for p in pairs}
    cum = {p: chunk_cumsum(lw[p]) for p in pairs}

    def chunk_operands(it):
        p, ch = it
        sl = slice(ch * CHUNK, (ch + 1) * CHUNK)
        lwc, cumc, rc, kc, vc, bc, kkc = lw[p][sl], cum[p][sl], r[p][sl], k2[p][sl], v[p][sl], \
            b[p][sl], kk[p][sl]
        last = cumc[CHUNK - 1:CHUNK, :]
        p_inv = jnp.exp(-cumc)
        to_end = jnp.exp(last - cumc)
        return dict(
            cts=stack(-kkc * jnp.exp(cumc - lwc)), rt=rc * jnp.exp(cumc),
            bt=bc * p_inv, kt=kc * p_inv, vs=stack(vc), decay=eye * jnp.exp(last),
            ends_t=jnp.concatenate([stack(bc * to_end).T, stack(kc * to_end).T], axis=1))

    ops = each(chunk_operands)
    g = each(lambda it: _dot_nt(
        jnp.concatenate([ops[it]["cts"], stack(ops[it]["rt"])], axis=0),
        jnp.concatenate([ops[it]["bt"], ops[it]["bt"], ops[it]["kt"], ops[it]["kt"]], axis=0)))
    nb = each(lambda it: jnp.where(strict_bd, g[it][:PAIR, :PAIR], 0.0))
    nk = each(lambda it: jnp.where(strict_bd, g[it][:PAIR, PAIR:], 0.0))
    m_bk = each(lambda it: jnp.concatenate(
        [jnp.where(incl_bd, g[it][PAIR:, :PAIR], 0.0),
         jnp.where(incl_bd, g[it][PAIR:, PAIR:], 0.0)], axis=1))
    t_inv = each(lambda it: eye + nb[it])
    npow = each(lambda it: _dot(nb[it], nb[it]))
    nkv = each(lambda it: _dot(nk[it], ops[it]["vs"]))
    for _ in range(int(math.log2(CHUNK)) - 2):
        both = each(lambda it: _dot(npow[it], jnp.concatenate([npow[it], t_inv[it]], axis=1)))
        npow = each(lambda it: both[it][:, :PAIR])
        t_inv = each(lambda it: t_inv[it] + both[it][:, PAIR:])
    last_term = each(lambda it: _dot(npow[it], t_inv[it]))
    t_inv = each(lambda it: t_inv[it] + last_term[it])
    w12 = each(lambda it: _dot(t_inv[it], jnp.concatenate([ops[it]["cts"], nkv[it]], axis=1)))
    big = each(lambda it: _dot(
        jnp.concatenate([ops[it]["ends_t"], m_bk[it]], axis=0),
        jnp.concatenate([w12[it], jnp.concatenate([zeros_sq, ops[it]["vs"]], axis=1)], axis=0)))
    gt = each(lambda it: (big[it][:PAIR, :PAIR] + ops[it]["decay"]).astype(BF16))
    q1 = each(lambda it: ops[it]["rt"] + unstack(big[it][PAIR:, :PAIR]))

    h = {p: state_ref[p] for p in pairs}
    h_at = {}
    for ch in range(n_chunks):
        for p in pairs:
            it = (p, ch)
            h_at[it] = h[p]
            h_hi, h_lo = _split_bf16(h[p])
            h[p] = jnp.dot(jnp.concatenate([gt[it], gt[it]], axis=1),
                           jnp.concatenate([h_hi, h_lo], axis=0),
                           preferred_element_type=F32) + big[it][:PAIR, PAIR:]
    for p in pairs:
        state_ref[p] = h[p]
    y_it = each(lambda it: _dot(q1[it], h_at[it]) + unstack(big[it][PAIR:, PAIR:]))

    y = {p: jnp.concatenate([y_it[(p, ch)] for ch in range(n_chunks)], axis=0) for p in pairs}
    mean = {p: head_sum(y[p]) * (1.0 / A_HEAD_DIM) for p in pairs}
    d = {p: y[p] - mean[p] for p in pairs}
    var = {p: head_sum(d[p] * d[p]) * (1.0 / A_HEAD_DIM) for p in pairs}
    for p in pairs:
        y_n = d[p] * lax.rsqrt(var[p] + GN_EPS) * gg_ref[:, pcols(p)] + gb_ref[:, pcols(p)] + bonus[p]
        z = za_ref[:, pcols(p)]
        out_ref[:, pcols(p)] = (y_n * (z * _sigmoid(z))).astype(out_ref.dtype)


def _rwkv(feat, za, mu, w0, a0, k_k, k_a, r_k, gn_gain, gn_bias, wd_p, wa_p, bsz, seq, tt):
    nt = seq // tt
    const = lambda shape: pl.BlockSpec(shape, lambda b, t: (0,) * len(shape))
    rows = lambda n: pl.BlockSpec((tt, n), lambda b, t: (b * nt + t, 0))
    return pl.pallas_call(
        _rwkv_kernel,
        grid=(bsz, nt),
        in_specs=[rows(SHIFT_COLS), rows(A_WIDTH), const((1, SHIFT_COLS))]
        + [const((1, A_WIDTH))] * 7 + [const((LANES, A_WIDTH))] * 2,
        out_specs=rows(A_WIDTH),
        out_shape=jax.ShapeDtypeStruct((bsz * seq, A_WIDTH), BF16),
        scratch_shapes=[pltpu.VMEM((8, SHIFT_COLS), F32),
                        pltpu.VMEM((N_PAIRS, LANES, LANES), F32)],
        compiler_params=pltpu.CompilerParams(
            dimension_semantics=("arbitrary", "arbitrary"), vmem_limit_bytes=VMEM_LIMIT),
        name="rwkv7",
    )(feat, za, mu, w0, a0, k_k, k_a, r_k, gn_gain, gn_bias, wd_p, wa_p)


QK_SCALE = math.log2(math.e) / math.sqrt(QK_NOPE_DIM + QK_ROPE_DIM)


def _mla_prep_kernel(lat_ref, pos_ref, invf_ref, gq_ref, gkv_ref, wqt_ref, wk_ref, wvt_ref,
                     qt_ref, k_ref, vt_ref):
    lat = lat_ref[...]
    pos = pos_ref[0].astype(F32)
    ang_t = invf_ref[...] * pos
    cos_t, sin_t = jnp.cos(ang_t), jnp.sin(ang_t)
    feat = lax.broadcasted_iota(jnp.int32, (LANES, 1), 0)
    cos_q = jnp.where(feat < QK_NOPE_DIM + QK_ROPE_DIM, cos_t, 0.0)

    cq_t = jnp.concatenate([lat[:, j * LANES:(j + 1) * LANES].T
                            for j in range(Q_LORA_RANK // LANES)], axis=0)
    nq_t = cq_t * lax.rsqrt(jnp.mean(cq_t * cq_t, axis=0, keepdims=True) + NORM_EPS) * gq_ref[...]
    q_raw = _dot(wqt_ref[...], nq_t)
    for h in range(B_HEADS):
        sl = slice(h * LANES, (h + 1) * LANES)
        blk = q_raw[sl]
        rot = blk * cos_q + pltpu.roll(blk, LANES - QK_ROPE_DIM, 0) * sin_t
        qt_ref[0, sl, :] = (rot * QK_SCALE).astype(qt_ref.dtype)

    ckv = lat[:, Q_LORA_RANK:Q_LORA_RANK + KV_LORA_RANK]
    nkv = ckv * lax.rsqrt(jnp.mean(ckv * ckv, axis=-1, keepdims=True) + NORM_EPS) * gkv_ref[...]
    vt_ref[0] = _dot(wvt_ref[...], nkv.T).astype(vt_ref.dtype)
    k_nope = _dot(nkv, wk_ref[...])
    lane = lax.broadcasted_iota(jnp.int32, (1, LANES), 1)
    rope_lane = (lane >= QK_NOPE_DIM) & (lane < QK_NOPE_DIM + QK_ROPE_DIM)
    pe = lat[:, Q_LORA_RANK + KV_LORA_RANK:]
    k_pe = pe * jnp.where(rope_lane, cos_t.T, 0.0) + pltpu.roll(pe, LANES - QK_ROPE_DIM, 1) * sin_t.T
    for h in range(B_HEADS):
        sl = slice(h * LANES, (h + 1) * LANES)
        k_ref[:, sl] = (k_nope[:, sl] + k_pe).astype(k_ref.dtype)


def _mla_prep(lat, pos3, invf, g_q, g_kv, wqt_p, wk_p, wvt_p, bsz, seq, tm):
    t = lat.shape[0]
    nt = seq // tm
    const = lambda shape: pl.BlockSpec(shape, lambda i: (0,) * len(shape))
    return pl.pallas_call(
        _mla_prep_kernel,
        grid=(t // tm,),
        in_specs=[
            pl.BlockSpec((tm, LAT_COLS), lambda i: (i, 0)),
            pl.BlockSpec((1, 1, tm), lambda i: (i, 0, 0)),
            const((LANES, 1)), const((Q_LORA_RANK, 1)), const((1, KV_LORA_RANK)),
            const((B_HEADS * LANES, Q_LORA_RANK)), const((KV_LORA_RANK, B_HEADS * LANES)),
            const((B_WIDTH, KV_LORA_RANK)),
        ],
        out_specs=[
            pl.BlockSpec((1, B_HEADS * LANES, tm), lambda i: (i // nt, 0, i % nt)),
            pl.BlockSpec((tm, B_HEADS * LANES), lambda i: (i, 0)),
            pl.BlockSpec((1, B_WIDTH, tm), lambda i: (i // nt, 0, i % nt)),
        ],
        out_shape=[
            jax.ShapeDtypeStruct((bsz, B_HEADS * LANES, seq), BF16),
            jax.ShapeDtypeStruct((t, B_HEADS * LANES), BF16),
            jax.ShapeDtypeStruct((bsz, B_WIDTH, seq), BF16),
        ],
        compiler_params=pltpu.CompilerParams(
            dimension_semantics=("arbitrary",), vmem_limit_bytes=VMEM_LIMIT),
        name="mla_prep",
    )(lat, pos3, invf, g_q, g_kv, wqt_p, wk_p, wvt_p)


NEG_BIG = -1e30
ATT_TK = 512


def _attn_kernel(qt_ref, k_ref, vt_ref, zb_ref, out_ref):
    tq = qt_ref.shape[2]
    tk = ATT_TK
    i = pl.program_id(2)
    q_t = (qt_ref[0, :LANES, :], qt_ref[0, LANES:, :])

    def step(j, carry, q0=None):
        start = pl.multiple_of(j * tk, tk)
        kb = k_ref[0, pl.ds(start, tk), :]
        vb = vt_ref[0, :, pl.ds(start, tk)]
        new = []
        for h in range(2):
            m, l, acc = carry[h] if q0 is None else [c[:, q0:] for c in carry[h]]
            q_h = q_t[h] if q0 is None else q_t[h][:, q0:]
            s = jnp.dot(kb[:, h * LANES:(h + 1) * LANES], q_h, preferred_element_type=F32)
            if q0 is not None:
                key = lax.broadcasted_iota(jnp.int32, s.shape, 0)
                query = lax.broadcasted_iota(jnp.int32, s.shape, 1)
                s = jnp.where(key <= query, s, NEG_BIG)
            m_new = jnp.maximum(m, jnp.max(s, axis=0, keepdims=True))
            alpha = jnp.exp2(m - m_new)
            p = jnp.exp2(s - m_new)
            l_new = alpha * l + jnp.sum(p, axis=0, keepdims=True)
            pv = jnp.dot(vb[h * V_HEAD_DIM:(h + 1) * V_HEAD_DIM, :], p.astype(BF16),
                         preferred_element_type=F32)
            upd = (m_new, l_new, acc * alpha + pv)
            if q0:
                upd = tuple(jnp.concatenate([c[:, :q0], u], axis=1) for c, u in zip(carry[h], upd))
            new.append(upd)
        return tuple(new)

    init = tuple((jnp.full((1, tq), NEG_BIG, F32), jnp.zeros((1, tq), F32),
                  jnp.zeros((V_HEAD_DIM, tq), F32)) for _ in range(2))
    n_full = i * (tq // tk)
    carry = lax.fori_loop(0, n_full, step, init)
    for d in range(tq // tk):
        carry = step(n_full + d, carry, q0=d * tk)
    o_t = jnp.concatenate([acc / l for (_, l, acc) in carry], axis=0)
    z = zb_ref[0]
    out_ref[0] = (o_t.T * (z * _sigmoid(z))).astype(out_ref.dtype)


def _attention(q_t, k, v_t, zb, tq):
    bsz, seq, _ = k.shape
    return pl.pallas_call(
        _attn_kernel,
        grid=(bsz, B_HEADS // 2, seq // tq),
        in_specs=[
            pl.BlockSpec((1, 2 * LANES, tq), lambda b, p, i: (b, p, i)),
            pl.BlockSpec((1, seq, 2 * LANES), lambda b, p, i: (b, 0, p)),
            pl.BlockSpec((1, 2 * V_HEAD_DIM, seq), lambda b, p, i: (b, p, 0)),
            pl.BlockSpec((1, tq, LANES), lambda b, p, i: (b, i, p)),
        ],
        out_specs=pl.BlockSpec((1, tq, LANES), lambda b, p, i: (b, i, p)),
        out_shape=jax.ShapeDtypeStruct((bsz, seq, B_WIDTH), BF16),
        compiler_params=pltpu.CompilerParams(
            dimension_semantics=("arbitrary", "arbitrary", "arbitrary"),
            vmem_limit_bytes=VMEM_LIMIT),
        name="mla_attn",
    )(q_t, k, v_t, zb)


def _out_kernel(ya_ref, yb_ref, gl_ref, x_ref, bg_ref, woa_ref, wob_ref, wo_ref, gp_ref, out_ref):
    y_a = jnp.dot(ya_ref[...], woa_ref[...], preferred_element_type=F32)
    y_b = jnp.dot(yb_ref[...], wob_ref[...], preferred_element_type=F32)
    gates = _sigmoid(gl_ref[...] + bg_ref[...])
    merged = gates[:, :D_MODEL] * y_a + gates[:, D_MODEL:] * y_b
    o = _dot(merged, wo_ref[...])
    ms = jnp.mean(o * o, axis=-1, keepdims=True)
    out_ref[...] = x_ref[...] + o * lax.rsqrt(ms + NORM_EPS) * gp_ref[...]


def _out_proj(ya, yb, gl, x2, b_gate, woa, wob, wo, g_post, tm):
    t = x2.shape[0]
    const = lambda shape: pl.BlockSpec(shape, lambda i: (0,) * len(shape))
    rows = lambda n: pl.BlockSpec((tm, n), lambda i: (i, 0))
    return pl.pallas_call(
        _out_kernel,
        grid=(t // tm,),
        in_specs=[
            rows(A_WIDTH), rows(B_WIDTH), rows(2 * D_MODEL), rows(D_MODEL),
            const((1, 2 * D_MODEL)), const((A_WIDTH, D_MODEL)), const((B_WIDTH, D_MODEL)),
            const((D_MODEL, D_MODEL)), const((1, D_MODEL)),
        ],
        out_specs=rows(D_MODEL),
        out_shape=jax.ShapeDtypeStruct((t, D_MODEL), F32),
        compiler_params=pltpu.CompilerParams(
            dimension_semantics=("arbitrary",), vmem_limit_bytes=VMEM_LIMIT),
        name="out_proj",
    )(ya, yb, gl, x2, b_gate, woa, wob, wo, g_post)


def _rope_partner(w_pe):
    half = QK_ROPE_DIM // 2
    return jnp.concatenate([-w_pe[..., half:], w_pe[..., :half]], axis=-1)


def _prep_weights(w_in, w_decay_up, w_iclr_up, w_uq, w_ukv):
    o = np.cumsum([0, SHIFT_COLS, A_WIDTH, Q_LORA_RANK, KV_LORA_RANK, QK_ROPE_DIM, B_WIDTH])
    w_pe = w_in[:, o[4]:o[5]]
    kpe_blk = jnp.concatenate(
        [jnp.zeros((D_MODEL, QK_NOPE_DIM), F32), w_pe, _rope_partner(w_pe)], axis=-1)
    w_p = jnp.concatenate(
        [w_in[:, :o[4]], kpe_blk, w_in[:, o[5]:]], axis=-1).astype(BF16)
    zeros_lora = jnp.zeros((DECAY_LORA, A_WIDTH), F32)
    wd_p = jnp.concatenate([w_decay_up, zeros_lora], axis=0).astype(BF16)
    wa_p = jnp.concatenate([zeros_lora, w_iclr_up], axis=0).astype(BF16)
    wq = w_uq.reshape(Q_LORA_RANK, B_HEADS, QK_NOPE_DIM + QK_ROPE_DIM)
    wqt_p = jnp.concatenate(
        [wq, _rope_partner(wq[..., QK_NOPE_DIM:])], axis=-1).reshape(Q_LORA_RANK, -1).T.astype(BF16)
    wkv = w_ukv.reshape(KV_LORA_RANK, B_HEADS, QK_NOPE_DIM + V_HEAD_DIM)
    wk_p = jnp.concatenate(
        [wkv[..., :QK_NOPE_DIM], jnp.zeros_like(wkv[..., :QK_NOPE_DIM])],
        axis=-1).reshape(KV_LORA_RANK, -1).astype(BF16)
    wvt_p = wkv[..., QK_NOPE_DIM:].reshape(KV_LORA_RANK, -1).T.astype(BF16)
    return w_p, wd_p, wa_p, wqt_p, wk_p, wvt_p


def kernel(x, positions, g_pre, w_in, b_gate, mu_shift, w0, w_decay_up, a0, w_iclr_up, k_k, k_a,
           r_k, gn_gain, gn_bias, w_out_a, g_q, w_uq, g_kv, w_ukv, w_out_b, w_o, g_post):
    bsz, seq, _ = x.shape
    t = bsz * seq
    tm = 256
    row = lambda a: a.reshape(1, -1).astype(F32)
    w_p, wd_p, wa_p, wqt_p, wk_p, wvt_p = _prep_weights(w_in, w_decay_up, w_iclr_up, w_uq, w_ukv)
    x2 = x.reshape(t, D_MODEL)

    feat, za, lat, zb, gl = _inproj(x2, row(g_pre), w_p, tm)

    ya = _rwkv(feat, za, row(mu_shift), row(w0), row(a0), row(k_k), row(k_a), row(r_k),
               row(gn_gain), row(gn_bias), wd_p, wa_p, bsz, seq, 256)

    lane = np.arange(LANES)
    freq = np.where((lane >= QK_NOPE_DIM) & (lane < QK_NOPE_DIM + QK_ROPE_DIM),
                    ROPE_THETA ** (-((lane - QK_NOPE_DIM) % (QK_ROPE_DIM // 2)) * 2.0 / QK_ROPE_DIM),
                    0.0)
    invf = jnp.asarray(freq.reshape(LANES, 1), F32)
    pos3 = positions.reshape(t // tm, 1, tm)
    q_t, k, v_t = _mla_prep(lat, pos3, invf, g_q.reshape(-1, 1).astype(F32), row(g_kv),
                            wqt_p, wk_p, wvt_p, bsz, seq, tm)
    yb = _attention(q_t, k.reshape(bsz, seq, -1), v_t, zb.reshape(bsz, seq, -1), 1024)

    out = _out_proj(ya, yb.reshape(t, B_WIDTH), gl, x2, row(b_gate), w_out_a.astype(BF16),
                    w_out_b.astype(BF16), w_o.astype(BF16), row(g_post), tm)
    return out.reshape(bsz, seq, D_MODEL)
```

```python
import functools
import math

import jax
import jax.numpy as jnp
import numpy as np
from jax import lax
from jax.experimental import pallas as pl
from jax.experimental.pallas import tpu as pltpu

D_MODEL = 1024
A_HEADS = 8
A_HEAD_DIM = 64
A_WIDTH = A_HEADS * A_HEAD_DIM
DECAY_LORA = 64
ICLR_LORA = 64
DECAY_SCALE = 0.6065306597
GN_EPS = 64e-5
B_HEADS = 8
QK_NOPE_DIM = 64
QK_ROPE_DIM = 32
V_HEAD_DIM = 64
Q_LORA_RANK = 256
KV_LORA_RANK = 128
B_WIDTH = B_HEADS * V_HEAD_DIM
ROPE_THETA = 10000.0
NORM_EPS = 1e-6
SHIFT_COLS = 3 * A_WIDTH + DECAY_LORA + ICLR_LORA

LANES = 128
PAIR = 2 * A_HEAD_DIM
N_PAIRS = A_HEADS // 2
CHUNK = 64
LAT_COLS = 512
PROJ_COLS = SHIFT_COLS + A_WIDTH + LAT_COLS + B_WIDTH + 2 * D_MODEL
VMEM_LIMIT = 52 * 1024 * 1024

F32 = jnp.float32
BF16 = jnp.bfloat16
HI = lax.Precision.HIGHEST


def _dot(a, b):
    return jnp.dot(a.astype(BF16), b.astype(BF16), preferred_element_type=F32)


def _dot_nt(a, b):
    return lax.dot_general(a.astype(BF16), b.astype(BF16), (((1,), (1,)), ((), ())),
                           preferred_element_type=F32)


def _sigmoid(x):
    return 1.0 / (1.0 + jnp.exp(-x))


def _inproj_kernel(x_ref, g_ref, w_ref, feat_ref, za_ref, lat_ref, zb_ref, gl_ref):
    x = x_ref[...]
    ms = jnp.mean(x * x, axis=-1, keepdims=True)
    u = (x * lax.rsqrt(ms + NORM_EPS) * g_ref[...]).astype(BF16)
    off = 0
    for ref in (feat_ref, za_ref, lat_ref, zb_ref, gl_ref):
        n = ref.shape[-1]
        ref[...] = jnp.dot(u, w_ref[:, off:off + n], preferred_element_type=F32)
        off += n


def _inproj(x2, g_pre, w_p, tm):
    t = x2.shape[0]
    widths = (SHIFT_COLS, A_WIDTH, LAT_COLS, B_WIDTH, 2 * D_MODEL)
    return pl.pallas_call(
        _inproj_kernel,
        grid=(t // tm,),
        in_specs=[
            pl.BlockSpec((tm, D_MODEL), lambda i: (i, 0)),
            pl.BlockSpec((1, D_MODEL), lambda i: (0, 0)),
            pl.BlockSpec((D_MODEL, PROJ_COLS), lambda i: (0, 0)),
        ],
        out_specs=[pl.BlockSpec((tm, n), lambda i: (i, 0)) for n in widths],
        out_shape=[jax.ShapeDtypeStruct((t, n), F32) for n in widths],
        compiler_params=pltpu.CompilerParams(
            dimension_semantics=("arbitrary",), vmem_limit_bytes=VMEM_LIMIT),
        name="inproj",
    )(x2, g_pre, w_p)


def _split_bf16(x):
    hi = x.astype(BF16)
    return hi, (x - hi.astype(F32)).astype(BF16)


def _rwkv_kernel(feat_ref, za_ref, mu_ref, w0_ref, a0_ref, kk_ref, ka_ref, rk_ref, gg_ref, gb_ref,
                 wd_ref, wa_ref, out_ref, carry_ref, state_ref):
    tt = feat_ref.shape[0]
    n_chunks = tt // CHUNK

    @pl.when(pl.program_id(1) == 0)
    def _():
        carry_ref[...] = jnp.zeros_like(carry_ref)
        state_ref[...] = jnp.zeros_like(state_ref)

    row_t = lax.broadcasted_iota(jnp.int32, (tt, LANES), 0)
    pos_in_chunk = row_t % CHUNK

    def shifted(blk):
        cols = slice(blk * LANES, (blk + 1) * LANES)
        cur = feat_ref[:, cols]
        prev = jnp.where(row_t == 0, carry_ref[0:1, cols], pltpu.roll(cur, 1, 0))
        return cur + mu_ref[:, cols] * (prev - cur)

    ri = lax.broadcasted_iota(jnp.int32, (LANES, LANES), 0)
    ci = lax.broadcasted_iota(jnp.int32, (LANES, LANES), 1)
    same_head = (ri // A_HEAD_DIM) == (ci // A_HEAD_DIM)
    strict_bd = same_head & (ci < ri)
    incl_bd = same_head & (ci <= ri)
    eye = (ri == ci).astype(F32)
    ones2 = jnp.concatenate([same_head, same_head], axis=0).astype(BF16)
    zeros_sq = jnp.zeros((LANES, LANES), F32)
    lane = lax.broadcasted_iota(jnp.int32, (1, LANES), 1)
    m_a = (lane < A_HEAD_DIM).astype(F32)
    m_b = 1.0 - m_a

    def head_sum(x):
        hi, lo = _split_bf16(x)
        return jnp.dot(jnp.concatenate([hi, lo], axis=1), ones2, preferred_element_type=F32)

    def chunk_cumsum(x):
        for sh in (1, 2, 4, 8, 16, 32):
            x = x + jnp.where(pos_in_chunk >= sh, pltpu.roll(x, sh, 0), 0.0)
        return x

    def stack(z):
        return jnp.concatenate([z * m_a, z * m_b], axis=0)

    def unstack(z):
        return z[:CHUNK] + z[CHUNK:]

    la = shifted(3 * N_PAIRS)
    tanh_la = jnp.tanh(la).astype(BF16)
    la_bf = la.astype(BF16)

    pairs = range(N_PAIRS)
    items = [(p, ch) for p in pairs for ch in range(n_chunks)]
    each = lambda fn: {it: fn(it) for it in items}
    pcols = lambda p: slice(p * LANES, (p + 1) * LANES)

    r = {p: shifted(p) for p in pairs}
    k = {p: shifted(N_PAIRS + p) for p in pairs}
    v = {p: shifted(2 * N_PAIRS + p) for p in pairs}
    carry_ref[0:1, :] = feat_ref[tt - 1:tt, :]
    w_logit = {p: w0_ref[:, pcols(p)] + jnp.dot(tanh_la, wd_ref[:, pcols(p)],
                                                preferred_element_type=F32) for p in pairs}
    a = {p: _sigmoid(a0_ref[:, pcols(p)] + jnp.dot(la_bf, wa_ref[:, pcols(p)],
                                                   preferred_element_type=F32)) for p in pairs}
    lw = {p: -DECAY_SCALE * _sigmoid(w_logit[p]) for p in pairs}
    kk = {p: k[p] * kk_ref[:, pcols(p)] for p in pairs}
    k2 = {p: k[p] * (1.0 + (a[p] - 1.0) * ka_ref[:, pcols(p)]) for p in pairs}
    sums = {p: head_sum(jnp.concatenate([kk[p] * kk[p], r[p] * k2[p] * rk_ref[:, pcols(p)]],
                                        axis=0)) for p in pairs}
    kk = {p: kk[p] / jnp.maximum(jnp.sqrt(sums[p][:tt]), 1e-12) for p in pairs}
    bonus = {p: sums[p][tt:] * v[p] for p in pairs}
    b = {p: kk[p] * a[p] for p in pairs}
    cum = {p: chunk_cumsum(lw[p]) for p in pairs}

    def chunk_operands(it):
        p, ch = it
        sl = slice(ch * CHUNK, (ch + 1) * CHUNK)
        lwc, cumc, rc, kc, vc, bc, kkc = lw[p][sl], cum[p][sl], r[p][sl], k2[p][sl], v[p][sl], \
            b[p][sl], kk[p][sl]
        last = cumc[CHUNK - 1:CHUNK, :]
        p_inv = jnp.exp(-cumc)
        to_end = jnp.exp(last - cumc)
        return dict(
            cts=stack(-kkc * jnp.exp(cumc - lwc)), rt=rc * jnp.exp(cumc),
            bt=bc * p_inv, kt=kc * p_inv, vs=stack(vc), decay=eye * jnp.exp(last),
            ends_t=jnp.concatenate([stack(bc * to_end).T, stack(kc * to_end).T], axis=1))

    ops = each(chunk_operands)
    g = each(lambda it: _dot_nt(
        jnp.concatenate([ops[it]["cts"], stack(ops[it]["rt"])], axis=0),
        jnp.concatenate([ops[it]["bt"], ops[it]["bt"], ops[it]["kt"], ops[it]["kt"]], axis=0)))
    nb = each(lambda it: jnp.where(strict_bd, g[it][:PAIR, :PAIR], 0.0))
    nk = each(lambda it: jnp.where(strict_bd, g[it][:PAIR, PAIR:], 0.0))
    m_bk = each(lambda it: jnp.concatenate(
        [jnp.where(incl_bd, g[it][PAIR:, :PAIR], 0.0),
         jnp.where(incl_bd, g[it][PAIR:, PAIR:], 0.0)], axis=1))
    t_inv = each(lambda it: eye + nb[it])
    npow = each(lambda it: _dot(nb[it], nb[it]))
    nkv = each(lambda it: _dot(nk[it], ops[it]["vs"]))
    for _ in range(int(math.log2(CHUNK)) - 2):
        both = each(lambda it: _dot(npow[it], jnp.concatenate([npow[it], t_inv[it]], axis=1)))
        npow = each(lambda it: both[it][:, :PAIR])
        t_inv = each(lambda it: t_inv[it] + both[it][:, PAIR:])
    last_term = each(lambda it: _dot(npow[it], t_inv[it]))
    t_inv = each(lambda it: t_inv[it] + last_term[it])
    w12 = each(lambda it: _dot(t_inv[it], jnp.concatenate([ops[it]["cts"], nkv[it]], axis=1)))
    big = each(lambda it: _dot(
        jnp.concatenate([ops[it]["ends_t"], m_bk[it]], axis=0),
        jnp.concatenate([w12[it], jnp.concatenate([zeros_sq, ops[it]["vs"]], axis=1)], axis=0)))
    gt = each(lambda it: (big[it][:PAIR, :PAIR] + ops[it]["decay"]).astype(BF16))
    q1 = each(lambda it: ops[it]["rt"] + unstack(big[it][PAIR:, :PAIR]))

    h = {p: state_ref[p] for p in pairs}
    h_at = {}
    for ch in range(n_chunks):
        for p in pairs:
            it = (p, ch)
            h_at[it] = h[p]
            h_hi, h_lo = _split_bf16(h[p])
            h[p] = jnp.dot(jnp.concatenate([gt[it], gt[it]], axis=1),
                           jnp.concatenate([h_hi, h_lo], axis=0),
                           preferred_element_type=F32) + big[it][:PAIR, PAIR:]
    for p in pairs:
        state_ref[p] = h[p]
    y_it = each(lambda it: _dot(q1[it], h_at[it]) + unstack(big[it][PAIR:, PAIR:]))

    y = {p: jnp.concatenate([y_it[(p, ch)] for ch in range(n_chunks)], axis=0) for p in pairs}
    mean = {p: head_sum(y[p]) * (1.0 / A_HEAD_DIM) for p in pairs}
    d = {p: y[p] - mean[p] for p in pairs}
    var = {p: head_sum(d[p] * d[p]) * (1.0 / A_HEAD_DIM) for p in pairs}
    for p in pairs:
        y_n = d[p] * lax.rsqrt(var[p] + GN_EPS) * gg_ref[:, pcols(p)] + gb_ref[:, pcols(p)] + bonus[p]
        z = za_ref[:, pcols(p)]
        out_ref[:, pcols(p)] = (y_n * (z * _sigmoid(z))).astype(out_ref.dtype)


def _rwkv(feat, za, mu, w0, a0, k_k, k_a, r_k, gn_gain, gn_bias, wd_p, wa_p, bsz, seq, tt):
    nt = seq // tt
    const = lambda shape: pl.BlockSpec(shape, lambda b, t: (0,) * len(shape))
    rows = lambda n: pl.BlockSpec((tt, n), lambda b, t: (b * nt + t, 0))
    return pl.pallas_call(
        _rwkv_kernel,
        grid=(bsz, nt),
        in_specs=[rows(SHIFT_COLS), rows(A_WIDTH), const((1, SHIFT_COLS))]
        + [const((1, A_WIDTH))] * 7 + [const((LANES, A_WIDTH))] * 2,
        out_specs=rows(A_WIDTH),
        out_shape=jax.ShapeDtypeStruct((bsz * seq, A_WIDTH), BF16),
        scratch_shapes=[pltpu.VMEM((8, SHIFT_COLS), F32),
                        pltpu.VMEM((N_PAIRS, LANES, LANES), F32)],
        compiler_params=pltpu.CompilerParams(
            dimension_semantics=("arbitrary", "arbitrary"), vmem_limit_bytes=VMEM_LIMIT),
        name="rwkv7",
    )(feat, za, mu, w0, a0, k_k, k_a, r_k, gn_gain, gn_bias, wd_p, wa_p)


QK_SCALE = math.log2(math.e) / math.sqrt(QK_NOPE_DIM + QK_ROPE_DIM)


def _mla_prep_kernel(lat_ref, pos_ref, invf_ref, gq_ref, gkv_ref, wqt_ref, wk_ref, wvt_ref,
                     qt_ref, k_ref, vt_ref):
    lat = lat_ref[...]
    pos = pos_ref[0].astype(F32)
    ang_t = invf_ref[...] * pos
    cos_t, sin_t = jnp.cos(ang_t), jnp.sin(ang_t)
    feat = lax.broadcasted_iota(jnp.int32, (LANES, 1), 0)
    cos_q = jnp.where(feat < QK_NOPE_DIM + QK_ROPE_DIM, cos_t, 0.0)

    cq_t = jnp.concatenate([lat[:, j * LANES:(j + 1) * LANES].T
                            for j in range(Q_LORA_RANK // LANES)], axis=0)
    nq_t = cq_t * lax.rsqrt(jnp.mean(cq_t * cq_t, axis=0, keepdims=True) + NORM_EPS) * gq_ref[...]
    q_raw = _dot(wqt_ref[...], nq_t)
    for h in range(B_HEADS):
        sl = slice(h * LANES, (h + 1) * LANES)
        blk = q_raw[sl]
        rot = blk * cos_q + pltpu.roll(blk, LANES - QK_ROPE_DIM, 0) * sin_t
        qt_ref[0, sl, :] = (rot * QK_SCALE).astype(qt_ref.dtype)

    ckv = lat[:, Q_LORA_RANK:Q_LORA_RANK + KV_LORA_RANK]
    nkv = ckv * lax.rsqrt(jnp.mean(ckv * ckv, axis=-1, keepdims=True) + NORM_EPS) * gkv_ref[...]
    vt_ref[0] = _dot(wvt_ref[...], nkv.T).astype(vt_ref.dtype)
    k_nope = _dot(nkv, wk_ref[...])
    lane = lax.broadcasted_iota(jnp.int32, (1, LANES), 1)
    rope_lane = (lane >= QK_NOPE_DIM) & (lane < QK_NOPE_DIM + QK_ROPE_DIM)
    pe = lat[:, Q_LORA_RANK + KV_LORA_RANK:]
    k_pe = pe * jnp.where(rope_lane, cos_t.T, 0.0) + pltpu.roll(pe, LANES - QK_ROPE_DIM, 1) * sin_t.T
    for h in range(B_HEADS):
        sl = slice(h * LANES, (h + 1) * LANES)
        k_ref[:, sl] = (k_nope[:, sl] + k_pe).astype(k_ref.dtype)


def _mla_prep(lat, pos3, invf, g_q, g_kv, wqt_p, wk_p, wvt_p, bsz, seq, tm):
    t = lat.shape[0]
    nt = seq // tm
    const = lambda shape: pl.BlockSpec(shape, lambda i: (0,) * len(shape))
    return pl.pallas_call(
        _mla_prep_kernel,
        grid=(t // tm,),
        in_specs=[
            pl.BlockSpec((tm, LAT_COLS), lambda i: (i, 0)),
            pl.BlockSpec((1, 1, tm), lambda i: (i, 0, 0)),
            const((LANES, 1)), const((Q_LORA_RANK, 1)), const((1, KV_LORA_RANK)),
            const((B_HEADS * LANES, Q_LORA_RANK)), const((KV_LORA_RANK, B_HEADS * LANES)),
            const((B_WIDTH, KV_LORA_RANK)),
        ],
        out_specs=[
            pl.BlockSpec((1, B_HEADS * LANES, tm), lambda i: (i // nt, 0, i % nt)),
            pl.BlockSpec((tm, B_HEADS * LANES), lambda i: (i, 0)),
            pl.BlockSpec((1, B_WIDTH, tm), lambda i: (i // nt, 0, i % nt)),
        ],
        out_shape=[
            jax.ShapeDtypeStruct((bsz, B_HEADS * LANES, seq), BF16),
            jax.ShapeDtypeStruct((t, B_HEADS * LANES), BF16),
            jax.ShapeDtypeStruct((bsz, B_WIDTH, seq), BF16),
        ],
        compiler_params=pltpu.CompilerParams(
            dimension_semantics=("arbitrary",), vmem_limit_bytes=VMEM_LIMIT),
        name="mla_prep",
    )(lat, pos3, invf, g_q, g_kv, wqt_p, wk_p, wvt_p)


NEG_BIG = -1e30
ATT_TK = 512


def _attn_kernel(qt_ref, k_ref, vt_ref, zb_ref, out_ref, s_ref):
    tq = qt_ref.shape[2]
    tk = ATT_TK
    i = pl.program_id(2)
    q_t = (qt_ref[0, :LANES, :], qt_ref[0, LANES:, :])
    ones_rows = jnp.ones((16, tk), BF16)

    def units(q0):
        return [(h, c) for h in range(2) for c in range(q0, tq, tk)]

    def score_unit(j, h, c):
        start = pl.multiple_of(j * tk, tk)
        return jnp.dot(k_ref[0, pl.ds(start, tk), h * LANES:(h + 1) * LANES], q_t[h][:, c:c + tk],
                       preferred_element_type=F32)

    def tile(j, carry, get_s, prefetch, q0=0):
        start = pl.multiple_of(j * tk, tk)
        vb = vt_ref[0, :, pl.ds(start, tk)]
        v_aug = [jnp.concatenate([vb[h * V_HEAD_DIM:(h + 1) * V_HEAD_DIM, :], ones_rows], axis=0)
                 for h in range(2)]
        cols = [[tuple(x[:, :q0] for x in carry[h])] if q0 else [] for h in range(2)]
        for u, (h, c) in enumerate(units(q0)):
            prefetch(u)
            s_u = get_s(u, h, c)
            m, l, acc = [x[:, c:c + tk] for x in carry[h]]
            m_new = jnp.maximum(m, jnp.max(s_u, axis=0, keepdims=True))
            alpha = jnp.exp2(m - m_new)
            p_u = jnp.exp2(s_u - m_new).astype(BF16)
            pv = jnp.dot(v_aug[h], p_u, preferred_element_type=F32)
            cols[h].append((m_new, l * alpha + pv[V_HEAD_DIM:V_HEAD_DIM + 1],
                            acc * alpha + pv[:V_HEAD_DIM]))
        return tuple(tuple(jnp.concatenate(parts, axis=1) if len(parts) > 1 else parts[0]
                           for parts in zip(*cols[h])) for h in range(2))

    init = tuple((jnp.full((1, tq), NEG_BIG, F32), jnp.zeros((1, tq), F32),
                  jnp.zeros((V_HEAD_DIM, tq), F32)) for _ in range(2))
    n_full = i * (tq // tk)
    all_units = units(0)

    def visible_tile(j, slot, carry):
        def prefetch(u):
            h, c = all_units[u]
            s_ref[1 - slot, u] = score_unit(j + 1, h, c)
        return tile(j, carry, lambda u, h, c: s_ref[slot, u], prefetch)

    def two_tiles(jj, carry):
        return visible_tile(2 * jj + 1, 1, visible_tile(2 * jj, 0, carry))

    assert (tq // tk) % 2 == 0
    for u, (h, c) in enumerate(all_units):
        s_ref[0, u] = score_unit(0, h, c)
    carry = lax.fori_loop(0, n_full // 2, two_tiles, init)

    key = lax.broadcasted_iota(jnp.int32, (tk, tk), 0)
    query = lax.broadcasted_iota(jnp.int32, (tk, tk), 1)
    ahead = {}
    for d in range(tq // tk):
        q0 = d * tk
        here, ahead = ahead, {}
        nxt = units(q0 + tk) if d + 1 < tq // tk else []

        def prefetch(u, d=d, nxt=nxt, ahead=ahead):
            if u < len(nxt):
                ahead[nxt[u]] = score_unit(n_full + d + 1, *nxt[u])

        def get_s(u, h, c, d=d, q0=q0, here=here):
            s_u = s_ref[0, u] if d == 0 else here[h, c]
            return jnp.where(key <= query, s_u, NEG_BIG) if c == q0 else s_u

        carry = tile(n_full + d, carry, get_s, prefetch, q0)
    o_t = jnp.concatenate([acc / l for (_, l, acc) in carry], axis=0)
    z = zb_ref[0]
    out_ref[0] = (o_t.T * (z * _sigmoid(z))).astype(out_ref.dtype)


def _attention(q_t, k, v_t, zb, tq):
    bsz, seq, _ = k.shape
    return pl.pallas_call(
        _attn_kernel,
        grid=(bsz, B_HEADS // 2, seq // tq),
        in_specs=[
            pl.BlockSpec((1, 2 * LANES, tq), lambda b, p, i: (b, p, i)),
            pl.BlockSpec((1, seq, 2 * LANES), lambda b, p, i: (b, 0, p)),
            pl.BlockSpec((1, 2 * V_HEAD_DIM, seq), lambda b, p, i: (b, p, 0)),
            pl.BlockSpec((1, tq, LANES), lambda b, p, i: (b, i, p)),
        ],
        out_specs=pl.BlockSpec((1, tq, LANES), lambda b, p, i: (b, i, p)),
        out_shape=jax.ShapeDtypeStruct((bsz, seq, B_WIDTH), BF16),
        scratch_shapes=[pltpu.VMEM((2, 2 * (tq // ATT_TK), ATT_TK, ATT_TK), F32)],
        compiler_params=pltpu.CompilerParams(
            dimension_semantics=("arbitrary", "arbitrary", "arbitrary"),
            vmem_limit_bytes=VMEM_LIMIT),
        name="mla_attn",
    )(q_t, k, v_t, zb)


def _out_kernel(ya_ref, yb_ref, gl_ref, x_ref, bg_ref, woa_ref, wob_ref, wo_ref, gp_ref, out_ref):
    y_a = jnp.dot(ya_ref[...], woa_ref[...], preferred_element_type=F32)
    y_b = jnp.dot(yb_ref[...], wob_ref[...], preferred_element_type=F32)
    gates = _sigmoid(gl_ref[...] + bg_ref[...])
    merged = gates[:, :D_MODEL] * y_a + gates[:, D_MODEL:] * y_b
    o = _dot(merged, wo_ref[...])
    ms = jnp.mean(o * o, axis=-1, keepdims=True)
    out_ref[...] = x_ref[...] + o * lax.rsqrt(ms + NORM_EPS) * gp_ref[...]


def _out_proj(ya, yb, gl, x2, b_gate, woa, wob, wo, g_post, tm):
    t = x2.shape[0]
    const = lambda shape: pl.BlockSpec(shape, lambda i: (0,) * len(shape))
    rows = lambda n: pl.BlockSpec((tm, n), lambda i: (i, 0))
    return pl.pallas_call(
        _out_kernel,
        grid=(t // tm,),
        in_specs=[
            rows(A_WIDTH), rows(B_WIDTH), rows(2 * D_MODEL), rows(D_MODEL),
            const((1, 2 * D_MODEL)), const((A_WIDTH, D_MODEL)), const((B_WIDTH, D_MODEL)),
            const((D_MODEL, D_MODEL)), const((1, D_MODEL)),
        ],
        out_specs=rows(D_MODEL),
        out_shape=jax.ShapeDtypeStruct((t, D_MODEL), F32),
        compiler_params=pltpu.CompilerParams(
            dimension_semantics=("arbitrary",), vmem_limit_bytes=VMEM_LIMIT),
        name="out_proj",
    )(ya, yb, gl, x2, b_gate, woa, wob, wo, g_post)


def _rope_partner(w_pe):
    half = QK_ROPE_DIM // 2
    return jnp.concatenate([-w_pe[..., half:], w_pe[..., :half]], axis=-1)


def _prep_weights(w_in, w_decay_up, w_iclr_up, w_uq, w_ukv):
    o = np.cumsum([0, SHIFT_COLS, A_WIDTH, Q_LORA_RANK, KV_LORA_RANK, QK_ROPE_DIM, B_WIDTH])
    w_pe = w_in[:, o[4]:o[5]]
    kpe_blk = jnp.concatenate(
        [jnp.zeros((D_MODEL, QK_NOPE_DIM), F32), w_pe, _rope_partner(w_pe)], axis=-1)
    w_p = jnp.concatenate(
        [w_in[:, :o[4]], kpe_blk, w_in[:, o[5]:]], axis=-1).astype(BF16)
    zeros_lora = jnp.zeros((DECAY_LORA, A_WIDTH), F32)
    wd_p = jnp.concatenate([w_decay_up, zeros_lora], axis=0).astype(BF16)
    wa_p = jnp.concatenate([zeros_lora, w_iclr_up], axis=0).astype(BF16)
    wq = w_uq.reshape(Q_LORA_RANK, B_HEADS, QK_NOPE_DIM + QK_ROPE_DIM)
    wqt_p = jnp.concatenate(
        [wq, _rope_partner(wq[..., QK_NOPE_DIM:])], axis=-1).reshape(Q_LORA_RANK, -1).T.astype(BF16)
    wkv = w_ukv.reshape(KV_LORA_RANK, B_HEADS, QK_NOPE_DIM + V_HEAD_DIM)
    wk_p = jnp.concatenate(
        [wkv[..., :QK_NOPE_DIM], jnp.zeros_like(wkv[..., :QK_NOPE_DIM])],
        axis=-1).reshape(KV_LORA_RANK, -1).astype(BF16)
    wvt_p = wkv[..., QK_NOPE_DIM:].reshape(KV_LORA_RANK, -1).T.astype(BF16)
    return w_p, wd_p, wa_p, wqt_p, wk_p, wvt_p


def kernel(x, positions, g_pre, w_in, b_gate, mu_shift, w0, w_decay_up, a0, w_iclr_up, k_k, k_a,
           r_k, gn_gain, gn_bias, w_out_a, g_q, w_uq, g_kv, w_ukv, w_out_b, w_o, g_post):
    bsz, seq, _ = x.shape
    t = bsz * seq
    tm = 256
    row = lambda a: a.reshape(1, -1).astype(F32)
    w_p, wd_p, wa_p, wqt_p, wk_p, wvt_p = _prep_weights(w_in, w_decay_up, w_iclr_up, w_uq, w_ukv)
    x2 = x.reshape(t, D_MODEL)

    feat, za, lat, zb, gl = _inproj(x2, row(g_pre), w_p, tm)

    ya = _rwkv(feat, za, row(mu_shift), row(w0), row(a0), row(k_k), row(k_a), row(r_k),
               row(gn_gain), row(gn_bias), wd_p, wa_p, bsz, seq, 256)

    lane = np.arange(LANES)
    freq = np.where((lane >= QK_NOPE_DIM) & (lane < QK_NOPE_DIM + QK_ROPE_DIM),
                    ROPE_THETA ** (-((lane - QK_NOPE_DIM) % (QK_ROPE_DIM // 2)) * 2.0 / QK_ROPE_DIM),
                    0.0)
    invf = jnp.asarray(freq.reshape(LANES, 1), F32)
    pos3 = positions.reshape(t // tm, 1, tm)
    q_t, k, v_t = _mla_prep(lat, pos3, invf, g_q.reshape(-1, 1).astype(F32), row(g_kv),
                            wqt_p, wk_p, wvt_p, bsz, seq, tm)
    yb = _attention(q_t, k.reshape(bsz, seq, -1), v_t, zb.reshape(bsz, seq, -1), 1024)

    out = _out_proj(ya, yb.reshape(t, B_WIDTH), gl, x2, row(b_gate), w_out_a.astype(BF16),
                    w_out_b.astype(BF16), w_o.astype(BF16), row(g_post), tm)
    return out.reshape(bsz, seq, D_MODEL)
```

```python
import functools
import math

import jax
import jax.numpy as jnp
import numpy as np
from jax import lax
from jax.experimental import pallas as pl
from jax.experimental.pallas import tpu as pltpu

D_MODEL = 1024
A_HEADS = 8
A_HEAD_DIM = 64
A_WIDTH = A_HEADS * A_HEAD_DIM
DECAY_LORA = 64
ICLR_LORA = 64
DECAY_SCALE = 0.6065306597
GN_EPS = 64e-5
B_HEADS = 8
QK_NOPE_DIM = 64
QK_ROPE_DIM = 32
V_HEAD_DIM = 64
Q_LORA_RANK = 256
KV_LORA_RANK = 128
B_WIDTH = B_HEADS * V_HEAD_DIM
ROPE_THETA = 10000.0
NORM_EPS = 1e-6
SHIFT_COLS = 3 * A_WIDTH + DECAY_LORA + ICLR_LORA

LANES = 128
PAIR = 2 * A_HEAD_DIM
N_PAIRS = A_HEADS // 2
CHUNK = 64
LAT_COLS = 512
PROJ_COLS = SHIFT_COLS + A_WIDTH + LAT_COLS + B_WIDTH + 2 * D_MODEL
VMEM_LIMIT = 52 * 1024 * 1024

F32 = jnp.float32
BF16 = jnp.bfloat16
HI = lax.Precision.HIGHEST


def _dot(a, b):
    return jnp.dot(a.astype(BF16), b.astype(BF16), preferred_element_type=F32)


def _dot_nt(a, b):
    return lax.dot_general(a.astype(BF16), b.astype(BF16), (((1,), (1,)), ((), ())),
                           preferred_element_type=F32)


def _sigmoid(x):
    return 1.0 / (1.0 + jnp.exp(-x))


def _inproj_kernel(x_ref, g_ref, w_ref, bg_ref, feat_ref, sa_ref, lat_ref, sb_ref, gate_ref):
    x = x_ref[...]
    ms = jnp.mean(x * x, axis=-1, keepdims=True)
    u = (x * lax.rsqrt(ms + NORM_EPS) * g_ref[...]).astype(BF16)
    silu = lambda z: z * _sigmoid(z)
    epilogues = (None, silu, None, silu, lambda z: _sigmoid(z + bg_ref[...]))
    off = 0
    for ref, fn in zip((feat_ref, sa_ref, lat_ref, sb_ref, gate_ref), epilogues):
        n = ref.shape[-1]
        y = jnp.dot(u, w_ref[:, off:off + n], preferred_element_type=F32)
        ref[...] = (y if fn is None else fn(y)).astype(ref.dtype)
        off += n


def _inproj(x2, g_pre, w_p, b_gate, tm):
    t = x2.shape[0]
    outs = ((SHIFT_COLS, F32), (A_WIDTH, BF16), (LAT_COLS, F32), (B_WIDTH, BF16),
            (2 * D_MODEL, BF16))
    return pl.pallas_call(
        _inproj_kernel,
        grid=(t // tm,),
        in_specs=[
            pl.BlockSpec((tm, D_MODEL), lambda i: (i, 0)),
            pl.BlockSpec((1, D_MODEL), lambda i: (0, 0)),
            pl.BlockSpec((D_MODEL, PROJ_COLS), lambda i: (0, 0)),
            pl.BlockSpec((1, 2 * D_MODEL), lambda i: (0, 0)),
        ],
        out_specs=[pl.BlockSpec((tm, n), lambda i: (i, 0)) for n, _ in outs],
        out_shape=[jax.ShapeDtypeStruct((t, n), dt) for n, dt in outs],
        compiler_params=pltpu.CompilerParams(
            dimension_semantics=("arbitrary",), vmem_limit_bytes=VMEM_LIMIT),
        name="inproj",
    )(x2, g_pre, w_p, b_gate)


def _split_bf16(x):
    hi = x.astype(BF16)
    return hi, (x - hi.astype(F32)).astype(BF16)


def _rwkv_kernel(feat_ref, sa_ref, mu_ref, w0_ref, a0_ref, kk_ref, ka_ref, rk_ref, gg_ref, gb_ref,
                 wd_ref, wa_ref, out_ref, carry_ref, state_ref):
    tt = feat_ref.shape[0]
    n_chunks = tt // CHUNK

    @pl.when(pl.program_id(1) == 0)
    def _():
        carry_ref[...] = jnp.zeros_like(carry_ref)
        state_ref[...] = jnp.zeros_like(state_ref)

    row_t = lax.broadcasted_iota(jnp.int32, (tt, LANES), 0)
    pos_in_chunk = row_t % CHUNK

    def shifted(blk):
        cols = slice(blk * LANES, (blk + 1) * LANES)
        cur = feat_ref[:, cols]
        prev = jnp.where(row_t == 0, carry_ref[0:1, cols], pltpu.roll(cur, 1, 0))
        return cur + mu_ref[:, cols] * (prev - cur)

    ri = lax.broadcasted_iota(jnp.int32, (LANES, LANES), 0)
    ci = lax.broadcasted_iota(jnp.int32, (LANES, LANES), 1)
    same_head = (ri // A_HEAD_DIM) == (ci // A_HEAD_DIM)
    strict_bd = same_head & (ci < ri)
    incl_bd = same_head & (ci <= ri)
    eye = (ri == ci).astype(F32)
    ones2 = jnp.concatenate([same_head, same_head], axis=0).astype(BF16)
    zeros_sq = jnp.zeros((LANES, LANES), F32)
    lane = lax.broadcasted_iota(jnp.int32, (1, LANES), 1)
    m_a = (lane < A_HEAD_DIM).astype(F32)
    m_b = 1.0 - m_a

    def head_sum(x):
        hi, lo = _split_bf16(x)
        return jnp.dot(jnp.concatenate([hi, lo], axis=1), ones2, preferred_element_type=F32)

    def chunk_cumsum(x):
        for sh in (1, 2, 4, 8, 16, 32):
            x = x + jnp.where(pos_in_chunk >= sh, pltpu.roll(x, sh, 0), 0.0)
        return x

    def stack(z):
        return jnp.concatenate([z * m_a, z * m_b], axis=0)

    def unstack(z):
        return z[:CHUNK] + z[CHUNK:]

    la = shifted(3 * N_PAIRS)
    tanh_la = jnp.tanh(la).astype(BF16)
    la_bf = la.astype(BF16)

    pairs = range(N_PAIRS)
    items = [(p, ch) for p in pairs for ch in range(n_chunks)]
    each = lambda fn: {it: fn(it) for it in items}
    pcols = lambda p: slice(p * LANES, (p + 1) * LANES)

    r = {p: shifted(p) for p in pairs}
    k = {p: shifted(N_PAIRS + p) for p in pairs}
    v = {p: shifted(2 * N_PAIRS + p) for p in pairs}
    carry_ref[0:1, :] = feat_ref[tt - 1:tt, :]
    w_logit = {p: w0_ref[:, pcols(p)] + jnp.dot(tanh_la, wd_ref[:, pcols(p)],
                                                preferred_element_type=F32) for p in pairs}
    a = {p: _sigmoid(a0_ref[:, pcols(p)] + jnp.dot(la_bf, wa_ref[:, pcols(p)],
                                                   preferred_element_type=F32)) for p in pairs}
    lw = {p: -DECAY_SCALE * _sigmoid(w_logit[p]) for p in pairs}
    kk = {p: k[p] * kk_ref[:, pcols(p)] for p in pairs}
    k2 = {p: k[p] * (1.0 + (a[p] - 1.0) * ka_ref[:, pcols(p)]) for p in pairs}
    sums = {p: head_sum(jnp.concatenate([kk[p] * kk[p], r[p] * k2[p] * rk_ref[:, pcols(p)]],
                                        axis=0)) for p in pairs}
    kk = {p: kk[p] / jnp.maximum(jnp.sqrt(sums[p][:tt]), 1e-12) for p in pairs}
    bonus = {p: sums[p][tt:] * v[p] for p in pairs}
    b = {p: kk[p] * a[p] for p in pairs}
    cum = {p: chunk_cumsum(lw[p]) for p in pairs}

    def chunk_operands(it):
        p, ch = it
        sl = slice(ch * CHUNK, (ch + 1) * CHUNK)
        lwc, cumc, rc, kc, vc, bc, kkc = lw[p][sl], cum[p][sl], r[p][sl], k2[p][sl], v[p][sl], \
            b[p][sl], kk[p][sl]
        last = cumc[CHUNK - 1:CHUNK, :]
        p_inv = jnp.exp(-cumc)
        to_end = jnp.exp(last - cumc)
        return dict(
            cts=stack(-kkc * jnp.exp(cumc - lwc)), rt=rc * jnp.exp(cumc),
            bt=bc * p_inv, kt=kc * p_inv, vs=stack(vc), decay=eye * jnp.exp(last),
            ends_t=jnp.concatenate([stack(bc * to_end).T, stack(kc * to_end).T], axis=1))

    ops = each(chunk_operands)
    g = each(lambda it: _dot_nt(
        jnp.concatenate([ops[it]["cts"], stack(ops[it]["rt"])], axis=0),
        jnp.concatenate([ops[it]["bt"], ops[it]["bt"], ops[it]["kt"], ops[it]["kt"]], axis=0)))
    nb = each(lambda it: jnp.where(strict_bd, g[it][:PAIR, :PAIR], 0.0))
    nk = each(lambda it: jnp.where(strict_bd, g[it][:PAIR, PAIR:], 0.0))
    m_bk = each(lambda it: jnp.concatenate(
        [jnp.where(incl_bd, g[it][PAIR:, :PAIR], 0.0),
         jnp.where(incl_bd, g[it][PAIR:, PAIR:], 0.0)], axis=1))
    t_inv = each(lambda it: eye + nb[it])
    npow = each(lambda it: _dot(nb[it], nb[it]))
    nkv = each(lambda it: _dot(nk[it], ops[it]["vs"]))
    for _ in range(int(math.log2(CHUNK)) - 2):
        both = each(lambda it: _dot(npow[it], jnp.concatenate([npow[it], t_inv[it]], axis=1)))
        npow = each(lambda it: both[it][:, :PAIR])
        t_inv = each(lambda it: t_inv[it] + both[it][:, PAIR:])
    last_term = each(lambda it: _dot(npow[it], t_inv[it]))
    t_inv = each(lambda it: t_inv[it] + last_term[it])
    w12 = each(lambda it: _dot(t_inv[it], jnp.concatenate([ops[it]["cts"], nkv[it]], axis=1)))
    big = each(lambda it: _dot(
        jnp.concatenate([ops[it]["ends_t"], m_bk[it]], axis=0),
        jnp.concatenate([w12[it], jnp.concatenate([zeros_sq, ops[it]["vs"]], axis=1)], axis=0)))
    gt = each(lambda it: (big[it][:PAIR, :PAIR] + ops[it]["decay"]).astype(BF16))
    q1 = each(lambda it: ops[it]["rt"] + unstack(big[it][PAIR:, :PAIR]))

    h = {p: state_ref[p] for p in pairs}
    h_at = {}
    for ch in range(n_chunks):
        for p in pairs:
            it = (p, ch)
            h_at[it] = h[p]
            h_hi, h_lo = _split_bf16(h[p])
            h[p] = jnp.dot(jnp.concatenate([gt[it], gt[it]], axis=1),
                           jnp.concatenate([h_hi, h_lo], axis=0),
                           preferred_element_type=F32) + big[it][:PAIR, PAIR:]
    for p in pairs:
        state_ref[p] = h[p]
    y_it = each(lambda it: _dot(q1[it], h_at[it]) + unstack(big[it][PAIR:, PAIR:]))

    y = {p: jnp.concatenate([y_it[(p, ch)] for ch in range(n_chunks)], axis=0) for p in pairs}
    mean = {p: head_sum(y[p]) * (1.0 / A_HEAD_DIM) for p in pairs}
    d = {p: y[p] - mean[p] for p in pairs}
    var = {p: head_sum(d[p] * d[p]) * (1.0 / A_HEAD_DIM) for p in pairs}
    for p in pairs:
        y_n = d[p] * lax.rsqrt(var[p] + GN_EPS) * gg_ref[:, pcols(p)] + gb_ref[:, pcols(p)] + bonus[p]
        out_ref[:, pcols(p)] = (y_n * sa_ref[:, pcols(p)].astype(F32)).astype(out_ref.dtype)


def _rwkv(feat, za, mu, w0, a0, k_k, k_a, r_k, gn_gain, gn_bias, wd_p, wa_p, bsz, seq, tt):
    nt = seq // tt
    const = lambda shape: pl.BlockSpec(shape, lambda b, t: (0,) * len(shape))
    rows = lambda n: pl.BlockSpec((tt, n), lambda b, t: (b * nt + t, 0))
    return pl.pallas_call(
        _rwkv_kernel,
        grid=(bsz, nt),
        in_specs=[rows(SHIFT_COLS), rows(A_WIDTH), const((1, SHIFT_COLS))]
        + [const((1, A_WIDTH))] * 7 + [const((LANES, A_WIDTH))] * 2,
        out_specs=rows(A_WIDTH),
        out_shape=jax.ShapeDtypeStruct((bsz * seq, A_WIDTH), BF16),
        scratch_shapes=[pltpu.VMEM((8, SHIFT_COLS), F32),
                        pltpu.VMEM((N_PAIRS, LANES, LANES), F32)],
        compiler_params=pltpu.CompilerParams(
            dimension_semantics=("arbitrary", "arbitrary"), vmem_limit_bytes=VMEM_LIMIT),
        name="rwkv7",
    )(feat, za, mu, w0, a0, k_k, k_a, r_k, gn_gain, gn_bias, wd_p, wa_p)


QK_SCALE = math.log2(math.e) / math.sqrt(QK_NOPE_DIM + QK_ROPE_DIM)


def _mla_prep_kernel(lat_ref, pos_ref, invf_ref, gq_ref, gkv_ref, wqt_ref, wk_ref, wvt_ref,
                     qt_ref, k_ref, vt_ref):
    lat = lat_ref[...]
    tm = lat.shape[0]
    pos = pos_ref[0].astype(F32)
    ang = invf_ref[...] * pos
    pad = jnp.zeros((LANES - QK_NOPE_DIM - QK_ROPE_DIM, tm), F32)
    cos_q = jnp.concatenate([jnp.ones((QK_NOPE_DIM, tm), F32), jnp.cos(ang), pad], axis=0)
    sin_t = jnp.concatenate([jnp.zeros((QK_NOPE_DIM, tm), F32), jnp.sin(ang), pad], axis=0)

    cq_t = jnp.concatenate([lat[:, j * LANES:(j + 1) * LANES].T
                            for j in range(Q_LORA_RANK // LANES)], axis=0)
    nq_t = cq_t * lax.rsqrt(jnp.mean(cq_t * cq_t, axis=0, keepdims=True) + NORM_EPS) * gq_ref[...]
    q_raw = _dot(wqt_ref[...], nq_t)
    for h in range(B_HEADS):
        sl = slice(h * LANES, (h + 1) * LANES)
        blk = q_raw[sl]
        rot = blk * cos_q + pltpu.roll(blk, LANES - QK_ROPE_DIM, 0) * sin_t
        qt_ref[0, sl, :] = (rot * QK_SCALE).astype(qt_ref.dtype)

    ckv = lat[:, Q_LORA_RANK:Q_LORA_RANK + KV_LORA_RANK]
    nkv = ckv * lax.rsqrt(jnp.mean(ckv * ckv, axis=-1, keepdims=True) + NORM_EPS) * gkv_ref[...]
    vt_ref[0] = _dot(wvt_ref[...], nkv.T).astype(vt_ref.dtype)
    k_nope = _dot(nkv, wk_ref[...])
    lane = lax.broadcasted_iota(jnp.int32, (1, LANES), 1)
    rope_lane = (lane >= QK_NOPE_DIM) & (lane < QK_NOPE_DIM + QK_ROPE_DIM)
    pe = lat[:, Q_LORA_RANK + KV_LORA_RANK:]
    k_pe = pe * jnp.where(rope_lane, cos_q.T, 0.0) + pltpu.roll(pe, LANES - QK_ROPE_DIM, 1) * sin_t.T
    for h in range(B_HEADS):
        sl = slice(h * LANES, (h + 1) * LANES)
        k_ref[:, sl] = (k_nope[:, sl] + k_pe).astype(k_ref.dtype)


def _mla_prep(lat, pos3, invf, g_q, g_kv, wqt_p, wk_p, wvt_p, bsz, seq, tm):
    t = lat.shape[0]
    nt = seq // tm
    const = lambda shape: pl.BlockSpec(shape, lambda i: (0,) * len(shape))
    return pl.pallas_call(
        _mla_prep_kernel,
        grid=(t // tm,),
        in_specs=[
            pl.BlockSpec((tm, LAT_COLS), lambda i: (i, 0)),
            pl.BlockSpec((1, 1, tm), lambda i: (i, 0, 0)),
            const((QK_ROPE_DIM, 1)), const((Q_LORA_RANK, 1)), const((1, KV_LORA_RANK)),
            const((B_HEADS * LANES, Q_LORA_RANK)), const((KV_LORA_RANK, B_HEADS * LANES)),
            const((B_WIDTH, KV_LORA_RANK)),
        ],
        out_specs=[
            pl.BlockSpec((1, B_HEADS * LANES, tm), lambda i: (i // nt, 0, i % nt)),
            pl.BlockSpec((tm, B_HEADS * LANES), lambda i: (i, 0)),
            pl.BlockSpec((1, B_WIDTH, tm), lambda i: (i // nt, 0, i % nt)),
        ],
        out_shape=[
            jax.ShapeDtypeStruct((bsz, B_HEADS * LANES, seq), BF16),
            jax.ShapeDtypeStruct((t, B_HEADS * LANES), BF16),
            jax.ShapeDtypeStruct((bsz, B_WIDTH, seq), BF16),
        ],
        compiler_params=pltpu.CompilerParams(
            dimension_semantics=("arbitrary",), vmem_limit_bytes=VMEM_LIMIT),
        name="mla_prep",
    )(lat, pos3, invf, g_q, g_kv, wqt_p, wk_p, wvt_p)


NEG_BIG = -1e30
ATT_TK = 512


def _attn_kernel(qt_ref, k_ref, vt_ref, sb_ref, out_ref, s_ref):
    tq = qt_ref.shape[2]
    tk = ATT_TK
    i = pl.program_id(2)
    q_t = (qt_ref[0, :LANES, :], qt_ref[0, LANES:, :])
    ones_rows = jnp.ones((16, tk), BF16)

    def units(q0):
        return [(h, c) for h in range(2) for c in range(q0, tq, tk)]

    def score_unit(j, h, c):
        start = pl.multiple_of(j * tk, tk)
        return jnp.dot(k_ref[0, pl.ds(start, tk), h * LANES:(h + 1) * LANES], q_t[h][:, c:c + tk],
                       preferred_element_type=F32)

    def tile(j, carry, get_s, prefetch, q0=0):
        start = pl.multiple_of(j * tk, tk)
        vb = vt_ref[0, :, pl.ds(start, tk)]
        v_aug = [jnp.concatenate([vb[h * V_HEAD_DIM:(h + 1) * V_HEAD_DIM, :], ones_rows], axis=0)
                 for h in range(2)]
        cols = [[tuple(x[:, :q0] for x in carry[h])] if q0 else [] for h in range(2)]
        for u, (h, c) in enumerate(units(q0)):
            prefetch(u)
            s_u = get_s(u, h, c)
            m, l, acc = [x[:, c:c + tk] for x in carry[h]]
            m_new = jnp.maximum(m, jnp.max(s_u, axis=0, keepdims=True))
            alpha = jnp.exp2(m - m_new)
            p_u = jnp.exp2(s_u - m_new).astype(BF16)
            pv = jnp.dot(v_aug[h], p_u, preferred_element_type=F32)
            cols[h].append((m_new, l * alpha + pv[V_HEAD_DIM:V_HEAD_DIM + 1],
                            acc * alpha + pv[:V_HEAD_DIM]))
        return tuple(tuple(jnp.concatenate(parts, axis=1) if len(parts) > 1 else parts[0]
                           for parts in zip(*cols[h])) for h in range(2))

    init = tuple((jnp.full((1, tq), NEG_BIG, F32), jnp.zeros((1, tq), F32),
                  jnp.zeros((V_HEAD_DIM, tq), F32)) for _ in range(2))
    n_full = i * (tq // tk)
    all_units = units(0)

    def visible_tile(j, slot, carry):
        def prefetch(u):
            h, c = all_units[u]
            s_ref[1 - slot, u] = score_unit(j + 1, h, c)
        return tile(j, carry, lambda u, h, c: s_ref[slot, u], prefetch)

    def two_tiles(jj, carry):
        return visible_tile(2 * jj + 1, 1, visible_tile(2 * jj, 0, carry))

    assert (tq // tk) % 2 == 0
    for u, (h, c) in enumerate(all_units):
        s_ref[0, u] = score_unit(0, h, c)
    carry = lax.fori_loop(0, n_full // 2, two_tiles, init)

    key = lax.broadcasted_iota(jnp.int32, (tk, tk), 0)
    query = lax.broadcasted_iota(jnp.int32, (tk, tk), 1)
    ahead = {}
    for d in range(tq // tk):
        q0 = d * tk
        here, ahead = ahead, {}
        nxt = units(q0 + tk) if d + 1 < tq // tk else []

        def prefetch(u, d=d, nxt=nxt, ahead=ahead):
            if u < len(nxt):
                ahead[nxt[u]] = score_unit(n_full + d + 1, *nxt[u])

        def get_s(u, h, c, d=d, q0=q0, here=here):
            s_u = s_ref[0, u] if d == 0 else here[h, c]
            return jnp.where(key <= query, s_u, NEG_BIG) if c == q0 else s_u

        carry = tile(n_full + d, carry, get_s, prefetch, q0)
    o_t = jnp.concatenate([acc / l for (_, l, acc) in carry], axis=0)
    out_ref[0] = (o_t.T * sb_ref[0].astype(F32)).astype(out_ref.dtype)


def _attention(q_t, k, v_t, zb, tq):
    bsz, seq, _ = k.shape
    return pl.pallas_call(
        _attn_kernel,
        grid=(bsz, B_HEADS // 2, seq // tq),
        in_specs=[
            pl.BlockSpec((1, 2 * LANES, tq), lambda b, p, i: (b, p, i)),
            pl.BlockSpec((1, seq, 2 * LANES), lambda b, p, i: (b, 0, p)),
            pl.BlockSpec((1, 2 * V_HEAD_DIM, seq), lambda b, p, i: (b, p, 0)),
            pl.BlockSpec((1, tq, LANES), lambda b, p, i: (b, i, p)),
        ],
        out_specs=pl.BlockSpec((1, tq, LANES), lambda b, p, i: (b, i, p)),
        out_shape=jax.ShapeDtypeStruct((bsz, seq, B_WIDTH), BF16),
        scratch_shapes=[pltpu.VMEM((2, 2 * (tq // ATT_TK), ATT_TK, ATT_TK), F32)],
        compiler_params=pltpu.CompilerParams(
            dimension_semantics=("arbitrary", "arbitrary", "arbitrary"),
            vmem_limit_bytes=VMEM_LIMIT),
        name="mla_attn",
    )(q_t, k, v_t, zb)


def _out_kernel(ya_ref, yb_ref, gate_ref, x_ref, woa_ref, wob_ref, wo_ref, gp_ref, out_ref):
    y_a = jnp.dot(ya_ref[...], woa_ref[...], preferred_element_type=F32)
    y_b = jnp.dot(yb_ref[...], wob_ref[...], preferred_element_type=F32)
    gates = gate_ref[...].astype(F32)
    merged = gates[:, :D_MODEL] * y_a + gates[:, D_MODEL:] * y_b
    o = _dot(merged, wo_ref[...])
    ms = jnp.mean(o * o, axis=-1, keepdims=True)
    out_ref[...] = x_ref[...] + o * lax.rsqrt(ms + NORM_EPS) * gp_ref[...]


def _out_proj(ya, yb, gates, x2, woa, wob, wo, g_post, tm):
    t = x2.shape[0]
    const = lambda shape: pl.BlockSpec(shape, lambda i: (0,) * len(shape))
    rows = lambda n: pl.BlockSpec((tm, n), lambda i: (i, 0))
    return pl.pallas_call(
        _out_kernel,
        grid=(t // tm,),
        in_specs=[
            rows(A_WIDTH), rows(B_WIDTH), rows(2 * D_MODEL), rows(D_MODEL),
            const((A_WIDTH, D_MODEL)), const((B_WIDTH, D_MODEL)),
            const((D_MODEL, D_MODEL)), const((1, D_MODEL)),
        ],
        out_specs=rows(D_MODEL),
        out_shape=jax.ShapeDtypeStruct((t, D_MODEL), F32),
        compiler_params=pltpu.CompilerParams(
            dimension_semantics=("arbitrary",), vmem_limit_bytes=VMEM_LIMIT),
        name="out_proj",
    )(ya, yb, gates, x2, woa, wob, wo, g_post)


def _rope_partner(w_pe):
    half = QK_ROPE_DIM // 2
    return jnp.concatenate([-w_pe[..., half:], w_pe[..., :half]], axis=-1)


def _prep_weights(w_in, w_decay_up, w_iclr_up, w_uq, w_ukv):
    o = np.cumsum([0, SHIFT_COLS, A_WIDTH, Q_LORA_RANK, KV_LORA_RANK, QK_ROPE_DIM, B_WIDTH])
    w_pe = w_in[:, o[4]:o[5]]
    kpe_blk = jnp.concatenate(
        [jnp.zeros((D_MODEL, QK_NOPE_DIM), F32), w_pe, _rope_partner(w_pe)], axis=-1)
    w_p = jnp.concatenate(
        [w_in[:, :o[4]], kpe_blk, w_in[:, o[5]:]], axis=-1).astype(BF16)
    zeros_lora = jnp.zeros((DECAY_LORA, A_WIDTH), F32)
    wd_p = jnp.concatenate([w_decay_up, zeros_lora], axis=0).astype(BF16)
    wa_p = jnp.concatenate([zeros_lora, w_iclr_up], axis=0).astype(BF16)
    wq = w_uq.reshape(Q_LORA_RANK, B_HEADS, QK_NOPE_DIM + QK_ROPE_DIM)
    wqt_p = jnp.concatenate(
        [wq, _rope_partner(wq[..., QK_NOPE_DIM:])], axis=-1).reshape(Q_LORA_RANK, -1).T.astype(BF16)
    wkv = w_ukv.reshape(KV_LORA_RANK, B_HEADS, QK_NOPE_DIM + V_HEAD_DIM)
    wk_p = jnp.concatenate(
        [wkv[..., :QK_NOPE_DIM], jnp.zeros_like(wkv[..., :QK_NOPE_DIM])],
        axis=-1).reshape(KV_LORA_RANK, -1).astype(BF16)
    wvt_p = wkv[..., QK_NOPE_DIM:].reshape(KV_LORA_RANK, -1).T.astype(BF16)
    return w_p, wd_p, wa_p, wqt_p, wk_p, wvt_p


def kernel(x, positions, g_pre, w_in, b_gate, mu_shift, w0, w_decay_up, a0, w_iclr_up, k_k, k_a,
           r_k, gn_gain, gn_bias, w_out_a, g_q, w_uq, g_kv, w_ukv, w_out_b, w_o, g_post):
    bsz, seq, _ = x.shape
    t = bsz * seq
    tm = 256
    row = lambda a: a.reshape(1, -1).astype(F32)
    w_p, wd_p, wa_p, wqt_p, wk_p, wvt_p = _prep_weights(w_in, w_decay_up, w_iclr_up, w_uq, w_ukv)
    x2 = x.reshape(t, D_MODEL)

    feat, silu_a, lat, silu_b, gates = _inproj(x2, row(g_pre), w_p, row(b_gate), tm)

    ya = _rwkv(feat, silu_a, row(mu_shift), row(w0), row(a0), row(k_k), row(k_a), row(r_k),
               row(gn_gain), row(gn_bias), wd_p, wa_p, bsz, seq, 256)

    freq = ROPE_THETA ** (-(np.arange(QK_ROPE_DIM) % (QK_ROPE_DIM // 2)) * 2.0 / QK_ROPE_DIM)
    invf = jnp.asarray(freq.reshape(QK_ROPE_DIM, 1), F32)
    pos3 = positions.reshape(t // tm, 1, tm)
    q_t, k, v_t = _mla_prep(lat, pos3, invf, g_q.reshape(-1, 1).astype(F32), row(g_kv),
                            wqt_p, wk_p, wvt_p, bsz, seq, tm)
    yb = _attention(q_t, k.reshape(bsz, seq, -1), v_t, silu_b.reshape(bsz, seq, -1), 1024)

    out = _out_proj(ya, yb.reshape(t, B_WIDTH), gates, x2, w_out_a.astype(BF16),
                    w_out_b.astype(BF16), w_o.astype(BF16), row(g_post), tm)
    return out.reshape(bsz, seq, D_MODEL)
```

```python
import functools
import math

import jax
import jax.numpy as jnp
import numpy as np
from jax import lax
from jax.experimental import pallas as pl
from jax.experimental.pallas import tpu as pltpu

D_MODEL = 1024
A_HEADS = 8
A_HEAD_DIM = 64
A_WIDTH = A_HEADS * A_HEAD_DIM
DECAY_LORA = 64
ICLR_LORA = 64
DECAY_SCALE = 0.6065306597
GN_EPS = 64e-5
B_HEADS = 8
QK_NOPE_DIM = 64
QK_ROPE_DIM = 32
V_HEAD_DIM = 64
Q_LORA_RANK = 256
KV_LORA_RANK = 128
B_WIDTH = B_HEADS * V_HEAD_DIM
ROPE_THETA = 10000.0
NORM_EPS = 1e-6
SHIFT_COLS = 3 * A_WIDTH + DECAY_LORA + ICLR_LORA

LANES = 128
PAIR = 2 * A_HEAD_DIM
N_PAIRS = A_HEADS // 2
CHUNK = 64
LAT_COLS = 512
PROJ_COLS = SHIFT_COLS + A_WIDTH + LAT_COLS + B_WIDTH + 2 * D_MODEL
VMEM_LIMIT = 52 * 1024 * 1024
TM_INPROJ = 256
TM_ROWS = 512
RWKV_TT = 256

F32 = jnp.float32
BF16 = jnp.bfloat16
HI = lax.Precision.HIGHEST


def _dot(a, b):
    return jnp.dot(a.astype(BF16), b.astype(BF16), preferred_element_type=F32)


def _dot_nt(a, b):
    return lax.dot_general(a.astype(BF16), b.astype(BF16), (((1,), (1,)), ((), ())),
                           preferred_element_type=F32)


def _sigmoid(x):
    return 1.0 / (1.0 + jnp.exp(-x))


def _inproj_kernel(tiles_per_seq, x_ref, g_ref, w_ref, bg_ref, mu_ref,
                   feat_ref, sa_ref, lat_ref, sb_ref, gate_ref, last_ref):
    @pl.when(pl.program_id(0) % tiles_per_seq == 0)
    def _():
        last_ref[...] = jnp.zeros_like(last_ref)

    x = x_ref[...]
    tm = x.shape[0]
    ms = jnp.mean(x * x, axis=-1, keepdims=True)
    u = (x * lax.rsqrt(ms + NORM_EPS) * g_ref[...]).astype(BF16)
    silu = lambda z: z * _sigmoid(z)
    first_row = lax.broadcasted_iota(jnp.int32, (tm, 1), 0) == 0

    def token_shift(feat):
        prev = jnp.where(first_row, last_ref[0:1, :], pltpu.roll(feat, 1, 0))
        last_ref[0:1, :] = feat[tm - 1:tm, :]
        return feat + mu_ref[...] * (prev - feat)

    epilogues = (token_shift, silu, None, silu, lambda z: _sigmoid(z + bg_ref[...]))
    off = 0
    for ref, fn in zip((feat_ref, sa_ref, lat_ref, sb_ref, gate_ref), epilogues):
        n = ref.shape[-1]
        y = jnp.dot(u, w_ref[:, off:off + n], preferred_element_type=F32)
        ref[...] = (y if fn is None else fn(y)).astype(ref.dtype)
        off += n


def _inproj(x2, g_pre, w_p, b_gate, mu, seq, tm):
    t = x2.shape[0]
    outs = ((SHIFT_COLS, F32), (A_WIDTH, BF16), (LAT_COLS, F32), (B_WIDTH, BF16),
            (2 * D_MODEL, BF16))
    return pl.pallas_call(
        functools.partial(_inproj_kernel, seq // tm),
        grid=(t // tm,),
        in_specs=[
            pl.BlockSpec((tm, D_MODEL), lambda i: (i, 0)),
            pl.BlockSpec((1, D_MODEL), lambda i: (0, 0)),
            pl.BlockSpec((D_MODEL, PROJ_COLS), lambda i: (0, 0)),
            pl.BlockSpec((1, 2 * D_MODEL), lambda i: (0, 0)),
            pl.BlockSpec((1, SHIFT_COLS), lambda i: (0, 0)),
        ],
        out_specs=[pl.BlockSpec((tm, n), lambda i: (i, 0)) for n, _ in outs],
        out_shape=[jax.ShapeDtypeStruct((t, n), dt) for n, dt in outs],
        scratch_shapes=[pltpu.VMEM((8, SHIFT_COLS), F32)],
        compiler_params=pltpu.CompilerParams(
            dimension_semantics=("arbitrary",), vmem_limit_bytes=VMEM_LIMIT),
        name="inproj",
    )(x2, g_pre, w_p, b_gate, mu)


def _split_bf16(x):
    hi = x.astype(BF16)
    return hi, (x - hi.astype(F32)).astype(BF16)


def _rwkv_kernel(feat_ref, sa_ref, w0_ref, a0_ref, kk_ref, ka_ref, rk_ref, gg_ref, gb_ref,
                 wd_ref, wa_ref, out_ref, state_ref):
    tt = feat_ref.shape[0]
    n_chunks = tt // CHUNK

    @pl.when(pl.program_id(1) == 0)
    def _():
        state_ref[...] = jnp.zeros_like(state_ref)

    pos_in_chunk = lax.broadcasted_iota(jnp.int32, (tt, LANES), 0) % CHUNK

    def shifted(blk):
        return feat_ref[:, blk * LANES:(blk + 1) * LANES]

    ri = lax.broadcasted_iota(jnp.int32, (LANES, LANES), 0)
    ci = lax.broadcasted_iota(jnp.int32, (LANES, LANES), 1)
    same_head = (ri // A_HEAD_DIM) == (ci // A_HEAD_DIM)
    strict_bd = same_head & (ci < ri)
    incl_bd = same_head & (ci <= ri)
    eye = (ri == ci).astype(F32)
    ones2 = jnp.concatenate([same_head, same_head], axis=0).astype(BF16)
    zeros_sq = jnp.zeros((LANES, LANES), F32)
    lane = lax.broadcasted_iota(jnp.int32, (1, LANES), 1)
    m_a = (lane < A_HEAD_DIM).astype(F32)
    m_b = 1.0 - m_a

    def head_sum(x):
        hi, lo = _split_bf16(x)
        return jnp.dot(jnp.concatenate([hi, lo], axis=1), ones2, preferred_element_type=F32)

    def chunk_cumsum(x):
        for sh in (1, 2, 4, 8, 16, 32):
            x = x + jnp.where(pos_in_chunk >= sh, pltpu.roll(x, sh, 0), 0.0)
        return x

    def stack(z):
        return jnp.concatenate([z * m_a, z * m_b], axis=0)

    def unstack(z):
        return z[:CHUNK] + z[CHUNK:]

    la = shifted(3 * N_PAIRS)
    tanh_la = jnp.tanh(la).astype(BF16)
    la_bf = la.astype(BF16)

    pairs = range(N_PAIRS)
    items = [(p, ch) for p in pairs for ch in range(n_chunks)]
    each = lambda fn: {it: fn(it) for it in items}
    pcols = lambda p: slice(p * LANES, (p + 1) * LANES)

    r = {p: shifted(p) for p in pairs}
    k = {p: shifted(N_PAIRS + p) for p in pairs}
    v = {p: shifted(2 * N_PAIRS + p) for p in pairs}
    w_logit = {p: w0_ref[:, pcols(p)] + jnp.dot(tanh_la, wd_ref[:, pcols(p)],
                                                preferred_element_type=F32) for p in pairs}
    a = {p: _sigmoid(a0_ref[:, pcols(p)] + jnp.dot(la_bf, wa_ref[:, pcols(p)],
                                                   preferred_element_type=F32)) for p in pairs}
    lw = {p: -DECAY_SCALE * _sigmoid(w_logit[p]) for p in pairs}
    kk = {p: k[p] * kk_ref[:, pcols(p)] for p in pairs}
    k2 = {p: k[p] * (1.0 + (a[p] - 1.0) * ka_ref[:, pcols(p)]) for p in pairs}
    sums = {p: head_sum(jnp.concatenate([kk[p] * kk[p], r[p] * k2[p] * rk_ref[:, pcols(p)]],
                                        axis=0)) for p in pairs}
    kk = {p: kk[p] / jnp.maximum(jnp.sqrt(sums[p][:tt]), 1e-12) for p in pairs}
    bonus = {p: sums[p][tt:] * v[p] for p in pairs}
    b = {p: kk[p] * a[p] for p in pairs}
    cum = {p: chunk_cumsum(lw[p]) for p in pairs}

    def chunk_operands(it):
        p, ch = it
        sl = slice(ch * CHUNK, (ch + 1) * CHUNK)
        lwc, cumc, rc, kc, vc, bc, kkc = lw[p][sl], cum[p][sl], r[p][sl], k2[p][sl], v[p][sl], \
            b[p][sl], kk[p][sl]
        last = cumc[CHUNK - 1:CHUNK, :]
        p_inv = jnp.exp(-cumc)
        to_end = jnp.exp(last - cumc)
        return dict(
            cts=stack(-kkc * jnp.exp(cumc - lwc)), rt=rc * jnp.exp(cumc),
            bt=bc * p_inv, kt=kc * p_inv, vs=stack(vc), decay=eye * jnp.exp(last),
            ends_t=jnp.concatenate([stack(bc * to_end).T, stack(kc * to_end).T], axis=1))

    ops = each(chunk_operands)
    g = each(lambda it: _dot_nt(
        jnp.concatenate([ops[it]["cts"], stack(ops[it]["rt"])], axis=0),
        jnp.concatenate([ops[it]["bt"], ops[it]["bt"], ops[it]["kt"], ops[it]["kt"]], axis=0)))
    nb = each(lambda it: jnp.where(strict_bd, g[it][:PAIR, :PAIR], 0.0))
    nk = each(lambda it: jnp.where(strict_bd, g[it][:PAIR, PAIR:], 0.0))
    m_bk = each(lambda it: jnp.concatenate(
        [jnp.where(incl_bd, g[it][PAIR:, :PAIR], 0.0),
         jnp.where(incl_bd, g[it][PAIR:, PAIR:], 0.0)], axis=1))
    t_inv = each(lambda it: eye + nb[it])
    npow = each(lambda it: _dot(nb[it], nb[it]))
    nkv = each(lambda it: _dot(nk[it], ops[it]["vs"]))
    for _ in range(int(math.log2(CHUNK)) - 2):
        both = each(lambda it: _dot(npow[it], jnp.concatenate([npow[it], t_inv[it]], axis=1)))
        npow = each(lambda it: both[it][:, :PAIR])
        t_inv = each(lambda it: t_inv[it] + both[it][:, PAIR:])
    last_term = each(lambda it: _dot(npow[it], t_inv[it]))
    t_inv = each(lambda it: t_inv[it] + last_term[it])
    w12 = each(lambda it: _dot(t_inv[it], jnp.concatenate([ops[it]["cts"], nkv[it]], axis=1)))
    big = each(lambda it: _dot(
        jnp.concatenate([ops[it]["ends_t"], m_bk[it]], axis=0),
        jnp.concatenate([w12[it], jnp.concatenate([zeros_sq, ops[it]["vs"]], axis=1)], axis=0)))
    gt = each(lambda it: (big[it][:PAIR, :PAIR] + ops[it]["decay"]).astype(BF16))
    q1 = each(lambda it: ops[it]["rt"] + unstack(big[it][PAIR:, :PAIR]))

    h = {p: state_ref[p] for p in pairs}
    h_at = {}
    for ch in range(n_chunks):
        for p in pairs:
            it = (p, ch)
            h_at[it] = h[p]
            h_hi, h_lo = _split_bf16(h[p])
            h[p] = jnp.dot(jnp.concatenate([gt[it], gt[it]], axis=1),
                           jnp.concatenate([h_hi, h_lo], axis=0),
                           preferred_element_type=F32) + big[it][:PAIR, PAIR:]
    for p in pairs:
        state_ref[p] = h[p]
    y_it = each(lambda it: _dot(q1[it], h_at[it]) + unstack(big[it][PAIR:, PAIR:]))

    y = {p: jnp.concatenate([y_it[(p, ch)] for ch in range(n_chunks)], axis=0) for p in pairs}
    mean = {p: head_sum(y[p]) * (1.0 / A_HEAD_DIM) for p in pairs}
    d = {p: y[p] - mean[p] for p in pairs}
    var = {p: head_sum(d[p] * d[p]) * (1.0 / A_HEAD_DIM) for p in pairs}
    for p in pairs:
        y_n = d[p] * lax.rsqrt(var[p] + GN_EPS) * gg_ref[:, pcols(p)] + gb_ref[:, pcols(p)] + bonus[p]
        out_ref[:, pcols(p)] = (y_n * sa_ref[:, pcols(p)].astype(F32)).astype(out_ref.dtype)


def _rwkv(feat, silu_a, w0, a0, k_k, k_a, r_k, gn_gain, gn_bias, wd_p, wa_p, bsz, seq, tt):
    nt = seq // tt
    const = lambda shape: pl.BlockSpec(shape, lambda b, t: (0,) * len(shape))
    rows = lambda n: pl.BlockSpec((tt, n), lambda b, t: (b * nt + t, 0))
    return pl.pallas_call(
        _rwkv_kernel,
        grid=(bsz, nt),
        in_specs=[rows(SHIFT_COLS), rows(A_WIDTH)]
        + [const((1, A_WIDTH))] * 7 + [const((LANES, A_WIDTH))] * 2,
        out_specs=rows(A_WIDTH),
        out_shape=jax.ShapeDtypeStruct((bsz * seq, A_WIDTH), BF16),
        scratch_shapes=[pltpu.VMEM((N_PAIRS, LANES, LANES), F32)],
        compiler_params=pltpu.CompilerParams(
            dimension_semantics=("arbitrary", "arbitrary"), vmem_limit_bytes=VMEM_LIMIT),
        name="rwkv7",
    )(feat, silu_a, w0, a0, k_k, k_a, r_k, gn_gain, gn_bias, wd_p, wa_p)


QK_SCALE = math.log2(math.e) / math.sqrt(QK_NOPE_DIM + QK_ROPE_DIM)


def _mla_prep_kernel(lat_ref, pos_ref, invf_ref, gq_ref, gkv_ref, wqt_ref, wk_ref, wvt_ref,
                     qt_ref, k_ref, vt_ref):
    lat = lat_ref[...]
    tm = lat.shape[0]
    pos = pos_ref[0].astype(F32)
    ang = invf_ref[...] * pos
    pad = jnp.zeros((LANES - QK_NOPE_DIM - QK_ROPE_DIM, tm), F32)
    cos_q = jnp.concatenate([jnp.ones((QK_NOPE_DIM, tm), F32), jnp.cos(ang), pad], axis=0)
    sin_t = jnp.concatenate([jnp.zeros((QK_NOPE_DIM, tm), F32), jnp.sin(ang), pad], axis=0)

    cq_t = jnp.concatenate([lat[:, j * LANES:(j + 1) * LANES].T
                            for j in range(Q_LORA_RANK // LANES)], axis=0)
    nq_t = cq_t * lax.rsqrt(jnp.mean(cq_t * cq_t, axis=0, keepdims=True) + NORM_EPS) * gq_ref[...]
    q_raw = _dot(wqt_ref[...], nq_t)
    for h in range(B_HEADS):
        sl = slice(h * LANES, (h + 1) * LANES)
        blk = q_raw[sl]
        rot = blk * cos_q + pltpu.roll(blk, LANES - QK_ROPE_DIM, 0) * sin_t
        qt_ref[0, sl, :] = (rot * QK_SCALE).astype(qt_ref.dtype)

    ckv = lat[:, Q_LORA_RANK:Q_LORA_RANK + KV_LORA_RANK]
    nkv = ckv * lax.rsqrt(jnp.mean(ckv * ckv, axis=-1, keepdims=True) + NORM_EPS) * gkv_ref[...]
    vt_ref[0] = _dot(wvt_ref[...], nkv.T).astype(vt_ref.dtype)
    k_nope = _dot(nkv, wk_ref[...])
    lane = lax.broadcasted_iota(jnp.int32, (1, LANES), 1)
    rope_lane = (lane >= QK_NOPE_DIM) & (lane < QK_NOPE_DIM + QK_ROPE_DIM)
    pe = lat[:, Q_LORA_RANK + KV_LORA_RANK:]
    k_pe = pe * jnp.where(rope_lane, cos_q.T, 0.0) + pltpu.roll(pe, LANES - QK_ROPE_DIM, 1) * sin_t.T
    for h in range(B_HEADS):
        sl = slice(h * LANES, (h + 1) * LANES)
        k_ref[:, sl] = (k_nope[:, sl] + k_pe).astype(k_ref.dtype)


def _mla_prep(lat, pos3, invf, g_q, g_kv, wqt_p, wk_p, wvt_p, bsz, seq, tm):
    t = lat.shape[0]
    nt = seq // tm
    const = lambda shape: pl.BlockSpec(shape, lambda i: (0,) * len(shape))
    return pl.pallas_call(
        _mla_prep_kernel,
        grid=(t // tm,),
        in_specs=[
            pl.BlockSpec((tm, LAT_COLS), lambda i: (i, 0)),
            pl.BlockSpec((1, 1, tm), lambda i: (i, 0, 0)),
            const((QK_ROPE_DIM, 1)), const((Q_LORA_RANK, 1)), const((1, KV_LORA_RANK)),
            const((B_HEADS * LANES, Q_LORA_RANK)), const((KV_LORA_RANK, B_HEADS * LANES)),
            const((B_WIDTH, KV_LORA_RANK)),
        ],
        out_specs=[
            pl.BlockSpec((1, B_HEADS * LANES, tm), lambda i: (i // nt, 0, i % nt)),
            pl.BlockSpec((tm, B_HEADS * LANES), lambda i: (i, 0)),
            pl.BlockSpec((1, B_WIDTH, tm), lambda i: (i // nt, 0, i % nt)),
        ],
        out_shape=[
            jax.ShapeDtypeStruct((bsz, B_HEADS * LANES, seq), BF16),
            jax.ShapeDtypeStruct((t, B_HEADS * LANES), BF16),
            jax.ShapeDtypeStruct((bsz, B_WIDTH, seq), BF16),
        ],
        compiler_params=pltpu.CompilerParams(
            dimension_semantics=("arbitrary",), vmem_limit_bytes=VMEM_LIMIT),
        name="mla_prep",
    )(lat, pos3, invf, g_q, g_kv, wqt_p, wk_p, wvt_p)


NEG_BIG = -1e30
ATT_TK = 512
ATT_TQ = 1024


def _attn_kernel(qt_ref, k_ref, vt_ref, sb_ref, out_ref, s_ref):
    seq = k_ref.shape[1]
    tq, tk = ATT_TQ, ATT_TK
    per_q = tq // tk
    ones_rows = jnp.ones((16, tk), BF16)
    key = lax.broadcasted_iota(jnp.int32, (tk, tk), 0)
    query = lax.broadcasted_iota(jnp.int32, (tk, tk), 1)

    tasks = []
    for i in range(seq // tq):
        tasks += [(i, j, None) for j in range(i * per_q)]
        tasks += [(i, i * per_q + d, d * tk) for d in range(per_q)]

    def units(task):
        return [(h, c) for h in range(2) for c in range(task[2] or 0, tq, tk)]

    def issue_scores(slot, task, u):
        i, j, _ = task
        h, c = units(task)[u]
        s_ref[slot, u] = jnp.dot(
            k_ref[0, j * tk:(j + 1) * tk, h * LANES:(h + 1) * LANES],
            qt_ref[0, h * LANES:(h + 1) * LANES, i * tq + c:i * tq + c + tk],
            preferred_element_type=F32)

    for u in range(len(units(tasks[0]))):
        issue_scores(0, tasks[0], u)
    state = {}
    for n, task in enumerate(tasks):
        i, j, q0 = task
        slot = n % 2
        nxt = tasks[n + 1] if n + 1 < len(tasks) else None
        n_next = len(units(nxt)) if nxt else 0
        if j == 0:
            state = {(h, c): (jnp.full((1, tk), NEG_BIG, F32), jnp.zeros((1, tk), F32),
                              jnp.zeros((V_HEAD_DIM, tk), F32))
                     for h in range(2) for c in range(0, tq, tk)}
        v_aug = [jnp.concatenate([vt_ref[0, h * V_HEAD_DIM:(h + 1) * V_HEAD_DIM,
                                         j * tk:(j + 1) * tk], ones_rows], axis=0)
                 for h in range(2)]
        for u, (h, c) in enumerate(units(task)):
            if u < n_next:
                issue_scores(1 - slot, nxt, u)
            s_u = s_ref[slot, u]
            if c == q0:
                s_u = jnp.where(key <= query, s_u, NEG_BIG)
            m, l, acc = state[h, c]
            m_new = jnp.maximum(m, jnp.max(s_u, axis=0, keepdims=True))
            alpha = jnp.exp2(m - m_new)
            p_u = jnp.exp2(s_u - m_new).astype(BF16)
            pv = jnp.dot(v_aug[h], p_u, preferred_element_type=F32)
            state[h, c] = (m_new, l * alpha + pv[V_HEAD_DIM:V_HEAD_DIM + 1],
                           acc * alpha + pv[:V_HEAD_DIM])
        for u in range(len(units(task)), n_next):
            issue_scores(1 - slot, nxt, u)
        if nxt is None or nxt[0] != i:
            for c in range(0, tq, tk):
                o_t = jnp.concatenate([state[h, c][2] / state[h, c][1] for h in range(2)], axis=0)
                rows = slice(i * tq + c, i * tq + c + tk)
                out_ref[0, rows, :] = (o_t.T * sb_ref[0, rows, :].astype(F32)).astype(out_ref.dtype)


def _attention(q_t, k, v_t, silu_b):
    bsz, seq, _ = k.shape
    assert seq % ATT_TQ == 0 and ATT_TQ % ATT_TK == 0
    return pl.pallas_call(
        _attn_kernel,
        grid=(bsz, B_HEADS // 2),
        in_specs=[
            pl.BlockSpec((1, 2 * LANES, seq), lambda b, p: (b, p, 0)),
            pl.BlockSpec((1, seq, 2 * LANES), lambda b, p: (b, 0, p)),
            pl.BlockSpec((1, 2 * V_HEAD_DIM, seq), lambda b, p: (b, p, 0)),
            pl.BlockSpec((1, seq, LANES), lambda b, p: (b, 0, p)),
        ],
        out_specs=pl.BlockSpec((1, seq, LANES), lambda b, p: (b, 0, p)),
        out_shape=jax.ShapeDtypeStruct((bsz, seq, B_WIDTH), BF16),
        scratch_shapes=[pltpu.VMEM((2, 2 * (ATT_TQ // ATT_TK), ATT_TK, ATT_TK), F32)],
        compiler_params=pltpu.CompilerParams(
            dimension_semantics=("arbitrary", "arbitrary"), vmem_limit_bytes=VMEM_LIMIT),
        name="mla_attn",
    )(q_t, k, v_t, silu_b)


def _out_kernel(ya_ref, yb_ref, gate_ref, x_ref, woa_ref, wob_ref, wo_ref, gp_ref, out_ref):
    y_a = jnp.dot(ya_ref[...], woa_ref[...], preferred_element_type=F32)
    y_b = jnp.dot(yb_ref[...], wob_ref[...], preferred_element_type=F32)
    gates = gate_ref[...].astype(F32)
    merged = gates[:, :D_MODEL] * y_a + gates[:, D_MODEL:] * y_b
    o = _dot(merged, wo_ref[...])
    ms = jnp.mean(o * o, axis=-1, keepdims=True)
    out_ref[...] = x_ref[...] + o * lax.rsqrt(ms + NORM_EPS) * gp_ref[...]


def _out_proj(ya, yb, gates, x2, woa, wob, wo, g_post, tm):
    t = x2.shape[0]
    const = lambda shape: pl.BlockSpec(shape, lambda i: (0,) * len(shape))
    rows = lambda n: pl.BlockSpec((tm, n), lambda i: (i, 0))
    return pl.pallas_call(
        _out_kernel,
        grid=(t // tm,),
        in_specs=[
            rows(A_WIDTH), rows(B_WIDTH), rows(2 * D_MODEL), rows(D_MODEL),
            const((A_WIDTH, D_MODEL)), const((B_WIDTH, D_MODEL)),
            const((D_MODEL, D_MODEL)), const((1, D_MODEL)),
        ],
        out_specs=rows(D_MODEL),
        out_shape=jax.ShapeDtypeStruct((t, D_MODEL), F32),
        compiler_params=pltpu.CompilerParams(
            dimension_semantics=("arbitrary",), vmem_limit_bytes=VMEM_LIMIT),
        name="out_proj",
    )(ya, yb, gates, x2, woa, wob, wo, g_post)


def _rope_partner(w_pe):
    half = QK_ROPE_DIM // 2
    return jnp.concatenate([-w_pe[..., half:], w_pe[..., :half]], axis=-1)


def _prep_weights(w_in, w_decay_up, w_iclr_up, w_uq, w_ukv):
    o = np.cumsum([0, SHIFT_COLS, A_WIDTH, Q_LORA_RANK, KV_LORA_RANK, QK_ROPE_DIM, B_WIDTH])
    w_pe = w_in[:, o[4]:o[5]]
    kpe_blk = jnp.concatenate(
        [jnp.zeros((D_MODEL, QK_NOPE_DIM), F32), w_pe, _rope_partner(w_pe)], axis=-1)
    w_p = jnp.concatenate(
        [w_in[:, :o[4]], kpe_blk, w_in[:, o[5]:]], axis=-1).astype(BF16)
    zeros_lora = jnp.zeros((DECAY_LORA, A_WIDTH), F32)
    wd_p = jnp.concatenate([w_decay_up, zeros_lora], axis=0).astype(BF16)
    wa_p = jnp.concatenate([zeros_lora, w_iclr_up], axis=0).astype(BF16)
    wq = w_uq.reshape(Q_LORA_RANK, B_HEADS, QK_NOPE_DIM + QK_ROPE_DIM)
    wqt_p = jnp.concatenate(
        [wq, _rope_partner(wq[..., QK_NOPE_DIM:])], axis=-1).reshape(Q_LORA_RANK, -1).T.astype(BF16)
    wkv = w_ukv.reshape(KV_LORA_RANK, B_HEADS, QK_NOPE_DIM + V_HEAD_DIM)
    wk_p = jnp.concatenate(
        [wkv[..., :QK_NOPE_DIM], jnp.zeros_like(wkv[..., :QK_NOPE_DIM])],
        axis=-1).reshape(KV_LORA_RANK, -1).astype(BF16)
    wvt_p = wkv[..., QK_NOPE_DIM:].reshape(KV_LORA_RANK, -1).T.astype(BF16)
    return w_p, wd_p, wa_p, wqt_p, wk_p, wvt_p


def kernel(x, positions, g_pre, w_in, b_gate, mu_shift, w0, w_decay_up, a0, w_iclr_up, k_k, k_a,
           r_k, gn_gain, gn_bias, w_out_a, g_q, w_uq, g_kv, w_ukv, w_out_b, w_o, g_post):
    bsz, seq, _ = x.shape
    t = bsz * seq
    tm = TM_ROWS
    row = lambda a: a.reshape(1, -1).astype(F32)
    w_p, wd_p, wa_p, wqt_p, wk_p, wvt_p = _prep_weights(w_in, w_decay_up, w_iclr_up, w_uq, w_ukv)
    x2 = x.reshape(t, D_MODEL)

    feat, silu_a, lat, silu_b, gates = _inproj(x2, row(g_pre), w_p, row(b_gate), row(mu_shift),
                                               seq, TM_INPROJ)

    ya = _rwkv(feat, silu_a, row(w0), row(a0), row(k_k), row(k_a), row(r_k),
               row(gn_gain), row(gn_bias), wd_p, wa_p, bsz, seq, RWKV_TT)

    freq = ROPE_THETA ** (-(np.arange(QK_ROPE_DIM) % (QK_ROPE_DIM // 2)) * 2.0 / QK_ROPE_DIM)
    invf = jnp.asarray(freq.reshape(QK_ROPE_DIM, 1), F32)
    pos3 = positions.reshape(t // tm, 1, tm)
    q_t, k, v_t = _mla_prep(lat, pos3, invf, g_q.reshape(-1, 1).astype(F32), row(g_kv),
                            wqt_p, wk_p, wvt_p, bsz, seq, tm)
    yb = _attention(q_t, k.reshape(bsz, seq, -1), v_t, silu_b.reshape(bsz, seq, -1))

    out = _out_proj(ya, yb.reshape(t, B_WIDTH), gates, x2, w_out_a.astype(BF16),
                    w_out_b.astype(BF16), w_o.astype(BF16), row(g_post), tm)
    return out.reshape(bsz, seq, D_MODEL)
```

```python
import functools
import math

import jax
import jax.numpy as jnp
import numpy as np
from jax import lax
from jax.experimental import pallas as pl
from jax.experimental.pallas import tpu as pltpu

D_MODEL = 1024
A_HEADS = 8
A_HEAD_DIM = 64
A_WIDTH = A_HEADS * A_HEAD_DIM
DECAY_LORA = 64
ICLR_LORA = 64
DECAY_SCALE = 0.6065306597
GN_EPS = 64e-5
B_HEADS = 8
QK_NOPE_DIM = 64
QK_ROPE_DIM = 32
V_HEAD_DIM = 64
Q_LORA_RANK = 256
KV_LORA_RANK = 128
B_WIDTH = B_HEADS * V_HEAD_DIM
ROPE_THETA = 10000.0
NORM_EPS = 1e-6
SHIFT_COLS = 3 * A_WIDTH + DECAY_LORA + ICLR_LORA

LANES = 128
PAIR = 2 * A_HEAD_DIM
N_PAIRS = A_HEADS // 2
CHUNK = 64
LAT_COLS = 512
PROJ_COLS = SHIFT_COLS + A_WIDTH + LAT_COLS + B_WIDTH + 2 * D_MODEL
VMEM_LIMIT = 52 * 1024 * 1024
TM_INPROJ = 256
TM_ROWS = 512
RWKV_TT = 256

F32 = jnp.float32
BF16 = jnp.bfloat16
HI = lax.Precision.HIGHEST


def _dot(a, b):
    return jnp.dot(a.astype(BF16), b.astype(BF16), preferred_element_type=F32)


def _dot_nt(a, b):
    return lax.dot_general(a.astype(BF16), b.astype(BF16), (((1,), (1,)), ((), ())),
                           preferred_element_type=F32)


def _sigmoid(x):
    return 0.5 * jnp.tanh(0.5 * x) + 0.5


RWKV_SECTIONS = 8


def _inproj_kernel(tiles_per_seq, x_ref, g_ref, w_ref, bg_ref, mu_ref, w0_ref, a0_ref, kk_ref, ka_ref,
                   rk_ref, wd_ref, wa_ref,
                   ops_ref, decay_ref, sa_ref, lat_ref, sb_ref, gate_ref, last_ref):
    @pl.when(pl.program_id(0) % tiles_per_seq == 0)
    def _():
        last_ref[...] = jnp.zeros_like(last_ref)

    x = x_ref[...]
    tm = x.shape[0]
    ms = jnp.mean(x * x, axis=-1, keepdims=True)
    u = (x * lax.rsqrt(ms + NORM_EPS) * g_ref[...]).astype(BF16)
    silu = lambda z: z * _sigmoid(z)
    first_row = lax.broadcasted_iota(jnp.int32, (tm, 1), 0) == 0

    def token_shift(feat):
        prev = jnp.where(first_row, last_ref[0:1, :], pltpu.roll(feat, 1, 0))
        last_ref[0:1, :] = feat[tm - 1:tm, :]
        return feat + mu_ref[...] * (prev - feat)

    first_head = lax.broadcasted_iota(jnp.int32, (1, LANES), 1) < A_HEAD_DIM
    pos_in_chunk = lax.broadcasted_iota(jnp.int32, (tm, 1), 0) % CHUNK

    def head_sum(z):
        out = []
        for p in range(N_PAIRS):
            blk = z[:, p * LANES:(p + 1) * LANES]
            s_a = jnp.sum(jnp.where(first_head, blk, 0.0), axis=1, keepdims=True)
            s_b = jnp.sum(jnp.where(first_head, 0.0, blk), axis=1, keepdims=True)
            out.append(jnp.where(first_head, s_a, s_b))
        return jnp.concatenate(out, axis=1)

    tok_r = lax.broadcasted_iota(jnp.int32, (tm, tm), 0)
    tok_c = lax.broadcasted_iota(jnp.int32, (tm, tm), 1)
    prefix = ((tok_c <= tok_r) & (tok_c // CHUNK == tok_r // CHUNK)).astype(BF16)

    def chunk_cumsum(z):
        hi = z.astype(BF16)
        lo = (z - hi.astype(F32)).astype(BF16)
        return (jnp.dot(prefix, hi, preferred_element_type=F32)
                + jnp.dot(prefix, lo, preferred_element_type=F32))

    def rwkv_operands(feat):
        f = token_shift(feat)
        r, k, v = (f[:, j * A_WIDTH:(j + 1) * A_WIDTH] for j in range(3))
        la = f[:, 3 * A_WIDTH:]
        w_logit = w0_ref[...] + jnp.dot(jnp.tanh(la).astype(BF16), wd_ref[...],
                                        preferred_element_type=F32)
        a_logit = a0_ref[...] + jnp.dot(la.astype(BF16), wa_ref[...], preferred_element_type=F32)
        yield
        a = _sigmoid(a_logit)
        lw = -DECAY_SCALE * _sigmoid(w_logit)
        kk = k * kk_ref[...]
        k2 = k * (1.0 + (a - 1.0) * ka_ref[...])
        kk = kk / jnp.maximum(jnp.sqrt(head_sum(kk * kk)), 1e-12)
        yield
        bonus = head_sum(r * k2 * rk_ref[...]) * v
        b = kk * a
        yield
        cum = chunk_cumsum(lw)
        ends = [cum[c + CHUNK - 1:c + CHUNK, :] for c in range(0, tm, CHUNK)]
        last = jnp.concatenate([jnp.broadcast_to(e, (CHUNK, A_WIDTH)) for e in ends], axis=0)
        yield
        p_inv = jnp.exp(-cum)
        to_end = jnp.exp(last - cum)
        sections = (lambda: -kk * jnp.exp(cum - lw), lambda: r * jnp.exp(cum), lambda: b * p_inv,
                    lambda: k2 * p_inv, lambda: b * to_end, lambda: k2 * to_end, lambda: v,
                    lambda: bonus)
        assert len(sections) == RWKV_SECTIONS
        for j, sec in enumerate(sections):
            ops_ref[:, j * A_WIDTH:(j + 1) * A_WIDTH] = sec().astype(ops_ref.dtype)
            if j % 3 == 2:
                yield
        pad = jnp.zeros((decay_ref.shape[0] - len(ends), A_WIDTH), F32)
        decay_ref[...] = jnp.concatenate([jnp.exp(e) for e in ends] + [pad], axis=0)

    def store_silu(ref):
        def store(y, c):
            ref[...] = silu(y).astype(ref.dtype)
        return store

    def store_latents(y, c):
        lat_ref[...] = y

    def store_gates(y, c):
        gate_ref[:, c:c + A_WIDTH] = _sigmoid(y + bg_ref[:, c:c + A_WIDTH]).astype(gate_ref.dtype)

    assert A_WIDTH == LAT_COLS == B_WIDTH
    gate_off = SHIFT_COLS + A_WIDTH + LAT_COLS + B_WIDTH
    jobs = [(SHIFT_COLS, store_silu(sa_ref), 0),
            (SHIFT_COLS + A_WIDTH, store_latents, 0),
            (SHIFT_COLS + A_WIDTH + LAT_COLS, store_silu(sb_ref), 0)]
    jobs += [(gate_off + c, store_gates, c) for c in range(0, 2 * D_MODEL, A_WIDTH)]
    phases = rwkv_operands(jnp.dot(u, w_ref[:, :SHIFT_COLS], preferred_element_type=F32))
    for off, store, c in jobs:
        next(phases, None)
        store(jnp.dot(u, w_ref[:, off:off + A_WIDTH], preferred_element_type=F32), c)
    for _ in phases:
        pass


def _inproj(x2, g_pre, w_p, b_gate, mu, rwkv_rows, wd_p, wa_p, seq, tm):
    t = x2.shape[0]
    assert tm % CHUNK == 0 and tm // CHUNK <= 8
    const = lambda shape: pl.BlockSpec(shape, lambda i: (0,) * len(shape))
    outs = ((A_WIDTH, BF16), (LAT_COLS, F32), (B_WIDTH, BF16), (2 * D_MODEL, BF16))
    return pl.pallas_call(
        functools.partial(_inproj_kernel, seq // tm),
        grid=(t // tm,),
        in_specs=[
            pl.BlockSpec((tm, D_MODEL), lambda i: (i, 0)),
            const((1, D_MODEL)), const((D_MODEL, PROJ_COLS)), const((1, 2 * D_MODEL)),
            const((1, SHIFT_COLS))] + [const((1, A_WIDTH))] * 5 + [const((LANES, A_WIDTH))] * 2,
        out_specs=[pl.BlockSpec((tm, RWKV_SECTIONS * A_WIDTH), lambda i: (i, 0)),
                   pl.BlockSpec((8, A_WIDTH), lambda i: (i, 0))]
        + [pl.BlockSpec((tm, n), lambda i: (i, 0)) for n, _ in outs],
        out_shape=[jax.ShapeDtypeStruct((t, RWKV_SECTIONS * A_WIDTH), BF16),
                   jax.ShapeDtypeStruct((t // tm * 8, A_WIDTH), F32)]
        + [jax.ShapeDtypeStruct((t, n), dt) for n, dt in outs],
        scratch_shapes=[pltpu.VMEM((8, SHIFT_COLS), F32)],
        compiler_params=pltpu.CompilerParams(
            dimension_semantics=("arbitrary",), vmem_limit_bytes=VMEM_LIMIT),
        name="inproj",
    )(x2, g_pre, w_p, b_gate, mu, *rwkv_rows, wd_p, wa_p)


def _split_bf16(x):
    hi = x.astype(BF16)
    return hi, (x - hi.astype(F32)).astype(BF16)


def _rwkv_kernel(ops_ref, decay_ref, sa_ref, gg_ref, gb_ref, out_ref, state_ref):
    tt = ops_ref.shape[0]
    n_chunks = tt // CHUNK

    @pl.when(pl.program_id(1) == 0)
    def _():
        state_ref[...] = jnp.zeros_like(state_ref)

    ri = lax.broadcasted_iota(jnp.int32, (LANES, LANES), 0)
    ci = lax.broadcasted_iota(jnp.int32, (LANES, LANES), 1)
    same_head = (ri // A_HEAD_DIM) == (ci // A_HEAD_DIM)
    strict_bd = same_head & (ci < ri)
    incl_bd = same_head & (ci <= ri)
    eye = (ri == ci).astype(F32)
    ones2 = jnp.concatenate([same_head, same_head], axis=0).astype(BF16)
    zeros_sq = jnp.zeros((LANES, LANES), F32)
    lane = lax.broadcasted_iota(jnp.int32, (1, LANES), 1)
    m_a = (lane < A_HEAD_DIM).astype(BF16)
    m_b = 1.0 - m_a

    def head_sum(x):
        hi, lo = _split_bf16(x)
        return jnp.dot(jnp.concatenate([hi, lo], axis=1), ones2, preferred_element_type=F32)

    def stack(z):
        return jnp.concatenate([z * m_a, z * m_b], axis=0)

    def unstack(z):
        return z[:CHUNK] + z[CHUNK:]

    pairs = range(N_PAIRS)
    items = [(p, ch) for p in pairs for ch in range(n_chunks)]
    each = lambda fn: {it: fn(it) for it in items}
    pcols = lambda p: slice(p * LANES, (p + 1) * LANES)

    def chunk_operands(it):
        p, ch = it
        sec = lambda j: ops_ref[ch * CHUNK:(ch + 1) * CHUNK,
                                j * A_WIDTH + p * LANES:j * A_WIDTH + (p + 1) * LANES]
        ends_t = [stack(sec(j)).astype(F32).T for j in (4, 5)]
        return dict(cts=stack(sec(0)), rt=sec(1), bt=sec(2), kt=sec(3), vs=stack(sec(6)),
                    decay=eye * decay_ref[ch:ch + 1, pcols(p)],
                    ends_t=jnp.concatenate(ends_t, axis=1))

    ops = each(chunk_operands)
    g = each(lambda it: _dot_nt(
        jnp.concatenate([ops[it]["cts"], stack(ops[it]["rt"])], axis=0),
        jnp.concatenate([ops[it]["bt"], ops[it]["bt"], ops[it]["kt"], ops[it]["kt"]], axis=0)))
    nb = each(lambda it: jnp.where(strict_bd, g[it][:PAIR, :PAIR], 0.0))
    nk = each(lambda it: jnp.where(strict_bd, g[it][:PAIR, PAIR:], 0.0))
    m_bk = each(lambda it: jnp.concatenate(
        [jnp.where(incl_bd, g[it][PAIR:, :PAIR], 0.0),
         jnp.where(incl_bd, g[it][PAIR:, PAIR:], 0.0)], axis=1))
    t_inv = each(lambda it: eye + nb[it])
    npow = each(lambda it: _dot(nb[it], nb[it]))
    nkv = each(lambda it: _dot(nk[it], ops[it]["vs"]))
    for _ in range(int(math.log2(CHUNK)) - 2):
        both = each(lambda it: _dot(npow[it], jnp.concatenate([npow[it], t_inv[it]], axis=1)))
        npow = each(lambda it: both[it][:, :PAIR])
        t_inv = each(lambda it: t_inv[it] + both[it][:, PAIR:])
    last_term = each(lambda it: _dot(npow[it], t_inv[it]))
    t_inv = each(lambda it: t_inv[it] + last_term[it])
    w12 = each(lambda it: _dot(t_inv[it], jnp.concatenate([ops[it]["cts"], nkv[it]], axis=1)))
    big = each(lambda it: _dot(
        jnp.concatenate([ops[it]["ends_t"], m_bk[it]], axis=0),
        jnp.concatenate([w12[it], jnp.concatenate([zeros_sq, ops[it]["vs"]], axis=1)], axis=0)))
    gt = each(lambda it: (big[it][:PAIR, :PAIR] + ops[it]["decay"]).astype(BF16))
    q1 = each(lambda it: ops[it]["rt"].astype(F32) + unstack(big[it][PAIR:, :PAIR]))

    h = {p: state_ref[p] for p in pairs}
    h_at = {}
    for ch in range(n_chunks):
        for p in pairs:
            it = (p, ch)
            h_at[it] = h[p]
            h_hi, h_lo = _split_bf16(h[p])
            h[p] = jnp.dot(jnp.concatenate([gt[it], gt[it]], axis=1),
                           jnp.concatenate([h_hi, h_lo], axis=0),
                           preferred_element_type=F32) + big[it][:PAIR, PAIR:]
    for p in pairs:
        state_ref[p] = h[p]
    y_it = each(lambda it: _dot(q1[it], h_at[it]) + unstack(big[it][PAIR:, PAIR:]))

    y = {p: jnp.concatenate([y_it[(p, ch)] for ch in range(n_chunks)], axis=0) for p in pairs}
    mean = {p: head_sum(y[p]) * (1.0 / A_HEAD_DIM) for p in pairs}
    d = {p: y[p] - mean[p] for p in pairs}
    var = {p: head_sum(d[p] * d[p]) * (1.0 / A_HEAD_DIM) for p in pairs}
    for p in pairs:
        bonus = ops_ref[:, 7 * A_WIDTH + p * LANES:7 * A_WIDTH + (p + 1) * LANES].astype(F32)
        y_n = d[p] * lax.rsqrt(var[p] + GN_EPS) * gg_ref[:, pcols(p)] + gb_ref[:, pcols(p)] + bonus
        out_ref[:, pcols(p)] = (y_n * sa_ref[:, pcols(p)].astype(F32)).astype(out_ref.dtype)


def _rwkv(ops, decay, silu_a, gn_gain, gn_bias, bsz, seq, tt):
    nt = seq // tt
    const = lambda shape: pl.BlockSpec(shape, lambda b, t: (0,) * len(shape))
    rows = lambda r, n: pl.BlockSpec((r, n), lambda b, t: (b * nt + t, 0))
    return pl.pallas_call(
        _rwkv_kernel,
        grid=(bsz, nt),
        in_specs=[rows(tt, RWKV_SECTIONS * A_WIDTH), rows(8, A_WIDTH), rows(tt, A_WIDTH),
                  const((1, A_WIDTH)), const((1, A_WIDTH))],
        out_specs=rows(tt, A_WIDTH),
        out_shape=jax.ShapeDtypeStruct((bsz * seq, A_WIDTH), BF16),
        scratch_shapes=[pltpu.VMEM((N_PAIRS, LANES, LANES), F32)],
        compiler_params=pltpu.CompilerParams(
            dimension_semantics=("arbitrary", "arbitrary"), vmem_limit_bytes=VMEM_LIMIT),
        name="rwkv7",
    )(ops, decay, silu_a, gn_gain, gn_bias)


QK_SCALE = math.log2(math.e) / math.sqrt(QK_NOPE_DIM + QK_ROPE_DIM)


def _mla_prep_kernel(lat_ref, pos_ref, invf_ref, gq_ref, gkv_ref, wqt_ref, wk_ref, wvt_ref,
                     qt_ref, k_ref, vt_ref):
    lat = lat_ref[...]
    tm = lat.shape[0]
    pos = pos_ref[0].astype(F32)
    ang = invf_ref[...] * pos
    pad = jnp.zeros((LANES - QK_NOPE_DIM - QK_ROPE_DIM, tm), F32)
    cos_q = jnp.concatenate([jnp.ones((QK_NOPE_DIM, tm), F32), jnp.cos(ang), pad], axis=0)
    sin_t = jnp.concatenate([jnp.zeros((QK_NOPE_DIM, tm), F32), jnp.sin(ang), pad], axis=0)

    cq_t = jnp.concatenate([lat[:, j * LANES:(j + 1) * LANES].T
                            for j in range(Q_LORA_RANK // LANES)], axis=0)
    nq_t = cq_t * lax.rsqrt(jnp.mean(cq_t * cq_t, axis=0, keepdims=True) + NORM_EPS) * gq_ref[...]
    q_raw = _dot(wqt_ref[...], nq_t)
    for h in range(B_HEADS):
        sl = slice(h * LANES, (h + 1) * LANES)
        blk = q_raw[sl]
        rot = blk * cos_q + pltpu.roll(blk, LANES - QK_ROPE_DIM, 0) * sin_t
        qt_ref[0, sl, :] = (rot * QK_SCALE).astype(qt_ref.dtype)

    ckv = lat[:, Q_LORA_RANK:Q_LORA_RANK + KV_LORA_RANK]
    nkv = ckv * lax.rsqrt(jnp.mean(ckv * ckv, axis=-1, keepdims=True) + NORM_EPS) * gkv_ref[...]
    vt_ref[0] = _dot(wvt_ref[...], nkv.T).astype(vt_ref.dtype)
    k_nope = _dot(nkv, wk_ref[...])
    lane = lax.broadcasted_iota(jnp.int32, (1, LANES), 1)
    rope_lane = (lane >= QK_NOPE_DIM) & (lane < QK_NOPE_DIM + QK_ROPE_DIM)
    pe = lat[:, Q_LORA_RANK + KV_LORA_RANK:]
    k_pe = pe * jnp.where(rope_lane, cos_q.T, 0.0) + pltpu.roll(pe, LANES - QK_ROPE_DIM, 1) * sin_t.T
    for h in range(B_HEADS):
        sl = slice(h * LANES, (h + 1) * LANES)
        k_ref[:, sl] = (k_nope[:, sl] + k_pe).astype(k_ref.dtype)


def _mla_prep(lat, pos3, invf, g_q, g_kv, wqt_p, wk_p, wvt_p, bsz, seq, tm):
    t = lat.shape[0]
    nt = seq // tm
    const = lambda shape: pl.BlockSpec(shape, lambda i: (0,) * len(shape))
    return pl.pallas_call(
        _mla_prep_kernel,
        grid=(t // tm,),
        in_specs=[
            pl.BlockSpec((tm, LAT_COLS), lambda i: (i, 0)),
            pl.BlockSpec((1, 1, tm), lambda i: (i, 0, 0)),
            const((QK_ROPE_DIM, 1)), const((Q_LORA_RANK, 1)), const((1, KV_LORA_RANK)),
            const((B_HEADS * LANES, Q_LORA_RANK)), const((KV_LORA_RANK, B_HEADS * LANES)),
            const((B_WIDTH, KV_LORA_RANK)),
        ],
        out_specs=[
            pl.BlockSpec((1, B_HEADS * LANES, tm), lambda i: (i // nt, 0, i % nt)),
            pl.BlockSpec((tm, B_HEADS * LANES), lambda i: (i, 0)),
            pl.BlockSpec((1, B_WIDTH, tm), lambda i: (i // nt, 0, i % nt)),
        ],
        out_shape=[
            jax.ShapeDtypeStruct((bsz, B_HEADS * LANES, seq), BF16),
            jax.ShapeDtypeStruct((t, B_HEADS * LANES), BF16),
            jax.ShapeDtypeStruct((bsz, B_WIDTH, seq), BF16),
        ],
        compiler_params=pltpu.CompilerParams(
            dimension_semantics=("arbitrary",), vmem_limit_bytes=VMEM_LIMIT),
        name="mla_prep",
    )(lat, pos3, invf, g_q, g_kv, wqt_p, wk_p, wvt_p)


NEG_BIG = -1e30
ATT_TK = 512
ATT_TQ = 1024


def _attn_kernel(qt_ref, k_ref, vt_ref, sb_ref, out_ref, s_ref):
    tq = qt_ref.shape[2]
    tk = ATT_TK
    i = pl.program_id(2)
    q_t = (qt_ref[0, :LANES, :], qt_ref[0, LANES:, :])
    ones_rows = jnp.ones((16, tk), BF16)

    def units(q0):
        return [(h, c) for h in range(2) for c in range(q0, tq, tk)]

    def score_unit(j, h, c):
        start = pl.multiple_of(j * tk, tk)
        return jnp.dot(k_ref[0, pl.ds(start, tk), h * LANES:(h + 1) * LANES], q_t[h][:, c:c + tk],
                       preferred_element_type=F32)

    def tile(j, carry, get_s, prefetch, q0=0):
        start = pl.multiple_of(j * tk, tk)
        vb = vt_ref[0, :, pl.ds(start, tk)]
        v_aug = [jnp.concatenate([vb[h * V_HEAD_DIM:(h + 1) * V_HEAD_DIM, :], ones_rows], axis=0)
                 for h in range(2)]
        cols = [[tuple(x[:, :q0] for x in carry[h])] if q0 else [] for h in range(2)]
        for u, (h, c) in enumerate(units(q0)):
            prefetch(u)
            s_u = get_s(u, h, c)
            m, l, acc = [x[:, c:c + tk] for x in carry[h]]
            m_new = jnp.maximum(m, jnp.max(s_u, axis=0, keepdims=True))
            alpha = jnp.exp2(m - m_new)
            p_u = jnp.exp2(s_u - m_new).astype(BF16)
            pv = jnp.dot(v_aug[h], p_u, preferred_element_type=F32)
            cols[h].append((m_new, l * alpha + pv[V_HEAD_DIM:V_HEAD_DIM + 1],
                            acc * alpha + pv[:V_HEAD_DIM]))
        return tuple(tuple(jnp.concatenate(parts, axis=1) if len(parts) > 1 else parts[0]
                           for parts in zip(*cols[h])) for h in range(2))

    init = tuple((jnp.full((1, tq), NEG_BIG, F32), jnp.zeros((1, tq), F32),
                  jnp.zeros((V_HEAD_DIM, tq), F32)) for _ in range(2))
    n_full = i * (tq // tk)
    all_units = units(0)

    def visible_tile(j, slot, carry):
        def prefetch(u):
            h, c = all_units[u]
            s_ref[1 - slot, u] = score_unit(j + 1, h, c)
        return tile(j, carry, lambda u, h, c: s_ref[slot, u], prefetch)

    def two_tiles(jj, carry):
        return visible_tile(2 * jj + 1, 1, visible_tile(2 * jj, 0, carry))

    assert (tq // tk) % 2 == 0
    for u, (h, c) in enumerate(all_units):
        s_ref[0, u] = score_unit(0, h, c)
    carry = lax.fori_loop(0, n_full // 2, two_tiles, init)

    key = lax.broadcasted_iota(jnp.int32, (tk, tk), 0)
    query = lax.broadcasted_iota(jnp.int32, (tk, tk), 1)
    ahead = {}
    for d in range(tq // tk):
        q0 = d * tk
        here, ahead = ahead, {}
        nxt = units(q0 + tk) if d + 1 < tq // tk else []

        def prefetch(u, d=d, nxt=nxt, ahead=ahead):
            if u < len(nxt):
                ahead[nxt[u]] = score_unit(n_full + d + 1, *nxt[u])

        def get_s(u, h, c, d=d, q0=q0, here=here):
            s_u = s_ref[0, u] if d == 0 else here[h, c]
            return jnp.where(key <= query, s_u, NEG_BIG) if c == q0 else s_u

        carry = tile(n_full + d, carry, get_s, prefetch, q0)
    o_t = jnp.concatenate([acc / l for (_, l, acc) in carry], axis=0)
    out_ref[0] = (o_t.T * sb_ref[0].astype(F32)).astype(out_ref.dtype)


def _attention(q_t, k, v_t, zb, tq):
    bsz, seq, _ = k.shape
    return pl.pallas_call(
        _attn_kernel,
        grid=(bsz, B_HEADS // 2, seq // tq),
        in_specs=[
            pl.BlockSpec((1, 2 * LANES, tq), lambda b, p, i: (b, p, i)),
            pl.BlockSpec((1, seq, 2 * LANES), lambda b, p, i: (b, 0, p)),
            pl.BlockSpec((1, 2 * V_HEAD_DIM, seq), lambda b, p, i: (b, p, 0)),
            pl.BlockSpec((1, tq, LANES), lambda b, p, i: (b, i, p)),
        ],
        out_specs=pl.BlockSpec((1, tq, LANES), lambda b, p, i: (b, i, p)),
        out_shape=jax.ShapeDtypeStruct((bsz, seq, B_WIDTH), BF16),
        scratch_shapes=[pltpu.VMEM((2, 2 * (tq // ATT_TK), ATT_TK, ATT_TK), F32)],
        compiler_params=pltpu.CompilerParams(
            dimension_semantics=("arbitrary", "arbitrary", "arbitrary"),
            vmem_limit_bytes=VMEM_LIMIT),
        name="mla_attn",
    )(q_t, k, v_t, zb)


def _out_kernel(ya_ref, yb_ref, gate_ref, x_ref, woa_ref, wob_ref, wo_ref, gp_ref, out_ref):
    y_a = jnp.dot(ya_ref[...], woa_ref[...], preferred_element_type=F32)
    y_b = jnp.dot(yb_ref[...], wob_ref[...], preferred_element_type=F32)
    gates = gate_ref[...].astype(F32)
    merged = gates[:, :D_MODEL] * y_a + gates[:, D_MODEL:] * y_b
    o = _dot(merged, wo_ref[...])
    ms = jnp.mean(o * o, axis=-1, keepdims=True)
    out_ref[...] = x_ref[...] + o * lax.rsqrt(ms + NORM_EPS) * gp_ref[...]


def _out_proj(ya, yb, gates, x2, woa, wob, wo, g_post, tm):
    t = x2.shape[0]
    const = lambda shape: pl.BlockSpec(shape, lambda i: (0,) * len(shape))
    rows = lambda n: pl.BlockSpec((tm, n), lambda i: (i, 0))
    return pl.pallas_call(
        _out_kernel,
        grid=(t // tm,),
        in_specs=[
            rows(A_WIDTH), rows(B_WIDTH), rows(2 * D_MODEL), rows(D_MODEL),
            const((A_WIDTH, D_MODEL)), const((B_WIDTH, D_MODEL)),
            const((D_MODEL, D_MODEL)), const((1, D_MODEL)),
        ],
        out_specs=rows(D_MODEL),
        out_shape=jax.ShapeDtypeStruct((t, D_MODEL), F32),
        compiler_params=pltpu.CompilerParams(
            dimension_semantics=("arbitrary",), vmem_limit_bytes=VMEM_LIMIT),
        name="out_proj",
    )(ya, yb, gates, x2, woa, wob, wo, g_post)


def _rope_partner(w_pe):
    half = QK_ROPE_DIM // 2
    return jnp.concatenate([-w_pe[..., half:], w_pe[..., :half]], axis=-1)


def _prep_weights(w_in, w_decay_up, w_iclr_up, w_uq, w_ukv):
    o = np.cumsum([0, SHIFT_COLS, A_WIDTH, Q_LORA_RANK, KV_LORA_RANK, QK_ROPE_DIM, B_WIDTH])
    w_pe = w_in[:, o[4]:o[5]]
    kpe_blk = jnp.concatenate(
        [jnp.zeros((D_MODEL, QK_NOPE_DIM), F32), w_pe, _rope_partner(w_pe)], axis=-1)
    w_p = jnp.concatenate(
        [w_in[:, :o[4]], kpe_blk, w_in[:, o[5]:]], axis=-1).astype(BF16)
    zeros_lora = jnp.zeros((DECAY_LORA, A_WIDTH), F32)
    wd_p = jnp.concatenate([w_decay_up, zeros_lora], axis=0).astype(BF16)
    wa_p = jnp.concatenate([zeros_lora, w_iclr_up], axis=0).astype(BF16)
    wq = w_uq.reshape(Q_LORA_RANK, B_HEADS, QK_NOPE_DIM + QK_ROPE_DIM)
    wqt_p = jnp.concatenate(
        [wq, _rope_partner(wq[..., QK_NOPE_DIM:])], axis=-1).reshape(Q_LORA_RANK, -1).T.astype(BF16)
    wkv = w_ukv.reshape(KV_LORA_RANK, B_HEADS, QK_NOPE_DIM + V_HEAD_DIM)
    wk_p = jnp.concatenate(
        [wkv[..., :QK_NOPE_DIM], jnp.zeros_like(wkv[..., :QK_NOPE_DIM])],
        axis=-1).reshape(KV_LORA_RANK, -1).astype(BF16)
    wvt_p = wkv[..., QK_NOPE_DIM:].reshape(KV_LORA_RANK, -1).T.astype(BF16)
    return w_p, wd_p, wa_p, wqt_p, wk_p, wvt_p


def kernel(x, positions, g_pre, w_in, b_gate, mu_shift, w0, w_decay_up, a0, w_iclr_up, k_k, k_a,
           r_k, gn_gain, gn_bias, w_out_a, g_q, w_uq, g_kv, w_ukv, w_out_b, w_o, g_post):
    bsz, seq, _ = x.shape
    t = bsz * seq
    tm = TM_ROWS
    row = lambda a: a.reshape(1, -1).astype(F32)
    w_p, wd_p, wa_p, wqt_p, wk_p, wvt_p = _prep_weights(w_in, w_decay_up, w_iclr_up, w_uq, w_ukv)
    x2 = x.reshape(t, D_MODEL)

    assert TM_INPROJ == RWKV_TT
    rwkv_ops, decay, silu_a, lat, silu_b, gates = _inproj(
        x2, row(g_pre), w_p, row(b_gate), row(mu_shift),
        [row(w0), row(a0), row(k_k), row(k_a), row(r_k)], wd_p, wa_p, seq, TM_INPROJ)

    ya = _rwkv(rwkv_ops, decay, silu_a, row(gn_gain), row(gn_bias), bsz, seq, RWKV_TT)

    freq = ROPE_THETA ** (-(np.arange(QK_ROPE_DIM) % (QK_ROPE_DIM // 2)) * 2.0 / QK_ROPE_DIM)
    invf = jnp.asarray(freq.reshape(QK_ROPE_DIM, 1), F32)
    pos3 = positions.reshape(t // tm, 1, tm)
    q_t, k, v_t = _mla_prep(lat, pos3, invf, g_q.reshape(-1, 1).astype(F32), row(g_kv),
                            wqt_p, wk_p, wvt_p, bsz, seq, tm)
    yb = _attention(q_t, k.reshape(bsz, seq, -1), v_t, silu_b.reshape(bsz, seq, -1), ATT_TQ)

    out = _out_proj(ya, yb.reshape(t, B_WIDTH), gates, x2, w_out_a.astype(BF16),
                    w_out_b.astype(BF16), w_o.astype(BF16), row(g_post), tm)
    return out.reshape(bsz, seq, D_MODEL)
```

```python
import functools
import math

import jax
import jax.numpy as jnp
import numpy as np
from jax import lax
from jax.experimental import pallas as pl
from jax.experimental.pallas import tpu as pltpu

D_MODEL = 1024
A_HEADS = 8
A_HEAD_DIM = 64
A_WIDTH = A_HEADS * A_HEAD_DIM
DECAY_LORA = 64
ICLR_LORA = 64
DECAY_SCALE = 0.6065306597
GN_EPS = 64e-5
B_HEADS = 8
QK_NOPE_DIM = 64
QK_ROPE_DIM = 32
V_HEAD_DIM = 64
Q_LORA_RANK = 256
KV_LORA_RANK = 128
B_WIDTH = B_HEADS * V_HEAD_DIM
ROPE_THETA = 10000.0
NORM_EPS = 1e-6
SHIFT_COLS = 3 * A_WIDTH + DECAY_LORA + ICLR_LORA

LANES = 128
PAIR = 2 * A_HEAD_DIM
N_PAIRS = A_HEADS // 2
CHUNK = 64
LAT_COLS = 512
PROJ_COLS = SHIFT_COLS + A_WIDTH + LAT_COLS + B_WIDTH + 2 * D_MODEL
VMEM_LIMIT = 52 * 1024 * 1024
TM_INPROJ = 256
TM_ROWS = 1024
RWKV_TT = 256

F32 = jnp.float32
BF16 = jnp.bfloat16
HI = lax.Precision.HIGHEST


def _dot(a, b):
    return jnp.dot(a.astype(BF16), b.astype(BF16), preferred_element_type=F32)


def _dot_nt(a, b):
    return lax.dot_general(a.astype(BF16), b.astype(BF16), (((1,), (1,)), ((), ())),
                           preferred_element_type=F32)


def _sigmoid(x):
    return 0.5 * jnp.tanh(0.5 * x) + 0.5


RWKV_SECTIONS = 8


def _inproj_kernel(tiles_per_seq, x_ref, g_ref, w_ref, bg_ref, mu_ref, w0_ref, a0_ref, kk_ref, ka_ref,
                   rk_ref, wd_ref, wa_ref,
                   ops_ref, decay_ref, sa_ref, lat_ref, sb_ref, gate_ref, last_ref):
    @pl.when(pl.program_id(0) % tiles_per_seq == 0)
    def _():
        last_ref[...] = jnp.zeros_like(last_ref)

    x = x_ref[...]
    tm = x.shape[0]
    ms = jnp.mean(x * x, axis=-1, keepdims=True)
    u = (x * lax.rsqrt(ms + NORM_EPS) * g_ref[...]).astype(BF16)
    silu = lambda z: z * _sigmoid(z)
    first_row = lax.broadcasted_iota(jnp.int32, (tm, 1), 0) == 0

    def token_shift(feat):
        prev = jnp.where(first_row, last_ref[0:1, :], pltpu.roll(feat, 1, 0))
        last_ref[0:1, :] = feat[tm - 1:tm, :]
        return feat + mu_ref[...] * (prev - feat)

    first_head = lax.broadcasted_iota(jnp.int32, (1, LANES), 1) < A_HEAD_DIM
    pos_in_chunk = lax.broadcasted_iota(jnp.int32, (tm, 1), 0) % CHUNK

    def head_sum(z):
        out = []
        for p in range(N_PAIRS):
            blk = z[:, p * LANES:(p + 1) * LANES]
            s_a = jnp.sum(jnp.where(first_head, blk, 0.0), axis=1, keepdims=True)
            s_b = jnp.sum(jnp.where(first_head, 0.0, blk), axis=1, keepdims=True)
            out.append(jnp.where(first_head, s_a, s_b))
        return jnp.concatenate(out, axis=1)

    tok_r = lax.broadcasted_iota(jnp.int32, (tm, tm), 0)
    tok_c = lax.broadcasted_iota(jnp.int32, (tm, tm), 1)
    prefix = ((tok_c <= tok_r) & (tok_c // CHUNK == tok_r // CHUNK)).astype(BF16)

    def chunk_cumsum(z):
        hi = z.astype(BF16)
        lo = (z - hi.astype(F32)).astype(BF16)
        return (jnp.dot(prefix, hi, preferred_element_type=F32)
                + jnp.dot(prefix, lo, preferred_element_type=F32))

    def rwkv_operands(feat):
        f = token_shift(feat)
        r, k, v = (f[:, j * A_WIDTH:(j + 1) * A_WIDTH] for j in range(3))
        la = f[:, 3 * A_WIDTH:]
        w_logit = w0_ref[...] + jnp.dot(jnp.tanh(la).astype(BF16), wd_ref[...],
                                        preferred_element_type=F32)
        a_logit = a0_ref[...] + jnp.dot(la.astype(BF16), wa_ref[...], preferred_element_type=F32)
        yield
        a = _sigmoid(a_logit)
        lw = -DECAY_SCALE * _sigmoid(w_logit)
        kk = k * kk_ref[...]
        k2 = k * (1.0 + (a - 1.0) * ka_ref[...])
        kk = kk / jnp.maximum(jnp.sqrt(head_sum(kk * kk)), 1e-12)
        yield
        bonus = head_sum(r * k2 * rk_ref[...]) * v
        b = kk * a
        yield
        cum = chunk_cumsum(lw)
        ends = [cum[c + CHUNK - 1:c + CHUNK, :] for c in range(0, tm, CHUNK)]
        last = jnp.concatenate([jnp.broadcast_to(e, (CHUNK, A_WIDTH)) for e in ends], axis=0)
        yield
        p_inv = jnp.exp(-cum)
        to_end = jnp.exp(last - cum)
        sections = (lambda: -kk * jnp.exp(cum - lw), lambda: r * jnp.exp(cum), lambda: b * p_inv,
                    lambda: k2 * p_inv, lambda: b * to_end, lambda: k2 * to_end, lambda: v,
                    lambda: bonus)
        assert len(sections) == RWKV_SECTIONS
        for j, sec in enumerate(sections):
            ops_ref[:, j * A_WIDTH:(j + 1) * A_WIDTH] = sec().astype(ops_ref.dtype)
            if j % 3 == 2:
                yield
        pad = jnp.zeros((decay_ref.shape[0] - len(ends), A_WIDTH), F32)
        decay_ref[...] = jnp.concatenate([jnp.exp(e) for e in ends] + [pad], axis=0)

    def store_silu(ref):
        def store(y, c):
            ref[...] = silu(y).astype(ref.dtype)
        return store

    def store_latents(y, c):
        lat_ref[...] = y

    def store_gates(y, c):
        gate_ref[:, c:c + A_WIDTH] = _sigmoid(y + bg_ref[:, c:c + A_WIDTH]).astype(gate_ref.dtype)

    assert A_WIDTH == LAT_COLS == B_WIDTH
    gate_off = SHIFT_COLS + A_WIDTH + LAT_COLS + B_WIDTH
    jobs = [(SHIFT_COLS, store_silu(sa_ref), 0),
            (SHIFT_COLS + A_WIDTH, store_latents, 0),
            (SHIFT_COLS + A_WIDTH + LAT_COLS, store_silu(sb_ref), 0)]
    jobs += [(gate_off + c, store_gates, c) for c in range(0, 2 * D_MODEL, A_WIDTH)]
    phases = rwkv_operands(jnp.dot(u, w_ref[:, :SHIFT_COLS], preferred_element_type=F32))
    for off, store, c in jobs:
        next(phases, None)
        store(jnp.dot(u, w_ref[:, off:off + A_WIDTH], preferred_element_type=F32), c)
    for _ in phases:
        pass


def _inproj(x2, g_pre, w_p, b_gate, mu, rwkv_rows, wd_p, wa_p, seq, tm):
    t = x2.shape[0]
    assert tm % CHUNK == 0 and tm // CHUNK <= 8
    const = lambda shape: pl.BlockSpec(shape, lambda i: (0,) * len(shape))
    outs = ((A_WIDTH, BF16), (LAT_COLS, F32), (B_WIDTH, BF16), (2 * D_MODEL, BF16))
    return pl.pallas_call(
        functools.partial(_inproj_kernel, seq // tm),
        grid=(t // tm,),
        in_specs=[
            pl.BlockSpec((tm, D_MODEL), lambda i: (i, 0)),
            const((1, D_MODEL)), const((D_MODEL, PROJ_COLS)), const((1, 2 * D_MODEL)),
            const((1, SHIFT_COLS))] + [const((1, A_WIDTH))] * 5 + [const((LANES, A_WIDTH))] * 2,
        out_specs=[pl.BlockSpec((tm, RWKV_SECTIONS * A_WIDTH), lambda i: (i, 0)),
                   pl.BlockSpec((8, A_WIDTH), lambda i: (i, 0))]
        + [pl.BlockSpec((tm, n), lambda i: (i, 0)) for n, _ in outs],
        out_shape=[jax.ShapeDtypeStruct((t, RWKV_SECTIONS * A_WIDTH), BF16),
                   jax.ShapeDtypeStruct((t // tm * 8, A_WIDTH), F32)]
        + [jax.ShapeDtypeStruct((t, n), dt) for n, dt in outs],
        scratch_shapes=[pltpu.VMEM((8, SHIFT_COLS), F32)],
        compiler_params=pltpu.CompilerParams(
            dimension_semantics=("arbitrary",), vmem_limit_bytes=VMEM_LIMIT),
        name="inproj",
    )(x2, g_pre, w_p, b_gate, mu, *rwkv_rows, wd_p, wa_p)


def _split_bf16(x):
    hi = x.astype(BF16)
    return hi, (x - hi.astype(F32)).astype(BF16)


def _rwkv_kernel(ops_ref, decay_ref, sa_ref, gg_ref, gb_ref, out_ref, state_ref):
    tt = ops_ref.shape[0]
    n_chunks = tt // CHUNK

    @pl.when(pl.program_id(1) == 0)
    def _():
        state_ref[...] = jnp.zeros_like(state_ref)

    ri = lax.broadcasted_iota(jnp.int32, (LANES, LANES), 0)
    ci = lax.broadcasted_iota(jnp.int32, (LANES, LANES), 1)
    same_head = (ri // A_HEAD_DIM) == (ci // A_HEAD_DIM)
    strict_bd = same_head & (ci < ri)
    incl_bd = same_head & (ci <= ri)
    eye = (ri == ci).astype(F32)
    ones2 = jnp.concatenate([same_head, same_head], axis=0).astype(BF16)
    zeros_sq = jnp.zeros((LANES, LANES), F32)
    lane = lax.broadcasted_iota(jnp.int32, (1, LANES), 1)
    m_a = (lane < A_HEAD_DIM).astype(BF16)
    m_b = 1.0 - m_a

    def head_sum(x):
        hi, lo = _split_bf16(x)
        return jnp.dot(jnp.concatenate([hi, lo], axis=1), ones2, preferred_element_type=F32)

    def stack(z):
        return jnp.concatenate([z * m_a, z * m_b], axis=0)

    def unstack(z):
        return z[:CHUNK] + z[CHUNK:]

    pairs = range(N_PAIRS)
    items = [(p, ch) for p in pairs for ch in range(n_chunks)]
    each = lambda fn: {it: fn(it) for it in items}
    pcols = lambda p: slice(p * LANES, (p + 1) * LANES)

    def chunk_operands(it):
        p, ch = it
        sec = lambda j: ops_ref[ch * CHUNK:(ch + 1) * CHUNK,
                                j * A_WIDTH + p * LANES:j * A_WIDTH + (p + 1) * LANES]
        ends_t = [stack(sec(j)).astype(F32).T for j in (4, 5)]
        return dict(cts=stack(sec(0)), rt=sec(1), bt=sec(2), kt=sec(3), vs=stack(sec(6)),
                    decay=eye * decay_ref[ch:ch + 1, pcols(p)],
                    ends_t=jnp.concatenate(ends_t, axis=1))

    ops = each(chunk_operands)
    g = each(lambda it: _dot_nt(
        jnp.concatenate([ops[it]["cts"], stack(ops[it]["rt"])], axis=0),
        jnp.concatenate([ops[it]["bt"], ops[it]["bt"], ops[it]["kt"], ops[it]["kt"]], axis=0)))
    nb = each(lambda it: jnp.where(strict_bd, g[it][:PAIR, :PAIR], 0.0))
    nk = each(lambda it: jnp.where(strict_bd, g[it][:PAIR, PAIR:], 0.0))
    m_bk = each(lambda it: jnp.concatenate(
        [jnp.where(incl_bd, g[it][PAIR:, :PAIR], 0.0),
         jnp.where(incl_bd, g[it][PAIR:, PAIR:], 0.0)], axis=1))
    t_inv = each(lambda it: eye + nb[it])
    npow = each(lambda it: _dot(nb[it], nb[it]))
    nkv = each(lambda it: _dot(nk[it], ops[it]["vs"]))
    for _ in range(int(math.log2(CHUNK)) - 2):
        both = each(lambda it: _dot(npow[it], jnp.concatenate([npow[it], t_inv[it]], axis=1)))
        npow = each(lambda it: both[it][:, :PAIR])
        t_inv = each(lambda it: t_inv[it] + both[it][:, PAIR:])
    last_term = each(lambda it: _dot(npow[it], t_inv[it]))
    t_inv = each(lambda it: t_inv[it] + last_term[it])
    w12 = each(lambda it: _dot(t_inv[it], jnp.concatenate([ops[it]["cts"], nkv[it]], axis=1)))
    big = each(lambda it: _dot(
        jnp.concatenate([ops[it]["ends_t"], m_bk[it]], axis=0),
        jnp.concatenate([w12[it], jnp.concatenate([zeros_sq, ops[it]["vs"]], axis=1)], axis=0)))
    gt = each(lambda it: (big[it][:PAIR, :PAIR] + ops[it]["decay"]).astype(BF16))
    q1 = each(lambda it: ops[it]["rt"].astype(F32) + unstack(big[it][PAIR:, :PAIR]))

    h = {p: state_ref[p] for p in pairs}
    h_at = {}
    for ch in range(n_chunks):
        for p in pairs:
            it = (p, ch)
            h_at[it] = h[p]
            h_hi, h_lo = _split_bf16(h[p])
            h[p] = jnp.dot(jnp.concatenate([gt[it], gt[it]], axis=1),
                           jnp.concatenate([h_hi, h_lo], axis=0),
                           preferred_element_type=F32) + big[it][:PAIR, PAIR:]
    for p in pairs:
        state_ref[p] = h[p]
    y_it = each(lambda it: _dot(q1[it], h_at[it]) + unstack(big[it][PAIR:, PAIR:]))

    y = {p: jnp.concatenate([y_it[(p, ch)] for ch in range(n_chunks)], axis=0) for p in pairs}
    mean = {p: head_sum(y[p]) * (1.0 / A_HEAD_DIM) for p in pairs}
    d = {p: y[p] - mean[p] for p in pairs}
    var = {p: head_sum(d[p] * d[p]) * (1.0 / A_HEAD_DIM) for p in pairs}
    for p in pairs:
        bonus = ops_ref[:, 7 * A_WIDTH + p * LANES:7 * A_WIDTH + (p + 1) * LANES].astype(F32)
        y_n = d[p] * lax.rsqrt(var[p] + GN_EPS) * gg_ref[:, pcols(p)] + gb_ref[:, pcols(p)] + bonus
        out_ref[:, pcols(p)] = (y_n * sa_ref[:, pcols(p)].astype(F32)).astype(out_ref.dtype)


def _rwkv(ops, decay, silu_a, gn_gain, gn_bias, bsz, seq, tt):
    nt = seq // tt
    const = lambda shape: pl.BlockSpec(shape, lambda b, t: (0,) * len(shape))
    rows = lambda r, n: pl.BlockSpec((r, n), lambda b, t: (b * nt + t, 0))
    return pl.pallas_call(
        _rwkv_kernel,
        grid=(bsz, nt),
        in_specs=[rows(tt, RWKV_SECTIONS * A_WIDTH), rows(8, A_WIDTH), rows(tt, A_WIDTH),
                  const((1, A_WIDTH)), const((1, A_WIDTH))],
        out_specs=rows(tt, A_WIDTH),
        out_shape=jax.ShapeDtypeStruct((bsz * seq, A_WIDTH), BF16),
        scratch_shapes=[pltpu.VMEM((N_PAIRS, LANES, LANES), F32)],
        compiler_params=pltpu.CompilerParams(
            dimension_semantics=("arbitrary", "arbitrary"), vmem_limit_bytes=VMEM_LIMIT),
        name="rwkv7",
    )(ops, decay, silu_a, gn_gain, gn_bias)


QK_SCALE = math.log2(math.e) / math.sqrt(QK_NOPE_DIM + QK_ROPE_DIM)


def _mla_prep_kernel(lat_ref, pos_ref, invf_ref, gq_ref, gkv_ref, wqt_ref, wk_ref, wvt_ref,
                     qt_ref, k_ref, vt_ref):
    lat = lat_ref[...]
    tm = lat.shape[0]
    pos = pos_ref[0].astype(F32)
    ang = invf_ref[...] * pos
    pad = jnp.zeros((LANES - QK_NOPE_DIM - QK_ROPE_DIM, tm), F32)
    cos_q = jnp.concatenate([jnp.ones((QK_NOPE_DIM, tm), F32), jnp.cos(ang), pad], axis=0)
    sin_t = jnp.concatenate([jnp.zeros((QK_NOPE_DIM, tm), F32), jnp.sin(ang), pad], axis=0)

    cq_t = jnp.concatenate([lat[:, j * LANES:(j + 1) * LANES].T
                            for j in range(Q_LORA_RANK // LANES)], axis=0)
    nq_t = cq_t * lax.rsqrt(jnp.mean(cq_t * cq_t, axis=0, keepdims=True) + NORM_EPS) * gq_ref[...]
    q_raw = _dot(wqt_ref[...], nq_t)
    for h in range(B_HEADS):
        sl = slice(h * LANES, (h + 1) * LANES)
        blk = q_raw[sl]
        rot = blk * cos_q + pltpu.roll(blk, LANES - QK_ROPE_DIM, 0) * sin_t
        qt_ref[0, sl, :] = (rot * QK_SCALE).astype(qt_ref.dtype)

    ckv = lat[:, Q_LORA_RANK:Q_LORA_RANK + KV_LORA_RANK]
    nkv = ckv * lax.rsqrt(jnp.mean(ckv * ckv, axis=-1, keepdims=True) + NORM_EPS) * gkv_ref[...]
    vt_ref[0] = _dot(wvt_ref[...], nkv.T).astype(vt_ref.dtype)
    k_nope = _dot(nkv, wk_ref[...])
    lane = lax.broadcasted_iota(jnp.int32, (1, LANES), 1)
    rope_lane = (lane >= QK_NOPE_DIM) & (lane < QK_NOPE_DIM + QK_ROPE_DIM)
    pe = lat[:, Q_LORA_RANK + KV_LORA_RANK:]
    k_pe = pe * jnp.where(rope_lane, cos_q.T, 0.0) + pltpu.roll(pe, LANES - QK_ROPE_DIM, 1) * sin_t.T
    for h in range(B_HEADS):
        sl = slice(h * LANES, (h + 1) * LANES)
        k_ref[:, sl] = (k_nope[:, sl] + k_pe).astype(k_ref.dtype)


def _mla_prep(lat, pos3, invf, g_q, g_kv, wqt_p, wk_p, wvt_p, bsz, seq, tm):
    t = lat.shape[0]
    nt = seq // tm
    const = lambda shape: pl.BlockSpec(shape, lambda i: (0,) * len(shape))
    return pl.pallas_call(
        _mla_prep_kernel,
        grid=(t // tm,),
        in_specs=[
            pl.BlockSpec((tm, LAT_COLS), lambda i: (i, 0)),
            pl.BlockSpec((1, 1, tm), lambda i: (i, 0, 0)),
            const((QK_ROPE_DIM, 1)), const((Q_LORA_RANK, 1)), const((1, KV_LORA_RANK)),
            const((B_HEADS * LANES, Q_LORA_RANK)), const((KV_LORA_RANK, B_HEADS * LANES)),
            const((B_WIDTH, KV_LORA_RANK)),
        ],
        out_specs=[
            pl.BlockSpec((1, B_HEADS * LANES, tm), lambda i: (i // nt, 0, i % nt)),
            pl.BlockSpec((tm, B_HEADS * LANES), lambda i: (i, 0)),
            pl.BlockSpec((1, B_WIDTH, tm), lambda i: (i // nt, 0, i % nt)),
        ],
        out_shape=[
            jax.ShapeDtypeStruct((bsz, B_HEADS * LANES, seq), BF16),
            jax.ShapeDtypeStruct((t, B_HEADS * LANES), BF16),
            jax.ShapeDtypeStruct((bsz, B_WIDTH, seq), BF16),
        ],
        compiler_params=pltpu.CompilerParams(
            dimension_semantics=("arbitrary",), vmem_limit_bytes=VMEM_LIMIT),
        name="mla_prep",
    )(lat, pos3, invf, g_q, g_kv, wqt_p, wk_p, wvt_p)


NEG_BIG = -1e30
ATT_TK = 512
ATT_TQ = 1024
ATT_CB = 256


def _attn_kernel(qt_ref, k_ref, vt_ref, sb_ref, out_ref, s_ref):
    tq = qt_ref.shape[2]
    tk, cb = ATT_TK, ATT_CB
    i = pl.program_id(2)
    q_t = (qt_ref[0, :LANES, :], qt_ref[0, LANES:, :])
    ones_rows = jnp.ones((16, tk), BF16)

    def units(q0):
        return [(h, c) for h in range(2) for c in range(q0, tq, cb)]

    def score_unit(j, h, c):
        start = pl.multiple_of(j * tk, tk)
        return jnp.dot(k_ref[0, pl.ds(start, tk), h * LANES:(h + 1) * LANES], q_t[h][:, c:c + cb],
                       preferred_element_type=F32)

    def tile(j, carry, get_s, prefetch, q0=0):
        start = pl.multiple_of(j * tk, tk)
        vb = vt_ref[0, :, pl.ds(start, tk)]
        v_aug = [jnp.concatenate([vb[h * V_HEAD_DIM:(h + 1) * V_HEAD_DIM, :], ones_rows], axis=0)
                 for h in range(2)]
        cols = [[tuple(x[:, :q0] for x in carry[h])] if q0 else [] for h in range(2)]
        for u, (h, c) in enumerate(units(q0)):
            prefetch(u)
            s_u = get_s(u, h, c)
            m, l, acc = [x[:, c:c + cb] for x in carry[h]]
            m_new = jnp.maximum(m, jnp.max(s_u, axis=0, keepdims=True))
            alpha = jnp.exp2(m - m_new)
            p_u = jnp.exp2(s_u - m_new).astype(BF16)
            pv = jnp.dot(v_aug[h], p_u, preferred_element_type=F32)
            cols[h].append((m_new, l * alpha + pv[V_HEAD_DIM:V_HEAD_DIM + 1],
                            acc * alpha + pv[:V_HEAD_DIM]))
        return tuple(tuple(jnp.concatenate(parts, axis=1) if len(parts) > 1 else parts[0]
                           for parts in zip(*cols[h])) for h in range(2))

    init = tuple((jnp.full((1, tq), NEG_BIG, F32), jnp.zeros((1, tq), F32),
                  jnp.zeros((V_HEAD_DIM, tq), F32)) for _ in range(2))
    n_full = i * (tq // tk)
    all_units = units(0)

    def visible_tile(j, slot, carry):
        def prefetch(u):
            h, c = all_units[u]
            s_ref[1 - slot, u] = score_unit(j + 1, h, c)
        return tile(j, carry, lambda u, h, c: s_ref[slot, u], prefetch)

    def two_tiles(jj, carry):
        return visible_tile(2 * jj + 1, 1, visible_tile(2 * jj, 0, carry))

    assert (tq // tk) % 2 == 0
    for u, (h, c) in enumerate(all_units):
        s_ref[0, u] = score_unit(0, h, c)
    carry = lax.fori_loop(0, n_full // 2, two_tiles, init)

    key = lax.broadcasted_iota(jnp.int32, (tk, cb), 0)
    query = lax.broadcasted_iota(jnp.int32, (tk, cb), 1)
    ahead = {}
    for d in range(tq // tk):
        q0 = d * tk
        here, ahead = ahead, {}
        nxt = units(q0 + tk) if d + 1 < tq // tk else []

        def prefetch(u, d=d, nxt=nxt, ahead=ahead):
            if u < len(nxt):
                ahead[nxt[u]] = score_unit(n_full + d + 1, *nxt[u])

        def get_s(u, h, c, d=d, q0=q0, here=here):
            s_u = s_ref[0, u] if d == 0 else here[h, c]
            on_diagonal = q0 <= c < q0 + tk
            return jnp.where(key <= query + (c - q0), s_u, NEG_BIG) if on_diagonal else s_u

        carry = tile(n_full + d, carry, get_s, prefetch, q0)
    o_t = jnp.concatenate([acc / l for (_, l, acc) in carry], axis=0)
    out_ref[0] = (o_t.T * sb_ref[0].astype(F32)).astype(out_ref.dtype)


def _attention(q_t, k, v_t, zb, tq):
    bsz, seq, _ = k.shape
    return pl.pallas_call(
        _attn_kernel,
        grid=(bsz, B_HEADS // 2, seq // tq),
        in_specs=[
            pl.BlockSpec((1, 2 * LANES, tq), lambda b, p, i: (b, p, i)),
            pl.BlockSpec((1, seq, 2 * LANES), lambda b, p, i: (b, 0, p)),
            pl.BlockSpec((1, 2 * V_HEAD_DIM, seq), lambda b, p, i: (b, p, 0)),
            pl.BlockSpec((1, tq, LANES), lambda b, p, i: (b, i, p)),
        ],
        out_specs=pl.BlockSpec((1, tq, LANES), lambda b, p, i: (b, i, p)),
        out_shape=jax.ShapeDtypeStruct((bsz, seq, B_WIDTH), BF16),
        scratch_shapes=[pltpu.VMEM((2, 2 * (tq // ATT_CB), ATT_TK, ATT_CB), F32)],
        compiler_params=pltpu.CompilerParams(
            dimension_semantics=("arbitrary", "arbitrary", "arbitrary"),
            vmem_limit_bytes=VMEM_LIMIT),
        name="mla_attn",
    )(q_t, k, v_t, zb)


def _out_kernel(ya_ref, yb_ref, gate_ref, x_ref, woa_ref, wob_ref, wo_ref, gp_ref, out_ref):
    y_a = jnp.dot(ya_ref[...], woa_ref[...], preferred_element_type=F32)
    y_b = jnp.dot(yb_ref[...], wob_ref[...], preferred_element_type=F32)
    gates = gate_ref[...].astype(F32)
    merged = gates[:, :D_MODEL] * y_a + gates[:, D_MODEL:] * y_b
    o = _dot(merged, wo_ref[...])
    ms = jnp.mean(o * o, axis=-1, keepdims=True)
    out_ref[...] = x_ref[...] + o * lax.rsqrt(ms + NORM_EPS) * gp_ref[...]


def _out_proj(ya, yb, gates, x2, woa, wob, wo, g_post, tm):
    t = x2.shape[0]
    const = lambda shape: pl.BlockSpec(shape, lambda i: (0,) * len(shape))
    rows = lambda n: pl.BlockSpec((tm, n), lambda i: (i, 0))
    return pl.pallas_call(
        _out_kernel,
        grid=(t // tm,),
        in_specs=[
            rows(A_WIDTH), rows(B_WIDTH), rows(2 * D_MODEL), rows(D_MODEL),
            const((A_WIDTH, D_MODEL)), const((B_WIDTH, D_MODEL)),
            const((D_MODEL, D_MODEL)), const((1, D_MODEL)),
        ],
        out_specs=rows(D_MODEL),
        out_shape=jax.ShapeDtypeStruct((t, D_MODEL), F32),
        compiler_params=pltpu.CompilerParams(
            dimension_semantics=("arbitrary",), vmem_limit_bytes=VMEM_LIMIT),
        name="out_proj",
    )(ya, yb, gates, x2, woa, wob, wo, g_post)


def _rope_partner(w_pe):
    half = QK_ROPE_DIM // 2
    return jnp.concatenate([-w_pe[..., half:], w_pe[..., :half]], axis=-1)


def _prep_weights(w_in, w_decay_up, w_iclr_up, w_uq, w_ukv):
    o = np.cumsum([0, SHIFT_COLS, A_WIDTH, Q_LORA_RANK, KV_LORA_RANK, QK_ROPE_DIM, B_WIDTH])
    w_pe = w_in[:, o[4]:o[5]]
    kpe_blk = jnp.concatenate(
        [jnp.zeros((D_MODEL, QK_NOPE_DIM), F32), w_pe, _rope_partner(w_pe)], axis=-1)
    w_p = jnp.concatenate(
        [w_in[:, :o[4]].astype(BF16), kpe_blk.astype(BF16), w_in[:, o[5]:].astype(BF16)], axis=-1)
    zeros_lora = jnp.zeros((DECAY_LORA, A_WIDTH), F32)
    wd_p = jnp.concatenate([w_decay_up, zeros_lora], axis=0).astype(BF16)
    wa_p = jnp.concatenate([zeros_lora, w_iclr_up], axis=0).astype(BF16)
    wq = w_uq.reshape(Q_LORA_RANK, B_HEADS, QK_NOPE_DIM + QK_ROPE_DIM)
    wqt_p = jnp.concatenate(
        [wq, _rope_partner(wq[..., QK_NOPE_DIM:])], axis=-1).reshape(Q_LORA_RANK, -1).T.astype(BF16)
    wkv = w_ukv.reshape(KV_LORA_RANK, B_HEADS, QK_NOPE_DIM + V_HEAD_DIM)
    wk_p = jnp.concatenate(
        [wkv[..., :QK_NOPE_DIM], jnp.zeros_like(wkv[..., :QK_NOPE_DIM])],
        axis=-1).reshape(KV_LORA_RANK, -1).astype(BF16)
    wvt_p = wkv[..., QK_NOPE_DIM:].reshape(KV_LORA_RANK, -1).T.astype(BF16)
    return w_p, wd_p, wa_p, wqt_p, wk_p, wvt_p


def kernel(x, positions, g_pre, w_in, b_gate, mu_shift, w0, w_decay_up, a0, w_iclr_up, k_k, k_a,
           r_k, gn_gain, gn_bias, w_out_a, g_q, w_uq, g_kv, w_ukv, w_out_b, w_o, g_post):
    bsz, seq, _ = x.shape
    t = bsz * seq
    tm = TM_ROWS
    row = lambda a: a.reshape(1, -1).astype(F32)
    w_p, wd_p, wa_p, wqt_p, wk_p, wvt_p = _prep_weights(w_in, w_decay_up, w_iclr_up, w_uq, w_ukv)
    x2 = x.reshape(t, D_MODEL)

    assert TM_INPROJ == RWKV_TT
    rwkv_ops, decay, silu_a, lat, silu_b, gates = _inproj(
        x2, row(g_pre), w_p, row(b_gate), row(mu_shift),
        [row(w0), row(a0), row(k_k), row(k_a), row(r_k)], wd_p, wa_p, seq, TM_INPROJ)

    ya = _rwkv(rwkv_ops, decay, silu_a, row(gn_gain), row(gn_bias), bsz, seq, RWKV_TT)

    freq = ROPE_THETA ** (-(np.arange(QK_ROPE_DIM) % (QK_ROPE_DIM // 2)) * 2.0 / QK_ROPE_DIM)
    invf = jnp.asarray(freq.reshape(QK_ROPE_DIM, 1), F32)
    pos3 = positions.reshape(t // tm, 1, tm)
    q_t, k, v_t = _mla_prep(lat, pos3, invf, g_q.reshape(-1, 1).astype(F32), row(g_kv),
                            wqt_p, wk_p, wvt_p, bsz, seq, tm)
    yb = _attention(q_t, k.reshape(bsz, seq, -1), v_t, silu_b.reshape(bsz, seq, -1), ATT_TQ)

    out = _out_proj(ya, yb.reshape(t, B_WIDTH), gates, x2, w_out_a.astype(BF16),
                    w_out_b.astype(BF16), w_o.astype(BF16), row(g_post), tm)
    return out.reshape(bsz, seq, D_MODEL)
```

```python
import functools
import math

import jax
import jax.numpy as jnp
import numpy as np
from jax import lax
from jax.experimental import pallas as pl
from jax.experimental.pallas import tpu as pltpu

D_MODEL = 1024
A_HEADS = 8
A_HEAD_DIM = 64
A_WIDTH = A_HEADS * A_HEAD_DIM
DECAY_LORA = 64
ICLR_LORA = 64
DECAY_SCALE = 0.6065306597
GN_EPS = 64e-5
B_HEADS = 8
QK_NOPE_DIM = 64
QK_ROPE_DIM = 32
V_HEAD_DIM = 64
Q_LORA_RANK = 256
KV_LORA_RANK = 128
B_WIDTH = B_HEADS * V_HEAD_DIM
ROPE_THETA = 10000.0
NORM_EPS = 1e-6
SHIFT_COLS = 3 * A_WIDTH + DECAY_LORA + ICLR_LORA

LANES = 128
PAIR = 2 * A_HEAD_DIM
N_PAIRS = A_HEADS // 2
CHUNK = 64
LAT_COLS = 512
PROJ_COLS = SHIFT_COLS + A_WIDTH + LAT_COLS + B_WIDTH + 2 * D_MODEL
LO_COLS = SHIFT_COLS + A_WIDTH + LAT_COLS
VMEM_LIMIT = 52 * 1024 * 1024
TM_INPROJ = 256
TM_ROWS = 1024
RWKV_TT = 256

F32 = jnp.float32
BF16 = jnp.bfloat16
HI = lax.Precision.HIGHEST


def _dot(a, b):
    return jnp.dot(a.astype(BF16), b.astype(BF16), preferred_element_type=F32)


def _dot_nt(a, b):
    return lax.dot_general(a.astype(BF16), b.astype(BF16), (((1,), (1,)), ((), ())),
                           preferred_element_type=F32)


def _sigmoid(x):
    return 0.5 * jnp.tanh(0.5 * x) + 0.5


RWKV_SECTIONS = 8


def _inproj_kernel(tiles_per_seq, x_ref, g_ref, wlo_ref, whi_ref, bg_ref, mu_ref, w0_ref, a0_ref, kk_ref, ka_ref,
                   rk_ref, wd_ref, wa_ref,
                   ops_ref, decay_ref, sa_ref, lat_ref, sb_ref, gate_ref, last_ref):
    @pl.when(pl.program_id(0) % tiles_per_seq == 0)
    def _():
        last_ref[...] = jnp.zeros_like(last_ref)

    x = x_ref[...]
    tm = x.shape[0]
    ms = jnp.mean(x * x, axis=-1, keepdims=True)
    u = (x * lax.rsqrt(ms + NORM_EPS) * g_ref[...]).astype(BF16)
    silu = lambda z: z * _sigmoid(z)
    first_row = lax.broadcasted_iota(jnp.int32, (tm, 1), 0) == 0

    def token_shift(feat):
        prev = jnp.where(first_row, last_ref[0:1, :], pltpu.roll(feat, 1, 0))
        last_ref[0:1, :] = feat[tm - 1:tm, :]
        return feat + mu_ref[...] * (prev - feat)

    first_head = lax.broadcasted_iota(jnp.int32, (1, LANES), 1) < A_HEAD_DIM
    pos_in_chunk = lax.broadcasted_iota(jnp.int32, (tm, 1), 0) % CHUNK

    def head_sum(z):
        out = []
        for p in range(N_PAIRS):
            blk = z[:, p * LANES:(p + 1) * LANES]
            s_a = jnp.sum(jnp.where(first_head, blk, 0.0), axis=1, keepdims=True)
            s_b = jnp.sum(jnp.where(first_head, 0.0, blk), axis=1, keepdims=True)
            out.append(jnp.where(first_head, s_a, s_b))
        return jnp.concatenate(out, axis=1)

    tok_r = lax.broadcasted_iota(jnp.int32, (tm, tm), 0)
    tok_c = lax.broadcasted_iota(jnp.int32, (tm, tm), 1)
    prefix = ((tok_c <= tok_r) & (tok_c // CHUNK == tok_r // CHUNK)).astype(BF16)

    def chunk_cumsum(z):
        hi = z.astype(BF16)
        lo = (z - hi.astype(F32)).astype(BF16)
        return (jnp.dot(prefix, hi, preferred_element_type=F32)
                + jnp.dot(prefix, lo, preferred_element_type=F32))

    def rwkv_operands(feat):
        f = token_shift(feat)
        r, k, v = (f[:, j * A_WIDTH:(j + 1) * A_WIDTH] for j in range(3))
        la = f[:, 3 * A_WIDTH:]
        w_logit = w0_ref[...] + jnp.dot(jnp.tanh(la).astype(BF16), wd_ref[...],
                                        preferred_element_type=F32)
        a_logit = a0_ref[...] + jnp.dot(la.astype(BF16), wa_ref[...], preferred_element_type=F32)
        yield
        a = _sigmoid(a_logit)
        lw = -DECAY_SCALE * _sigmoid(w_logit)
        kk = k * kk_ref[...]
        k2 = k * (1.0 + (a - 1.0) * ka_ref[...])
        kk = kk / jnp.maximum(jnp.sqrt(head_sum(kk * kk)), 1e-12)
        yield
        bonus = head_sum(r * k2 * rk_ref[...]) * v
        b = kk * a
        yield
        cum = chunk_cumsum(lw)
        ends = [cum[c + CHUNK - 1:c + CHUNK, :] for c in range(0, tm, CHUNK)]
        last = jnp.concatenate([jnp.broadcast_to(e, (CHUNK, A_WIDTH)) for e in ends], axis=0)
        yield
        p_inv = jnp.exp(-cum)
        to_end = jnp.exp(last - cum)
        sections = (lambda: -kk * jnp.exp(cum - lw), lambda: r * jnp.exp(cum), lambda: b * p_inv,
                    lambda: k2 * p_inv, lambda: b * to_end, lambda: k2 * to_end, lambda: v,
                    lambda: bonus)
        assert len(sections) == RWKV_SECTIONS
        for j, sec in enumerate(sections):
            ops_ref[:, j * A_WIDTH:(j + 1) * A_WIDTH] = sec().astype(ops_ref.dtype)
            if j % 3 == 2:
                yield
        pad = jnp.zeros((decay_ref.shape[0] - len(ends), A_WIDTH), F32)
        decay_ref[...] = jnp.concatenate([jnp.exp(e) for e in ends] + [pad], axis=0)

    def store_silu(ref):
        def store(y, c):
            ref[...] = silu(y).astype(ref.dtype)
        return store

    def store_latents(y, c):
        lat_ref[...] = y

    def store_gates(y, c):
        gate_ref[:, c:c + A_WIDTH] = _sigmoid(y + bg_ref[:, c:c + A_WIDTH]).astype(gate_ref.dtype)

    assert A_WIDTH == LAT_COLS == B_WIDTH
    gate_off = SHIFT_COLS + A_WIDTH + LAT_COLS + B_WIDTH
    jobs = [(SHIFT_COLS, store_silu(sa_ref), 0),
            (SHIFT_COLS + A_WIDTH, store_latents, 0),
            (SHIFT_COLS + A_WIDTH + LAT_COLS, store_silu(sb_ref), 0)]
    jobs += [(gate_off + c, store_gates, c) for c in range(0, 2 * D_MODEL, A_WIDTH)]
    pieces = ((0, LO_COLS, wlo_ref), (LO_COLS, PROJ_COLS, whi_ref))

    def project(off, n):
        outs = [jnp.dot(u, ref[:, max(off, lo) - lo:min(off + n, hi) - lo],
                        preferred_element_type=F32)
                for lo, hi, ref in pieces if max(off, lo) < min(off + n, hi)]
        return outs[0] if len(outs) == 1 else jnp.concatenate(outs, axis=1)

    phases = rwkv_operands(project(0, SHIFT_COLS))
    for off, store, c in jobs:
        next(phases, None)
        store(project(off, A_WIDTH), c)
    for _ in phases:
        pass


def _inproj(x2, g_pre, w_parts, b_gate, mu, rwkv_rows, wd_p, wa_p, seq, tm):
    t = x2.shape[0]
    assert tm % CHUNK == 0 and tm // CHUNK <= 8
    const = lambda shape: pl.BlockSpec(shape, lambda i: (0,) * len(shape))
    outs = ((A_WIDTH, BF16), (LAT_COLS, F32), (B_WIDTH, BF16), (2 * D_MODEL, BF16))
    return pl.pallas_call(
        functools.partial(_inproj_kernel, seq // tm),
        grid=(t // tm,),
        in_specs=[
            pl.BlockSpec((tm, D_MODEL), lambda i: (i, 0)),
            const((1, D_MODEL))] + [const(w.shape) for w in w_parts] + [const((1, 2 * D_MODEL)),
            const((1, SHIFT_COLS))] + [const((1, A_WIDTH))] * 5 + [const((LANES, A_WIDTH))] * 2,
        out_specs=[pl.BlockSpec((tm, RWKV_SECTIONS * A_WIDTH), lambda i: (i, 0)),
                   pl.BlockSpec((8, A_WIDTH), lambda i: (i, 0))]
        + [pl.BlockSpec((tm, n), lambda i: (i, 0)) for n, _ in outs],
        out_shape=[jax.ShapeDtypeStruct((t, RWKV_SECTIONS * A_WIDTH), BF16),
                   jax.ShapeDtypeStruct((t // tm * 8, A_WIDTH), F32)]
        + [jax.ShapeDtypeStruct((t, n), dt) for n, dt in outs],
        scratch_shapes=[pltpu.VMEM((8, SHIFT_COLS), F32)],
        compiler_params=pltpu.CompilerParams(
            dimension_semantics=("arbitrary",), vmem_limit_bytes=VMEM_LIMIT),
        name="inproj",
    )(x2, g_pre, *w_parts, b_gate, mu, *rwkv_rows, wd_p, wa_p)


def _split_bf16(x):
    hi = x.astype(BF16)
    return hi, (x - hi.astype(F32)).astype(BF16)


def _rwkv_kernel(ops_ref, decay_ref, sa_ref, gg_ref, gb_ref, out_ref, state_ref):
    tt = ops_ref.shape[0]
    n_chunks = tt // CHUNK

    @pl.when(pl.program_id(1) == 0)
    def _():
        state_ref[...] = jnp.zeros_like(state_ref)

    ri = lax.broadcasted_iota(jnp.int32, (LANES, LANES), 0)
    ci = lax.broadcasted_iota(jnp.int32, (LANES, LANES), 1)
    same_head = (ri // A_HEAD_DIM) == (ci // A_HEAD_DIM)
    strict_bd = same_head & (ci < ri)
    incl_bd = same_head & (ci <= ri)
    eye = (ri == ci).astype(F32)
    ones2 = jnp.concatenate([same_head, same_head], axis=0).astype(BF16)
    zeros_sq = jnp.zeros((LANES, LANES), F32)
    lane = lax.broadcasted_iota(jnp.int32, (1, LANES), 1)
    m_a = (lane < A_HEAD_DIM).astype(BF16)
    m_b = 1.0 - m_a

    def head_sum(x):
        hi, lo = _split_bf16(x)
        return jnp.dot(jnp.concatenate([hi, lo], axis=1), ones2, preferred_element_type=F32)

    def stack(z):
        return jnp.concatenate([z * m_a, z * m_b], axis=0)

    def unstack(z):
        return z[:CHUNK] + z[CHUNK:]

    pairs = range(N_PAIRS)
    items = [(p, ch) for p in pairs for ch in range(n_chunks)]
    each = lambda fn: {it: fn(it) for it in items}
    pcols = lambda p: slice(p * LANES, (p + 1) * LANES)

    def chunk_operands(it):
        p, ch = it
        sec = lambda j: ops_ref[ch * CHUNK:(ch + 1) * CHUNK,
                                j * A_WIDTH + p * LANES:j * A_WIDTH + (p + 1) * LANES]
        ends_t = [stack(sec(j)).astype(F32).T for j in (4, 5)]
        return dict(cts=stack(sec(0)), rt=sec(1), bt=sec(2), kt=sec(3), vs=stack(sec(6)),
                    decay=eye * decay_ref[ch:ch + 1, pcols(p)],
                    ends_t=jnp.concatenate(ends_t, axis=1))

    ops = each(chunk_operands)
    g = each(lambda it: _dot_nt(
        jnp.concatenate([ops[it]["cts"], stack(ops[it]["rt"])], axis=0),
        jnp.concatenate([ops[it]["bt"], ops[it]["bt"], ops[it]["kt"], ops[it]["kt"]], axis=0)))
    nb = each(lambda it: jnp.where(strict_bd, g[it][:PAIR, :PAIR], 0.0))
    nk = each(lambda it: jnp.where(strict_bd, g[it][:PAIR, PAIR:], 0.0))
    m_bk = each(lambda it: jnp.concatenate(
        [jnp.where(incl_bd, g[it][PAIR:, :PAIR], 0.0),
         jnp.where(incl_bd, g[it][PAIR:, PAIR:], 0.0)], axis=1))
    t_inv = each(lambda it: eye + nb[it])
    npow = each(lambda it: _dot(nb[it], nb[it]))
    nkv = each(lambda it: _dot(nk[it], ops[it]["vs"]))
    for _ in range(int(math.log2(CHUNK)) - 2):
        both = each(lambda it: _dot(npow[it], jnp.concatenate([npow[it], t_inv[it]], axis=1)))
        npow = each(lambda it: both[it][:, :PAIR])
        t_inv = each(lambda it: t_inv[it] + both[it][:, PAIR:])
    last_term = each(lambda it: _dot(npow[it], t_inv[it]))
    t_inv = each(lambda it: t_inv[it] + last_term[it])
    w12 = each(lambda it: _dot(t_inv[it], jnp.concatenate([ops[it]["cts"], nkv[it]], axis=1)))
    big = each(lambda it: _dot(
        jnp.concatenate([ops[it]["ends_t"], m_bk[it]], axis=0),
        jnp.concatenate([w12[it], jnp.concatenate([zeros_sq, ops[it]["vs"]], axis=1)], axis=0)))
    gt = each(lambda it: (big[it][:PAIR, :PAIR] + ops[it]["decay"]).astype(BF16))
    q1 = each(lambda it: ops[it]["rt"].astype(F32) + unstack(big[it][PAIR:, :PAIR]))

    h = {p: state_ref[p] for p in pairs}
    h_at = {}
    for ch in range(n_chunks):
        for p in pairs:
            it = (p, ch)
            h_at[it] = h[p]
            h_hi, h_lo = _split_bf16(h[p])
            h[p] = jnp.dot(jnp.concatenate([gt[it], gt[it]], axis=1),
                           jnp.concatenate([h_hi, h_lo], axis=0),
                           preferred_element_type=F32) + big[it][:PAIR, PAIR:]
    for p in pairs:
        state_ref[p] = h[p]
    y_it = each(lambda it: _dot(q1[it], h_at[it]) + unstack(big[it][PAIR:, PAIR:]))

    y = {p: jnp.concatenate([y_it[(p, ch)] for ch in range(n_chunks)], axis=0) for p in pairs}
    mean = {p: head_sum(y[p]) * (1.0 / A_HEAD_DIM) for p in pairs}
    d = {p: y[p] - mean[p] for p in pairs}
    var = {p: head_sum(d[p] * d[p]) * (1.0 / A_HEAD_DIM) for p in pairs}
    for p in pairs:
        bonus = ops_ref[:, 7 * A_WIDTH + p * LANES:7 * A_WIDTH + (p + 1) * LANES].astype(F32)
        y_n = d[p] * lax.rsqrt(var[p] + GN_EPS) * gg_ref[:, pcols(p)] + gb_ref[:, pcols(p)] + bonus
        out_ref[:, pcols(p)] = (y_n * sa_ref[:, pcols(p)].astype(F32)).astype(out_ref.dtype)


def _rwkv(ops, decay, silu_a, gn_gain, gn_bias, bsz, seq, tt):
    nt = seq // tt
    const = lambda shape: pl.BlockSpec(shape, lambda b, t: (0,) * len(shape))
    rows = lambda r, n: pl.BlockSpec((r, n), lambda b, t: (b * nt + t, 0))
    return pl.pallas_call(
        _rwkv_kernel,
        grid=(bsz, nt),
        in_specs=[rows(tt, RWKV_SECTIONS * A_WIDTH), rows(8, A_WIDTH), rows(tt, A_WIDTH),
                  const((1, A_WIDTH)), const((1, A_WIDTH))],
        out_specs=rows(tt, A_WIDTH),
        out_shape=jax.ShapeDtypeStruct((bsz * seq, A_WIDTH), BF16),
        scratch_shapes=[pltpu.VMEM((N_PAIRS, LANES, LANES), F32)],
        compiler_params=pltpu.CompilerParams(
            dimension_semantics=("arbitrary", "arbitrary"), vmem_limit_bytes=VMEM_LIMIT),
        name="rwkv7",
    )(ops, decay, silu_a, gn_gain, gn_bias)


QK_SCALE = math.log2(math.e) / math.sqrt(QK_NOPE_DIM + QK_ROPE_DIM)


def _mla_prep_kernel(lat_ref, pos_ref, invf_ref, gq_ref, gkv_ref, wqt_ref, wk_ref, wvt_ref,
                     qt_ref, k_ref, vt_ref):
    lat = lat_ref[...]
    tm = lat.shape[0]
    pos = pos_ref[0].astype(F32)
    ang = invf_ref[...] * pos
    pad = jnp.zeros((LANES - QK_NOPE_DIM - QK_ROPE_DIM, tm), F32)
    cos_q = jnp.concatenate([jnp.ones((QK_NOPE_DIM, tm), F32), jnp.cos(ang), pad], axis=0)
    sin_t = jnp.concatenate([jnp.zeros((QK_NOPE_DIM, tm), F32), jnp.sin(ang), pad], axis=0)

    cq_t = jnp.concatenate([lat[:, j * LANES:(j + 1) * LANES].T
                            for j in range(Q_LORA_RANK // LANES)], axis=0)
    nq_t = cq_t * lax.rsqrt(jnp.mean(cq_t * cq_t, axis=0, keepdims=True) + NORM_EPS) * gq_ref[...]
    q_raw = _dot(wqt_ref[...], nq_t)
    for h in range(B_HEADS):
        sl = slice(h * LANES, (h + 1) * LANES)
        blk = q_raw[sl]
        rot = blk * cos_q + pltpu.roll(blk, LANES - QK_ROPE_DIM, 0) * sin_t
        qt_ref[0, sl, :] = (rot * QK_SCALE).astype(qt_ref.dtype)

    ckv = lat[:, Q_LORA_RANK:Q_LORA_RANK + KV_LORA_RANK]
    nkv = ckv * lax.rsqrt(jnp.mean(ckv * ckv, axis=-1, keepdims=True) + NORM_EPS) * gkv_ref[...]
    vt_ref[0] = _dot(wvt_ref[...], nkv.T).astype(vt_ref.dtype)
    k_nope = _dot(nkv, wk_ref[...])
    lane = lax.broadcasted_iota(jnp.int32, (1, LANES), 1)
    rope_lane = (lane >= QK_NOPE_DIM) & (lane < QK_NOPE_DIM + QK_ROPE_DIM)
    pe = lat[:, Q_LORA_RANK + KV_LORA_RANK:]
    k_pe = pe * jnp.where(rope_lane, cos_q.T, 0.0) + pltpu.roll(pe, LANES - QK_ROPE_DIM, 1) * sin_t.T
    for h in range(B_HEADS):
        sl = slice(h * LANES, (h + 1) * LANES)
        k_ref[:, sl] = (k_nope[:, sl] + k_pe).astype(k_ref.dtype)


def _mla_prep(lat, pos3, invf, g_q, g_kv, wqt_p, wk_p, wvt_p, bsz, seq, tm):
    t = lat.shape[0]
    nt = seq // tm
    const = lambda shape: pl.BlockSpec(shape, lambda i: (0,) * len(shape))
    return pl.pallas_call(
        _mla_prep_kernel,
        grid=(t // tm,),
        in_specs=[
            pl.BlockSpec((tm, LAT_COLS), lambda i: (i, 0)),
            pl.BlockSpec((1, 1, tm), lambda i: (i, 0, 0)),
            const((QK_ROPE_DIM, 1)), const((Q_LORA_RANK, 1)), const((1, KV_LORA_RANK)),
            const((B_HEADS * LANES, Q_LORA_RANK)), const((KV_LORA_RANK, B_HEADS * LANES)),
            const((B_WIDTH, KV_LORA_RANK)),
        ],
        out_specs=[
            pl.BlockSpec((1, B_HEADS * LANES, tm), lambda i: (i // nt, 0, i % nt)),
            pl.BlockSpec((tm, B_HEADS * LANES), lambda i: (i, 0)),
            pl.BlockSpec((1, B_WIDTH, tm), lambda i: (i // nt, 0, i % nt)),
        ],
        out_shape=[
            jax.ShapeDtypeStruct((bsz, B_HEADS * LANES, seq), BF16),
            jax.ShapeDtypeStruct((t, B_HEADS * LANES), BF16),
            jax.ShapeDtypeStruct((bsz, B_WIDTH, seq), BF16),
        ],
        compiler_params=pltpu.CompilerParams(
            dimension_semantics=("arbitrary",), vmem_limit_bytes=VMEM_LIMIT),
        name="mla_prep",
    )(lat, pos3, invf, g_q, g_kv, wqt_p, wk_p, wvt_p)


NEG_BIG = -1e30
ATT_TK = 512
ATT_TQ = 2048
ATT_CB = 256


def _attn_kernel(qt_ref, k_ref, vt_ref, sb_ref, out_ref, s_ref):
    tq = qt_ref.shape[2]
    tk, cb = ATT_TK, ATT_CB
    i = pl.program_id(2)
    q_t = (qt_ref[0, :LANES, :], qt_ref[0, LANES:, :])
    ones_rows = jnp.ones((16, tk), BF16)

    def units(q0):
        return [(h, c) for h in range(2) for c in range(q0, tq, cb)]

    def score_unit(j, h, c):
        start = pl.multiple_of(j * tk, tk)
        return jnp.dot(k_ref[0, pl.ds(start, tk), h * LANES:(h + 1) * LANES], q_t[h][:, c:c + cb],
                       preferred_element_type=F32)

    def tile(j, carry, get_s, prefetch, q0=0):
        start = pl.multiple_of(j * tk, tk)
        vb = vt_ref[0, :, pl.ds(start, tk)]
        v_aug = [jnp.concatenate([vb[h * V_HEAD_DIM:(h + 1) * V_HEAD_DIM, :], ones_rows], axis=0)
                 for h in range(2)]
        cols = [[tuple(x[:, :q0] for x in carry[h])] if q0 else [] for h in range(2)]
        for u, (h, c) in enumerate(units(q0)):
            prefetch(u)
            s_u = get_s(u, h, c)
            m, l, acc = [x[:, c:c + cb] for x in carry[h]]
            m_new = jnp.maximum(m, jnp.max(s_u, axis=0, keepdims=True))
            alpha = jnp.exp2(m - m_new)
            p_u = jnp.exp2(s_u - m_new).astype(BF16)
            pv = jnp.dot(v_aug[h], p_u, preferred_element_type=F32)
            cols[h].append((m_new, l * alpha + pv[V_HEAD_DIM:V_HEAD_DIM + 1],
                            acc * alpha + pv[:V_HEAD_DIM]))
        return tuple(tuple(jnp.concatenate(parts, axis=1) if len(parts) > 1 else parts[0]
                           for parts in zip(*cols[h])) for h in range(2))

    init = tuple((jnp.full((1, tq), NEG_BIG, F32), jnp.zeros((1, tq), F32),
                  jnp.zeros((V_HEAD_DIM, tq), F32)) for _ in range(2))
    n_full = i * (tq // tk)
    all_units = units(0)

    def visible_tile(j, slot, carry):
        def prefetch(u):
            h, c = all_units[u]
            s_ref[1 - slot, u] = score_unit(j + 1, h, c)
        return tile(j, carry, lambda u, h, c: s_ref[slot, u], prefetch)

    def two_tiles(jj, carry):
        return visible_tile(2 * jj + 1, 1, visible_tile(2 * jj, 0, carry))

    assert (tq // tk) % 2 == 0
    for u, (h, c) in enumerate(all_units):
        s_ref[0, u] = score_unit(0, h, c)
    carry = lax.fori_loop(0, n_full // 2, two_tiles, init)

    key = lax.broadcasted_iota(jnp.int32, (tk, cb), 0)
    query = lax.broadcasted_iota(jnp.int32, (tk, cb), 1)
    ahead = {}
    for d in range(tq // tk):
        q0 = d * tk
        here, ahead = ahead, {}
        nxt = units(q0 + tk) if d + 1 < tq // tk else []

        def prefetch(u, d=d, nxt=nxt, ahead=ahead):
            if u < len(nxt):
                ahead[nxt[u]] = score_unit(n_full + d + 1, *nxt[u])

        def get_s(u, h, c, d=d, q0=q0, here=here):
            s_u = s_ref[0, u] if d == 0 else here[h, c]
            on_diagonal = q0 <= c < q0 + tk
            return jnp.where(key <= query + (c - q0), s_u, NEG_BIG) if on_diagonal else s_u

        carry = tile(n_full + d, carry, get_s, prefetch, q0)
    o_t = jnp.concatenate([acc / l for (_, l, acc) in carry], axis=0)
    out_ref[0] = (o_t.T * sb_ref[0].astype(F32)).astype(out_ref.dtype)


def _attention(q_t, k, v_t, zb, tq):
    bsz, seq, _ = k.shape
    return pl.pallas_call(
        _attn_kernel,
        grid=(bsz, B_HEADS // 2, seq // tq),
        in_specs=[
            pl.BlockSpec((1, 2 * LANES, tq), lambda b, p, i: (b, p, i)),
            pl.BlockSpec((1, seq, 2 * LANES), lambda b, p, i: (b, 0, p)),
            pl.BlockSpec((1, 2 * V_HEAD_DIM, seq), lambda b, p, i: (b, p, 0)),
            pl.BlockSpec((1, tq, LANES), lambda b, p, i: (b, i, p)),
        ],
        out_specs=pl.BlockSpec((1, tq, LANES), lambda b, p, i: (b, i, p)),
        out_shape=jax.ShapeDtypeStruct((bsz, seq, B_WIDTH), BF16),
        scratch_shapes=[pltpu.VMEM((2, 2 * (tq // ATT_CB), ATT_TK, ATT_CB), F32)],
        compiler_params=pltpu.CompilerParams(
            dimension_semantics=("arbitrary", "arbitrary", "arbitrary"),
            vmem_limit_bytes=VMEM_LIMIT),
        name="mla_attn",
    )(q_t, k, v_t, zb)


def _out_kernel(ya_ref, yb_ref, gate_ref, x_ref, woa_ref, wob_ref, wo_ref, gp_ref, out_ref):
    y_a = jnp.dot(ya_ref[...], woa_ref[...], preferred_element_type=F32)
    y_b = jnp.dot(yb_ref[...], wob_ref[...], preferred_element_type=F32)
    gates = gate_ref[...].astype(F32)
    merged = gates[:, :D_MODEL] * y_a + gates[:, D_MODEL:] * y_b
    o = _dot(merged, wo_ref[...])
    ms = jnp.mean(o * o, axis=-1, keepdims=True)
    out_ref[...] = x_ref[...] + o * lax.rsqrt(ms + NORM_EPS) * gp_ref[...]


def _out_proj(ya, yb, gates, x2, woa, wob, wo, g_post, tm):
    t = x2.shape[0]
    const = lambda shape: pl.BlockSpec(shape, lambda i: (0,) * len(shape))
    rows = lambda n: pl.BlockSpec((tm, n), lambda i: (i, 0))
    return pl.pallas_call(
        _out_kernel,
        grid=(t // tm,),
        in_specs=[
            rows(A_WIDTH), rows(B_WIDTH), rows(2 * D_MODEL), rows(D_MODEL),
            const((A_WIDTH, D_MODEL)), const((B_WIDTH, D_MODEL)),
            const((D_MODEL, D_MODEL)), const((1, D_MODEL)),
        ],
        out_specs=rows(D_MODEL),
        out_shape=jax.ShapeDtypeStruct((t, D_MODEL), F32),
        compiler_params=pltpu.CompilerParams(
            dimension_semantics=("arbitrary",), vmem_limit_bytes=VMEM_LIMIT),
        name="out_proj",
    )(ya, yb, gates, x2, woa, wob, wo, g_post)


def _rope_partner(w_pe):
    half = QK_ROPE_DIM // 2
    return jnp.concatenate([-w_pe[..., half:], w_pe[..., :half]], axis=-1)


def _prep_weights(w_in, w_decay_up, w_iclr_up, w_uq, w_ukv):
    o = np.cumsum([0, SHIFT_COLS, A_WIDTH, Q_LORA_RANK, KV_LORA_RANK, QK_ROPE_DIM, B_WIDTH])
    w_pe = w_in[:, o[4]:o[5]]
    kpe_blk = jnp.concatenate(
        [jnp.zeros((D_MODEL, QK_NOPE_DIM), F32), w_pe, _rope_partner(w_pe)], axis=-1)
    w_parts = (jnp.concatenate([w_in[:, :o[4]].astype(BF16), kpe_blk.astype(BF16)], axis=-1),
               w_in[:, o[5]:].astype(BF16))
    zeros_lora = jnp.zeros((DECAY_LORA, A_WIDTH), F32)
    wd_p = jnp.concatenate([w_decay_up, zeros_lora], axis=0).astype(BF16)
    wa_p = jnp.concatenate([zeros_lora, w_iclr_up], axis=0).astype(BF16)
    wq = w_uq.reshape(Q_LORA_RANK, B_HEADS, QK_NOPE_DIM + QK_ROPE_DIM)
    wqt_p = jnp.concatenate(
        [wq, _rope_partner(wq[..., QK_NOPE_DIM:])], axis=-1).reshape(Q_LORA_RANK, -1).T.astype(BF16)
    wkv = w_ukv.reshape(KV_LORA_RANK, B_HEADS, QK_NOPE_DIM + V_HEAD_DIM)
    wk_p = jnp.concatenate(
        [wkv[..., :QK_NOPE_DIM], jnp.zeros_like(wkv[..., :QK_NOPE_DIM])],
        axis=-1).reshape(KV_LORA_RANK, -1).astype(BF16)
    wvt_p = wkv[..., QK_NOPE_DIM:].reshape(KV_LORA_RANK, -1).T.astype(BF16)
    return w_parts, wd_p, wa_p, wqt_p, wk_p, wvt_p


def kernel(x, positions, g_pre, w_in, b_gate, mu_shift, w0, w_decay_up, a0, w_iclr_up, k_k, k_a,
           r_k, gn_gain, gn_bias, w_out_a, g_q, w_uq, g_kv, w_ukv, w_out_b, w_o, g_post):
    bsz, seq, _ = x.shape
    t = bsz * seq
    tm = TM_ROWS
    row = lambda a: a.reshape(1, -1).astype(F32)
    w_parts, wd_p, wa_p, wqt_p, wk_p, wvt_p = _prep_weights(w_in, w_decay_up, w_iclr_up, w_uq, w_ukv)
    x2 = x.reshape(t, D_MODEL)

    assert TM_INPROJ == RWKV_TT
    rwkv_ops, decay, silu_a, lat, silu_b, gates = _inproj(
        x2, row(g_pre), w_parts, row(b_gate), row(mu_shift),
        [row(w0), row(a0), row(k_k), row(k_a), row(r_k)], wd_p, wa_p, seq, TM_INPROJ)

    ya = _rwkv(rwkv_ops, decay, silu_a, row(gn_gain), row(gn_bias), bsz, seq, RWKV_TT)

    freq = ROPE_THETA ** (-(np.arange(QK_ROPE_DIM) % (QK_ROPE_DIM // 2)) * 2.0 / QK_ROPE_DIM)
    invf = jnp.asarray(freq.reshape(QK_ROPE_DIM, 1), F32)
    pos3 = positions.reshape(t // tm, 1, tm)
    q_t, k, v_t = _mla_prep(lat, pos3, invf, g_q.reshape(-1, 1).astype(F32), row(g_kv),
                            wqt_p, wk_p, wvt_p, bsz, seq, tm)
    yb = _attention(q_t, k.reshape(bsz, seq, -1), v_t, silu_b.reshape(bsz, seq, -1), ATT_TQ)

    out = _out_proj(ya, yb.reshape(t, B_WIDTH), gates, x2, w_out_a.astype(BF16),
                    w_out_b.astype(BF16), w_o.astype(BF16), row(g_post), tm)
    return out.reshape(bsz, seq, D_MODEL)
```

```python
import functools
import math

import jax
import jax.numpy as jnp
import numpy as np
from jax import lax
from jax.experimental import pallas as pl
from jax.experimental.pallas import tpu as pltpu

D_MODEL = 1024
A_HEADS = 8
A_HEAD_DIM = 64
A_WIDTH = A_HEADS * A_HEAD_DIM
DECAY_LORA = 64
ICLR_LORA = 64
DECAY_SCALE = 0.6065306597
GN_EPS = 64e-5
B_HEADS = 8
QK_NOPE_DIM = 64
QK_ROPE_DIM = 32
V_HEAD_DIM = 64
Q_LORA_RANK = 256
KV_LORA_RANK = 128
B_WIDTH = B_HEADS * V_HEAD_DIM
ROPE_THETA = 10000.0
NORM_EPS = 1e-6
SHIFT_COLS = 3 * A_WIDTH + DECAY_LORA + ICLR_LORA

LANES = 128
PAIR = 2 * A_HEAD_DIM
N_PAIRS = A_HEADS // 2
CHUNK = 64
LAT_COLS = 512
PROJ_COLS = SHIFT_COLS + A_WIDTH + LAT_COLS + B_WIDTH + 2 * D_MODEL
LO_COLS = SHIFT_COLS + A_WIDTH + LAT_COLS
VMEM_LIMIT = 52 * 1024 * 1024
TM_INPROJ = 256
TM_ROWS = 1024
RWKV_TT = 256

F32 = jnp.float32
BF16 = jnp.bfloat16
HI = lax.Precision.HIGHEST


def _dot(a, b):
    return jnp.dot(a.astype(BF16), b.astype(BF16), preferred_element_type=F32)


def _dot_nt(a, b):
    return lax.dot_general(a.astype(BF16), b.astype(BF16), (((1,), (1,)), ((), ())),
                           preferred_element_type=F32)


def _sigmoid(x):
    return 0.5 * jnp.tanh(0.5 * x) + 0.5


RWKV_SECTIONS = 8


def _inproj_kernel(tiles_per_seq, x_ref, g_ref, wlo_ref, whi_ref, bg_ref, mu_ref, w0_ref, a0_ref, kk_ref, ka_ref,
                   rk_ref, wd_ref, wa_ref,
                   ops_ref, decay_ref, sa_ref, lat_ref, sb_ref, gate_ref, last_ref):
    @pl.when(pl.program_id(0) % tiles_per_seq == 0)
    def _():
        last_ref[...] = jnp.zeros_like(last_ref)

    x = x_ref[...]
    tm = x.shape[0]
    ms = jnp.mean(x * x, axis=-1, keepdims=True)
    u = (x * lax.rsqrt(ms + NORM_EPS) * g_ref[...]).astype(BF16)
    silu = lambda z: z * _sigmoid(z)
    first_row = lax.broadcasted_iota(jnp.int32, (tm, 1), 0) == 0

    def token_shift(feat):
        prev = jnp.where(first_row, last_ref[0:1, :], pltpu.roll(feat, 1, 0))
        last_ref[0:1, :] = feat[tm - 1:tm, :]
        return feat + mu_ref[...] * (prev - feat)

    first_head = lax.broadcasted_iota(jnp.int32, (1, LANES), 1) < A_HEAD_DIM
    pos_in_chunk = lax.broadcasted_iota(jnp.int32, (tm, 1), 0) % CHUNK

    def head_sum(z):
        out = []
        for p in range(N_PAIRS):
            blk = z[:, p * LANES:(p + 1) * LANES]
            s_a = jnp.sum(jnp.where(first_head, blk, 0.0), axis=1, keepdims=True)
            s_b = jnp.sum(jnp.where(first_head, 0.0, blk), axis=1, keepdims=True)
            out.append(jnp.where(first_head, s_a, s_b))
        return jnp.concatenate(out, axis=1)

    tok_r = lax.broadcasted_iota(jnp.int32, (tm, tm), 0)
    tok_c = lax.broadcasted_iota(jnp.int32, (tm, tm), 1)
    prefix = ((tok_c <= tok_r) & (tok_c // CHUNK == tok_r // CHUNK)).astype(BF16)

    def chunk_cumsum(z):
        hi = z.astype(BF16)
        lo = (z - hi.astype(F32)).astype(BF16)
        return (jnp.dot(prefix, hi, preferred_element_type=F32)
                + jnp.dot(prefix, lo, preferred_element_type=F32))

    def rwkv_operands(feat):
        f = token_shift(feat)
        r, k, v = (f[:, j * A_WIDTH:(j + 1) * A_WIDTH] for j in range(3))
        la = f[:, 3 * A_WIDTH:]
        w_logit = w0_ref[...] + jnp.dot(jnp.tanh(la).astype(BF16), wd_ref[...],
                                        preferred_element_type=F32)
        a_logit = a0_ref[...] + jnp.dot(la.astype(BF16), wa_ref[...], preferred_element_type=F32)
        yield
        a = _sigmoid(a_logit)
        lw = -DECAY_SCALE * _sigmoid(w_logit)
        kk = k * kk_ref[...]
        k2 = k * (1.0 + (a - 1.0) * ka_ref[...])
        kk = kk / jnp.maximum(jnp.sqrt(head_sum(kk * kk)), 1e-12)
        yield
        bonus = head_sum(r * k2 * rk_ref[...]) * v
        b = kk * a
        yield
        cum = chunk_cumsum(lw)
        ends = [cum[c + CHUNK - 1:c + CHUNK, :] for c in range(0, tm, CHUNK)]
        last = jnp.concatenate([jnp.broadcast_to(e, (CHUNK, A_WIDTH)) for e in ends], axis=0)
        yield
        p_inv = jnp.exp(-cum)
        to_end = jnp.exp(last - cum)
        sections = (lambda: -kk * jnp.exp(cum - lw), lambda: r * jnp.exp(cum), lambda: b * p_inv,
                    lambda: k2 * p_inv, lambda: b * to_end, lambda: k2 * to_end, lambda: v,
                    lambda: bonus)
        assert len(sections) == RWKV_SECTIONS
        for j, sec in enumerate(sections):
            ops_ref[:, j * A_WIDTH:(j + 1) * A_WIDTH] = sec().astype(ops_ref.dtype)
            if j % 3 == 2:
                yield
        n_pad = decay_ref.shape[0] - len(ends)
        pad = [jnp.zeros((n_pad, A_WIDTH), F32)] if n_pad else []
        decay_ref[...] = jnp.concatenate([jnp.exp(e) for e in ends] + pad, axis=0)

    def store_silu(ref):
        def store(y, c):
            ref[...] = silu(y).astype(ref.dtype)
        return store

    def store_latents(y, c):
        lat_ref[...] = y

    def store_gates(y, c):
        gate_ref[:, c:c + A_WIDTH] = _sigmoid(y + bg_ref[:, c:c + A_WIDTH]).astype(gate_ref.dtype)

    assert A_WIDTH == LAT_COLS == B_WIDTH
    gate_off = SHIFT_COLS + A_WIDTH + LAT_COLS + B_WIDTH
    jobs = [(SHIFT_COLS, store_silu(sa_ref), 0),
            (SHIFT_COLS + A_WIDTH, store_latents, 0),
            (SHIFT_COLS + A_WIDTH + LAT_COLS, store_silu(sb_ref), 0)]
    jobs += [(gate_off + c, store_gates, c) for c in range(0, 2 * D_MODEL, A_WIDTH)]
    pieces = ((0, LO_COLS, wlo_ref), (LO_COLS, PROJ_COLS, whi_ref))

    def project(off, n):
        outs = [jnp.dot(u, ref[:, max(off, lo) - lo:min(off + n, hi) - lo],
                        preferred_element_type=F32)
                for lo, hi, ref in pieces if max(off, lo) < min(off + n, hi)]
        return outs[0] if len(outs) == 1 else jnp.concatenate(outs, axis=1)

    phases = rwkv_operands(project(0, SHIFT_COLS))
    for off, store, c in jobs:
        next(phases, None)
        store(project(off, A_WIDTH), c)
    for _ in phases:
        pass


def _inproj(x2, g_pre, w_parts, b_gate, mu, rwkv_rows, wd_p, wa_p, seq, tm):
    t = x2.shape[0]
    assert tm % CHUNK == 0 and tm // CHUNK <= 8
    const = lambda shape: pl.BlockSpec(shape, lambda i: (0,) * len(shape))
    outs = ((A_WIDTH, BF16), (LAT_COLS, F32), (B_WIDTH, BF16), (2 * D_MODEL, BF16))
    return pl.pallas_call(
        functools.partial(_inproj_kernel, seq // tm),
        grid=(t // tm,),
        in_specs=[
            pl.BlockSpec((tm, D_MODEL), lambda i: (i, 0)),
            const((1, D_MODEL))] + [const(w.shape) for w in w_parts] + [const((1, 2 * D_MODEL)),
            const((1, SHIFT_COLS))] + [const((1, A_WIDTH))] * 5 + [const((LANES, A_WIDTH))] * 2,
        out_specs=[pl.BlockSpec((tm, RWKV_SECTIONS * A_WIDTH), lambda i: (i, 0)),
                   pl.BlockSpec((8, A_WIDTH), lambda i: (i, 0))]
        + [pl.BlockSpec((tm, n), lambda i: (i, 0)) for n, _ in outs],
        out_shape=[jax.ShapeDtypeStruct((t, RWKV_SECTIONS * A_WIDTH), BF16),
                   jax.ShapeDtypeStruct((t // tm * 8, A_WIDTH), F32)]
        + [jax.ShapeDtypeStruct((t, n), dt) for n, dt in outs],
        scratch_shapes=[pltpu.VMEM((8, SHIFT_COLS), F32)],
        compiler_params=pltpu.CompilerParams(
            dimension_semantics=("arbitrary",), vmem_limit_bytes=VMEM_LIMIT),
        name="inproj",
    )(x2, g_pre, *w_parts, b_gate, mu, *rwkv_rows, wd_p, wa_p)


def _split_bf16(x):
    hi = x.astype(BF16)
    return hi, (x - hi.astype(F32)).astype(BF16)


def _rwkv_kernel(ops_ref, decay_ref, sa_ref, gg_ref, gb_ref, out_ref, state_ref):
    tt = ops_ref.shape[0]
    n_chunks = tt // CHUNK

    @pl.when(pl.program_id(1) == 0)
    def _():
        state_ref[...] = jnp.zeros_like(state_ref)

    ri = lax.broadcasted_iota(jnp.int32, (LANES, LANES), 0)
    ci = lax.broadcasted_iota(jnp.int32, (LANES, LANES), 1)
    same_head = (ri // A_HEAD_DIM) == (ci // A_HEAD_DIM)
    strict_bd = same_head & (ci < ri)
    incl_bd = same_head & (ci <= ri)
    eye = (ri == ci).astype(F32)
    ones2 = jnp.concatenate([same_head, same_head], axis=0).astype(BF16)
    zeros_sq = jnp.zeros((LANES, LANES), F32)
    lane = lax.broadcasted_iota(jnp.int32, (1, LANES), 1)
    m_a = (lane < A_HEAD_DIM).astype(BF16)
    m_b = 1.0 - m_a

    def head_sum(x):
        hi, lo = _split_bf16(x)
        return jnp.dot(jnp.concatenate([hi, lo], axis=1), ones2, preferred_element_type=F32)

    def stack(z):
        return jnp.concatenate([z * m_a, z * m_b], axis=0)

    def unstack(z):
        return z[:CHUNK] + z[CHUNK:]

    pairs = range(N_PAIRS)
    items = [(p, ch) for p in pairs for ch in range(n_chunks)]
    each = lambda fn: {it: fn(it) for it in items}
    pcols = lambda p: slice(p * LANES, (p + 1) * LANES)

    def chunk_operands(it):
        p, ch = it
        sec = lambda j: ops_ref[ch * CHUNK:(ch + 1) * CHUNK,
                                j * A_WIDTH + p * LANES:j * A_WIDTH + (p + 1) * LANES]
        ends_t = [stack(sec(j)).astype(F32).T for j in (4, 5)]
        return dict(cts=stack(sec(0)), rt=sec(1), bt=sec(2), kt=sec(3), vs=stack(sec(6)),
                    decay=eye * decay_ref[ch:ch + 1, pcols(p)],
                    ends_t=jnp.concatenate(ends_t, axis=1))

    ops = each(chunk_operands)
    g = each(lambda it: _dot_nt(
        jnp.concatenate([ops[it]["cts"], stack(ops[it]["rt"])], axis=0),
        jnp.concatenate([ops[it]["bt"], ops[it]["bt"], ops[it]["kt"], ops[it]["kt"]], axis=0)))
    nb = each(lambda it: jnp.where(strict_bd, g[it][:PAIR, :PAIR], 0.0))
    nk = each(lambda it: jnp.where(strict_bd, g[it][:PAIR, PAIR:], 0.0))
    m_bk = each(lambda it: jnp.concatenate(
        [jnp.where(incl_bd, g[it][PAIR:, :PAIR], 0.0),
         jnp.where(incl_bd, g[it][PAIR:, PAIR:], 0.0)], axis=1))
    t_inv = each(lambda it: eye + nb[it])
    npow = each(lambda it: _dot(nb[it], nb[it]))
    nkv = each(lambda it: _dot(nk[it], ops[it]["vs"]))
    for _ in range(int(math.log2(CHUNK)) - 2):
        both = each(lambda it: _dot(npow[it], jnp.concatenate([npow[it], t_inv[it]], axis=1)))
        npow = each(lambda it: both[it][:, :PAIR])
        t_inv = each(lambda it: t_inv[it] + both[it][:, PAIR:])
    last_term = each(lambda it: _dot(npow[it], t_inv[it]))
    t_inv = each(lambda it: t_inv[it] + last_term[it])
    w12 = each(lambda it: _dot(t_inv[it], jnp.concatenate([ops[it]["cts"], nkv[it]], axis=1)))
    big = each(lambda it: _dot(
        jnp.concatenate([ops[it]["ends_t"], m_bk[it]], axis=0),
        jnp.concatenate([w12[it], jnp.concatenate([zeros_sq, ops[it]["vs"]], axis=1)], axis=0)))
    gt = each(lambda it: (big[it][:PAIR, :PAIR] + ops[it]["decay"]).astype(BF16))
    q1 = each(lambda it: ops[it]["rt"].astype(F32) + unstack(big[it][PAIR:, :PAIR]))

    h = {p: state_ref[p] for p in pairs}
    h_at = {}
    for ch in range(n_chunks):
        for p in pairs:
            it = (p, ch)
            h_at[it] = h[p]
            h_hi, h_lo = _split_bf16(h[p])
            h[p] = jnp.dot(jnp.concatenate([gt[it], gt[it]], axis=1),
                           jnp.concatenate([h_hi, h_lo], axis=0),
                           preferred_element_type=F32) + big[it][:PAIR, PAIR:]
    for p in pairs:
        state_ref[p] = h[p]
    y_it = each(lambda it: _dot(q1[it], h_at[it]) + unstack(big[it][PAIR:, PAIR:]))

    y = {p: jnp.concatenate([y_it[(p, ch)] for ch in range(n_chunks)], axis=0) for p in pairs}
    mean = {p: head_sum(y[p]) * (1.0 / A_HEAD_DIM) for p in pairs}
    d = {p: y[p] - mean[p] for p in pairs}
    var = {p: head_sum(d[p] * d[p]) * (1.0 / A_HEAD_DIM) for p in pairs}
    for p in pairs:
        bonus = ops_ref[:, 7 * A_WIDTH + p * LANES:7 * A_WIDTH + (p + 1) * LANES].astype(F32)
        y_n = d[p] * lax.rsqrt(var[p] + GN_EPS) * gg_ref[:, pcols(p)] + gb_ref[:, pcols(p)] + bonus
        out_ref[:, pcols(p)] = (y_n * sa_ref[:, pcols(p)].astype(F32)).astype(out_ref.dtype)


def _rwkv(ops, decay, silu_a, gn_gain, gn_bias, bsz, seq, tt):
    nt = seq // tt
    const = lambda shape: pl.BlockSpec(shape, lambda b, t: (0,) * len(shape))
    rows = lambda r, n: pl.BlockSpec((r, n), lambda b, t: (b * nt + t, 0))
    return pl.pallas_call(
        _rwkv_kernel,
        grid=(bsz, nt),
        in_specs=[rows(tt, RWKV_SECTIONS * A_WIDTH), rows(8, A_WIDTH), rows(tt, A_WIDTH),
                  const((1, A_WIDTH)), const((1, A_WIDTH))],
        out_specs=rows(tt, A_WIDTH),
        out_shape=jax.ShapeDtypeStruct((bsz * seq, A_WIDTH), BF16),
        scratch_shapes=[pltpu.VMEM((N_PAIRS, LANES, LANES), F32)],
        compiler_params=pltpu.CompilerParams(
            dimension_semantics=("arbitrary", "arbitrary"), vmem_limit_bytes=VMEM_LIMIT),
        name="rwkv7",
    )(ops, decay, silu_a, gn_gain, gn_bias)


QK_SCALE = math.log2(math.e) / math.sqrt(QK_NOPE_DIM + QK_ROPE_DIM)


def _mla_prep_kernel(lat_ref, pos_ref, invf_ref, gq_ref, gkv_ref, wqt_ref, wk_ref, wvt_ref,
                     qt_ref, k_ref, vt_ref):
    lat = lat_ref[...]
    tm = lat.shape[0]
    pos = pos_ref[0].astype(F32)
    ang = invf_ref[...] * pos
    pad = jnp.zeros((LANES - QK_NOPE_DIM - QK_ROPE_DIM, tm), F32)
    cos_q = jnp.concatenate([jnp.ones((QK_NOPE_DIM, tm), F32), jnp.cos(ang), pad], axis=0)
    sin_t = jnp.concatenate([jnp.zeros((QK_NOPE_DIM, tm), F32), jnp.sin(ang), pad], axis=0)

    cq_t = jnp.concatenate([lat[:, j * LANES:(j + 1) * LANES].T
                            for j in range(Q_LORA_RANK // LANES)], axis=0)
    nq_t = cq_t * lax.rsqrt(jnp.mean(cq_t * cq_t, axis=0, keepdims=True) + NORM_EPS) * gq_ref[...]
    q_raw = _dot(wqt_ref[...], nq_t)
    for h in range(B_HEADS):
        sl = slice(h * LANES, (h + 1) * LANES)
        blk = q_raw[sl]
        rot = blk * cos_q + pltpu.roll(blk, LANES - QK_ROPE_DIM, 0) * sin_t
        qt_ref[0, sl, :] = (rot * QK_SCALE).astype(qt_ref.dtype)

    ckv = lat[:, Q_LORA_RANK:Q_LORA_RANK + KV_LORA_RANK]
    nkv = ckv * lax.rsqrt(jnp.mean(ckv * ckv, axis=-1, keepdims=True) + NORM_EPS) * gkv_ref[...]
    vt_ref[0] = _dot(wvt_ref[...], nkv.T).astype(vt_ref.dtype)
    k_nope = _dot(nkv, wk_ref[...])
    lane = lax.broadcasted_iota(jnp.int32, (1, LANES), 1)
    rope_lane = (lane >= QK_NOPE_DIM) & (lane < QK_NOPE_DIM + QK_ROPE_DIM)
    pe = lat[:, Q_LORA_RANK + KV_LORA_RANK:]
    k_pe = pe * jnp.where(rope_lane, cos_q.T, 0.0) + pltpu.roll(pe, LANES - QK_ROPE_DIM, 1) * sin_t.T
    for h in range(B_HEADS):
        sl = slice(h * LANES, (h + 1) * LANES)
        k_ref[:, sl] = (k_nope[:, sl] + k_pe).astype(k_ref.dtype)


def _mla_prep(lat, pos3, invf, g_q, g_kv, wqt_p, wk_p, wvt_p, bsz, seq, tm):
    t = lat.shape[0]
    nt = seq // tm
    const = lambda shape: pl.BlockSpec(shape, lambda i: (0,) * len(shape))
    return pl.pallas_call(
        _mla_prep_kernel,
        grid=(t // tm,),
        in_specs=[
            pl.BlockSpec((tm, LAT_COLS), lambda i: (i, 0)),
            pl.BlockSpec((1, 1, tm), lambda i: (i, 0, 0)),
            const((QK_ROPE_DIM, 1)), const((Q_LORA_RANK, 1)), const((1, KV_LORA_RANK)),
            const((B_HEADS * LANES, Q_LORA_RANK)), const((KV_LORA_RANK, B_HEADS * LANES)),
            const((B_WIDTH, KV_LORA_RANK)),
        ],
        out_specs=[
            pl.BlockSpec((1, B_HEADS * LANES, tm), lambda i: (i // nt, 0, i % nt)),
            pl.BlockSpec((tm, B_HEADS * LANES), lambda i: (i, 0)),
            pl.BlockSpec((1, B_WIDTH, tm), lambda i: (i // nt, 0, i % nt)),
        ],
        out_shape=[
            jax.ShapeDtypeStruct((bsz, B_HEADS * LANES, seq), BF16),
            jax.ShapeDtypeStruct((t, B_HEADS * LANES), BF16),
            jax.ShapeDtypeStruct((bsz, B_WIDTH, seq), BF16),
        ],
        compiler_params=pltpu.CompilerParams(
            dimension_semantics=("arbitrary",), vmem_limit_bytes=VMEM_LIMIT),
        name="mla_prep",
    )(lat, pos3, invf, g_q, g_kv, wqt_p, wk_p, wvt_p)


NEG_BIG = -1e30
ATT_TK = 512
ATT_TQ = 4096
ATT_CB = 256


def _attn_kernel(qt_ref, k_ref, vt_ref, sb_ref, out_ref, s_ref):
    tq = qt_ref.shape[2]
    tk, cb = ATT_TK, ATT_CB
    i = pl.program_id(2)
    q_t = (qt_ref[0, :LANES, :], qt_ref[0, LANES:, :])
    ones_rows = jnp.ones((16, tk), BF16)

    def units(q0):
        return [(h, c) for h in range(2) for c in range(q0, tq, cb)]

    def score_unit(j, h, c):
        start = pl.multiple_of(j * tk, tk)
        return jnp.dot(k_ref[0, pl.ds(start, tk), h * LANES:(h + 1) * LANES], q_t[h][:, c:c + cb],
                       preferred_element_type=F32)

    def tile(j, carry, get_s, prefetch, q0=0):
        start = pl.multiple_of(j * tk, tk)
        vb = vt_ref[0, :, pl.ds(start, tk)]
        v_aug = [jnp.concatenate([vb[h * V_HEAD_DIM:(h + 1) * V_HEAD_DIM, :], ones_rows], axis=0)
                 for h in range(2)]
        cols = [[tuple(x[:, :q0] for x in carry[h])] if q0 else [] for h in range(2)]
        for u, (h, c) in enumerate(units(q0)):
            prefetch(u)
            s_u = get_s(u, h, c)
            m, l, acc = [x[:, c:c + cb] for x in carry[h]]
            m_new = jnp.maximum(m, jnp.max(s_u, axis=0, keepdims=True))
            alpha = jnp.exp2(m - m_new)
            p_u = jnp.exp2(s_u - m_new).astype(BF16)
            pv = jnp.dot(v_aug[h], p_u, preferred_element_type=F32)
            cols[h].append((m_new, l * alpha + pv[V_HEAD_DIM:V_HEAD_DIM + 1],
                            acc * alpha + pv[:V_HEAD_DIM]))
        return tuple(tuple(jnp.concatenate(parts, axis=1) if len(parts) > 1 else parts[0]
                           for parts in zip(*cols[h])) for h in range(2))

    init = tuple((jnp.full((1, tq), NEG_BIG, F32), jnp.zeros((1, tq), F32),
                  jnp.zeros((V_HEAD_DIM, tq), F32)) for _ in range(2))
    n_full = i * (tq // tk)
    all_units = units(0)

    def visible_tile(j, slot, carry):
        def prefetch(u):
            h, c = all_units[u]
            s_ref[1 - slot, u] = score_unit(j + 1, h, c)
        return tile(j, carry, lambda u, h, c: s_ref[slot, u], prefetch)

    def two_tiles(jj, carry):
        return visible_tile(2 * jj + 1, 1, visible_tile(2 * jj, 0, carry))

    assert (tq // tk) % 2 == 0
    for u, (h, c) in enumerate(all_units):
        s_ref[0, u] = score_unit(0, h, c)
    carry = lax.fori_loop(0, n_full // 2, two_tiles, init)

    key = lax.broadcasted_iota(jnp.int32, (tk, cb), 0)
    query = lax.broadcasted_iota(jnp.int32, (tk, cb), 1)
    ahead = {}
    for d in range(tq // tk):
        q0 = d * tk
        here, ahead = ahead, {}
        nxt = units(q0 + tk) if d + 1 < tq // tk else []

        def prefetch(u, d=d, nxt=nxt, ahead=ahead):
            if u < len(nxt):
                ahead[nxt[u]] = score_unit(n_full + d + 1, *nxt[u])

        def get_s(u, h, c, d=d, q0=q0, here=here):
            s_u = s_ref[0, u] if d == 0 else here[h, c]
            on_diagonal = q0 <= c < q0 + tk
            return jnp.where(key <= query + (c - q0), s_u, NEG_BIG) if on_diagonal else s_u

        carry = tile(n_full + d, carry, get_s, prefetch, q0)
    o_t = jnp.concatenate([acc / l for (_, l, acc) in carry], axis=0)
    out_ref[0] = (o_t.T * sb_ref[0].astype(F32)).astype(out_ref.dtype)


def _attention(q_t, k, v_t, zb, tq):
    bsz, seq, _ = k.shape
    return pl.pallas_call(
        _attn_kernel,
        grid=(bsz, B_HEADS // 2, seq // tq),
        in_specs=[
            pl.BlockSpec((1, 2 * LANES, tq), lambda b, p, i: (b, p, i)),
            pl.BlockSpec((1, seq, 2 * LANES), lambda b, p, i: (b, 0, p)),
            pl.BlockSpec((1, 2 * V_HEAD_DIM, seq), lambda b, p, i: (b, p, 0)),
            pl.BlockSpec((1, tq, LANES), lambda b, p, i: (b, i, p)),
        ],
        out_specs=pl.BlockSpec((1, tq, LANES), lambda b, p, i: (b, i, p)),
        out_shape=jax.ShapeDtypeStruct((bsz, seq, B_WIDTH), BF16),
        scratch_shapes=[pltpu.VMEM((2, 2 * (tq // ATT_CB), ATT_TK, ATT_CB), F32)],
        compiler_params=pltpu.CompilerParams(
            dimension_semantics=("arbitrary", "arbitrary", "arbitrary"),
            vmem_limit_bytes=VMEM_LIMIT),
        name="mla_attn",
    )(q_t, k, v_t, zb)


def _out_kernel(ya_ref, yb_ref, gate_ref, x_ref, woa_ref, wob_ref, wo_ref, gp_ref, out_ref):
    y_a = jnp.dot(ya_ref[...], woa_ref[...], preferred_element_type=F32)
    y_b = jnp.dot(yb_ref[...], wob_ref[...], preferred_element_type=F32)
    gates = gate_ref[...].astype(F32)
    merged = gates[:, :D_MODEL] * y_a + gates[:, D_MODEL:] * y_b
    o = _dot(merged, wo_ref[...])
    ms = jnp.mean(o * o, axis=-1, keepdims=True)
    out_ref[...] = x_ref[...] + o * lax.rsqrt(ms + NORM_EPS) * gp_ref[...]


def _out_proj(ya, yb, gates, x2, woa, wob, wo, g_post, tm):
    t = x2.shape[0]
    const = lambda shape: pl.BlockSpec(shape, lambda i: (0,) * len(shape))
    rows = lambda n: pl.BlockSpec((tm, n), lambda i: (i, 0))
    return pl.pallas_call(
        _out_kernel,
        grid=(t // tm,),
        in_specs=[
            rows(A_WIDTH), rows(B_WIDTH), rows(2 * D_MODEL), rows(D_MODEL),
            const((A_WIDTH, D_MODEL)), const((B_WIDTH, D_MODEL)),
            const((D_MODEL, D_MODEL)), const((1, D_MODEL)),
        ],
        out_specs=rows(D_MODEL),
        out_shape=jax.ShapeDtypeStruct((t, D_MODEL), F32),
        compiler_params=pltpu.CompilerParams(
            dimension_semantics=("arbitrary",), vmem_limit_bytes=VMEM_LIMIT),
        name="out_proj",
    )(ya, yb, gates, x2, woa, wob, wo, g_post)


def _rope_partner(w_pe):
    half = QK_ROPE_DIM // 2
    return jnp.concatenate([-w_pe[..., half:], w_pe[..., :half]], axis=-1)


def _prep_weights(w_in, w_decay_up, w_iclr_up, w_uq, w_ukv):
    o = np.cumsum([0, SHIFT_COLS, A_WIDTH, Q_LORA_RANK, KV_LORA_RANK, QK_ROPE_DIM, B_WIDTH])
    w_pe = w_in[:, o[4]:o[5]]
    kpe_blk = jnp.concatenate(
        [jnp.zeros((D_MODEL, QK_NOPE_DIM), F32), w_pe, _rope_partner(w_pe)], axis=-1)
    w_bf = w_in.astype(BF16)
    w_parts = (jnp.concatenate([w_bf[:, :o[4]], kpe_blk.astype(BF16)], axis=-1), w_bf[:, o[5]:])
    zeros_lora = jnp.zeros((DECAY_LORA, A_WIDTH), F32)
    wd_p = jnp.concatenate([w_decay_up, zeros_lora], axis=0).astype(BF16)
    wa_p = jnp.concatenate([zeros_lora, w_iclr_up], axis=0).astype(BF16)
    wq = w_uq.reshape(Q_LORA_RANK, B_HEADS, QK_NOPE_DIM + QK_ROPE_DIM)
    wqt_p = jnp.concatenate(
        [wq, _rope_partner(wq[..., QK_NOPE_DIM:])], axis=-1).reshape(Q_LORA_RANK, -1).T.astype(BF16)
    wkv = w_ukv.reshape(KV_LORA_RANK, B_HEADS, QK_NOPE_DIM + V_HEAD_DIM)
    wk_p = jnp.concatenate(
        [wkv[..., :QK_NOPE_DIM], jnp.zeros_like(wkv[..., :QK_NOPE_DIM])],
        axis=-1).reshape(KV_LORA_RANK, -1).astype(BF16)
    wvt_p = wkv[..., QK_NOPE_DIM:].reshape(KV_LORA_RANK, -1).T.astype(BF16)
    return w_parts, wd_p, wa_p, wqt_p, wk_p, wvt_p


def kernel(x, positions, g_pre, w_in, b_gate, mu_shift, w0, w_decay_up, a0, w_iclr_up, k_k, k_a,
           r_k, gn_gain, gn_bias, w_out_a, g_q, w_uq, g_kv, w_ukv, w_out_b, w_o, g_post):
    bsz, seq, _ = x.shape
    t = bsz * seq
    tm = TM_ROWS
    row = lambda a: a.reshape(1, -1).astype(F32)
    w_parts, wd_p, wa_p, wqt_p, wk_p, wvt_p = _prep_weights(w_in, w_decay_up, w_iclr_up, w_uq, w_ukv)
    x2 = x.reshape(t, D_MODEL)

    assert TM_INPROJ == RWKV_TT
    rwkv_ops, decay, silu_a, lat, silu_b, gates = _inproj(
        x2, row(g_pre), w_parts, row(b_gate), row(mu_shift),
        [row(w0), row(a0), row(k_k), row(k_a), row(r_k)], wd_p, wa_p, seq, TM_INPROJ)

    ya = _rwkv(rwkv_ops, decay, silu_a, row(gn_gain), row(gn_bias), bsz, seq, RWKV_TT)

    freq = ROPE_THETA ** (-(np.arange(QK_ROPE_DIM) % (QK_ROPE_DIM // 2)) * 2.0 / QK_ROPE_DIM)
    invf = jnp.asarray(freq.reshape(QK_ROPE_DIM, 1), F32)
    pos3 = positions.reshape(t // tm, 1, tm)
    q_t, k, v_t = _mla_prep(lat, pos3, invf, g_q.reshape(-1, 1).astype(F32), row(g_kv),
                            wqt_p, wk_p, wvt_p, bsz, seq, tm)
    yb = _attention(q_t, k.reshape(bsz, seq, -1), v_t, silu_b.reshape(bsz, seq, -1), ATT_TQ)

    out = _out_proj(ya, yb.reshape(t, B_WIDTH), gates, x2, w_out_a.astype(BF16),
                    w_out_b.astype(BF16), w_o.astype(BF16), row(g_post), tm)
    return out.reshape(bsz, seq, D_MODEL)
```

```python
import functools
import math

import jax
import jax.numpy as jnp
import numpy as np
from jax import lax
from jax.experimental import pallas as pl
from jax.experimental.pallas import tpu as pltpu

D_MODEL = 1024
A_HEADS = 8
A_HEAD_DIM = 64
A_WIDTH = A_HEADS * A_HEAD_DIM
DECAY_LORA = 64
ICLR_LORA = 64
DECAY_SCALE = 0.6065306597
GN_EPS = 64e-5
B_HEADS = 8
QK_NOPE_DIM = 64
QK_ROPE_DIM = 32
V_HEAD_DIM = 64
Q_LORA_RANK = 256
KV_LORA_RANK = 128
B_WIDTH = B_HEADS * V_HEAD_DIM
ROPE_THETA = 10000.0
NORM_EPS = 1e-6
SHIFT_COLS = 3 * A_WIDTH + DECAY_LORA + ICLR_LORA

LANES = 128
PAIR = 2 * A_HEAD_DIM
N_PAIRS = A_HEADS // 2
CHUNK = 64
LAT_COLS = 512
PROJ_COLS = SHIFT_COLS + A_WIDTH + LAT_COLS + B_WIDTH + 2 * D_MODEL
LO_COLS = SHIFT_COLS + A_WIDTH + LAT_COLS
VMEM_LIMIT = 52 * 1024 * 1024
TM_INPROJ = 256
TM_ROWS = 1024
RWKV_TT = 256

F32 = jnp.float32
BF16 = jnp.bfloat16
F32_TILE_ROWS = 8
BF16_TILE_ROWS = 16


def _dot(a, b):
    return jnp.dot(a.astype(BF16), b.astype(BF16), preferred_element_type=F32)


def _dot_nt(a, b):
    return lax.dot_general(a.astype(BF16), b.astype(BF16), (((1,), (1,)), ((), ())),
                           preferred_element_type=F32)


def _sigmoid(x):
    return 0.5 * jnp.tanh(0.5 * x) + 0.5


RWKV_SECTIONS = 8


def _inproj_kernel(tiles_per_seq, x_ref, g_ref, wlo_ref, whi_ref, bg_ref, mu_ref, w0_ref, a0_ref, kk_ref, ka_ref,
                   rk_ref, wd_ref, wa_ref,
                   ops_ref, decay_ref, sa_ref, lat_ref, sb_ref, gate_ref, last_ref):
    @pl.when(pl.program_id(0) % tiles_per_seq == 0)
    def _():
        last_ref[...] = jnp.zeros_like(last_ref)

    x = x_ref[...]
    tm = x.shape[0]
    ms = jnp.mean(x * x, axis=-1, keepdims=True)
    u = (x * lax.rsqrt(ms + NORM_EPS) * g_ref[...]).astype(BF16)
    silu = lambda z: z * _sigmoid(z)
    first_row = lax.broadcasted_iota(jnp.int32, (tm, 1), 0) == 0

    def token_shift(feat):
        prev = jnp.where(first_row, last_ref[0:1, :], pltpu.roll(feat, 1, 0))
        last_ref[0:1, :] = feat[tm - 1:tm, :]
        return feat + mu_ref[...] * (prev - feat)

    first_head = lax.broadcasted_iota(jnp.int32, (1, LANES), 1) < A_HEAD_DIM
    pos_in_chunk = lax.broadcasted_iota(jnp.int32, (tm, 1), 0) % CHUNK

    def head_sum(z):
        out = []
        for p in range(N_PAIRS):
            blk = z[:, p * LANES:(p + 1) * LANES]
            s_a = jnp.sum(jnp.where(first_head, blk, 0.0), axis=1, keepdims=True)
            s_b = jnp.sum(jnp.where(first_head, 0.0, blk), axis=1, keepdims=True)
            out.append(jnp.where(first_head, s_a, s_b))
        return jnp.concatenate(out, axis=1)

    tok_r = lax.broadcasted_iota(jnp.int32, (tm, tm), 0)
    tok_c = lax.broadcasted_iota(jnp.int32, (tm, tm), 1)
    prefix = ((tok_c <= tok_r) & (tok_c // CHUNK == tok_r // CHUNK)).astype(BF16)

    def chunk_cumsum(z):
        hi = z.astype(BF16)
        lo = (z - hi.astype(F32)).astype(BF16)
        return (jnp.dot(prefix, hi, preferred_element_type=F32)
                + jnp.dot(prefix, lo, preferred_element_type=F32))

    def rwkv_operands(feat):
        f = token_shift(feat)
        r, k, v = (f[:, j * A_WIDTH:(j + 1) * A_WIDTH] for j in range(3))
        la = f[:, 3 * A_WIDTH:]
        w_logit = w0_ref[...] + jnp.dot(jnp.tanh(la).astype(BF16), wd_ref[...],
                                        preferred_element_type=F32)
        a_logit = a0_ref[...] + jnp.dot(la.astype(BF16), wa_ref[...], preferred_element_type=F32)
        yield
        a = _sigmoid(a_logit)
        lw = -DECAY_SCALE * _sigmoid(w_logit)
        kk = k * kk_ref[...]
        k2 = k * (1.0 + (a - 1.0) * ka_ref[...])
        kk = kk / jnp.maximum(jnp.sqrt(head_sum(kk * kk)), 1e-12)
        yield
        bonus = head_sum(r * k2 * rk_ref[...]) * v
        b = kk * a
        yield
        cum = chunk_cumsum(lw)
        ends = [cum[c + CHUNK - 1:c + CHUNK, :] for c in range(0, tm, CHUNK)]
        last = jnp.concatenate([jnp.broadcast_to(e, (CHUNK, A_WIDTH)) for e in ends], axis=0)
        yield
        p_inv = jnp.exp(-cum)
        to_end = jnp.exp(last - cum)
        sections = (lambda: -kk * jnp.exp(cum - lw), lambda: r * jnp.exp(cum), lambda: b * p_inv,
                    lambda: k2 * p_inv, lambda: b * to_end, lambda: k2 * to_end, lambda: v,
                    lambda: bonus)
        assert len(sections) == RWKV_SECTIONS
        for j, sec in enumerate(sections):
            ops_ref[:, j * A_WIDTH:(j + 1) * A_WIDTH] = sec().astype(ops_ref.dtype)
            if j % 3 == 2:
                yield
        pad = jnp.zeros((decay_ref.shape[0] - len(ends), A_WIDTH), F32)
        decay_ref[...] = jnp.concatenate([jnp.exp(e) for e in ends] + [pad], axis=0)

    def store_silu(ref):
        def store(y, c):
            ref[...] = silu(y).astype(ref.dtype)
        return store

    def store_latents(y, c):
        lat_ref[...] = y

    def store_gates(y, c):
        gate_ref[:, c:c + A_WIDTH] = _sigmoid(y + bg_ref[:, c:c + A_WIDTH]).astype(gate_ref.dtype)

    assert A_WIDTH == LAT_COLS == B_WIDTH
    gate_off = SHIFT_COLS + A_WIDTH + LAT_COLS + B_WIDTH
    jobs = [(SHIFT_COLS, store_silu(sa_ref), 0),
            (SHIFT_COLS + A_WIDTH, store_latents, 0),
            (SHIFT_COLS + A_WIDTH + LAT_COLS, store_silu(sb_ref), 0)]
    jobs += [(gate_off + c, store_gates, c) for c in range(0, 2 * D_MODEL, A_WIDTH)]
    pieces = ((0, LO_COLS, wlo_ref), (LO_COLS, PROJ_COLS, whi_ref))

    def project(off, n):
        outs = [jnp.dot(u, ref[:, max(off, lo) - lo:min(off + n, hi) - lo],
                        preferred_element_type=F32)
                for lo, hi, ref in pieces if max(off, lo) < min(off + n, hi)]
        return outs[0] if len(outs) == 1 else jnp.concatenate(outs, axis=1)

    phases = rwkv_operands(project(0, SHIFT_COLS))
    for off, store, c in jobs:
        next(phases, None)
        store(project(off, A_WIDTH), c)
    for _ in phases:
        pass


def _inproj(x2, g_pre, w_parts, b_gate, mu, rwkv_rows, wd_p, wa_p, seq, tm):
    t = x2.shape[0]
    assert tm % CHUNK == 0 and tm // CHUNK <= F32_TILE_ROWS
    const = lambda shape: pl.BlockSpec(shape, lambda i: (0,) * len(shape))
    outs = ((A_WIDTH, BF16), (LAT_COLS, F32), (B_WIDTH, BF16), (2 * D_MODEL, BF16))
    return pl.pallas_call(
        functools.partial(_inproj_kernel, seq // tm),
        grid=(t // tm,),
        in_specs=[
            pl.BlockSpec((tm, D_MODEL), lambda i: (i, 0)),
            const((1, D_MODEL))] + [const(w.shape) for w in w_parts] + [const((1, 2 * D_MODEL)),
            const((1, SHIFT_COLS))] + [const((1, A_WIDTH))] * 5 + [const((LANES, A_WIDTH))] * 2,
        out_specs=[pl.BlockSpec((tm, RWKV_SECTIONS * A_WIDTH), lambda i: (i, 0)),
                   pl.BlockSpec((F32_TILE_ROWS, A_WIDTH), lambda i: (i, 0))]
        + [pl.BlockSpec((tm, n), lambda i: (i, 0)) for n, _ in outs],
        out_shape=[jax.ShapeDtypeStruct((t, RWKV_SECTIONS * A_WIDTH), BF16),
                   jax.ShapeDtypeStruct((t // tm * F32_TILE_ROWS, A_WIDTH), F32)]
        + [jax.ShapeDtypeStruct((t, n), dt) for n, dt in outs],
        scratch_shapes=[pltpu.VMEM((8, SHIFT_COLS), F32)],
        compiler_params=pltpu.CompilerParams(
            dimension_semantics=("arbitrary",), vmem_limit_bytes=VMEM_LIMIT),
        name="inproj",
    )(x2, g_pre, *w_parts, b_gate, mu, *rwkv_rows, wd_p, wa_p)


def _split_bf16(x):
    hi = x.astype(BF16)
    return hi, (x - hi.astype(F32)).astype(BF16)


def _rwkv_kernel(ops_ref, decay_ref, sa_ref, gg_ref, gb_ref, out_ref, state_ref):
    tt = ops_ref.shape[0]
    n_chunks = tt // CHUNK

    @pl.when(pl.program_id(1) == 0)
    def _():
        state_ref[...] = jnp.zeros_like(state_ref)

    ri = lax.broadcasted_iota(jnp.int32, (LANES, LANES), 0)
    ci = lax.broadcasted_iota(jnp.int32, (LANES, LANES), 1)
    same_head = (ri // A_HEAD_DIM) == (ci // A_HEAD_DIM)
    strict_bd = same_head & (ci < ri)
    incl_bd = same_head & (ci <= ri)
    eye = (ri == ci).astype(F32)
    ones2 = jnp.concatenate([same_head, same_head], axis=0).astype(BF16)
    zeros_sq = jnp.zeros((LANES, LANES), F32)
    lane = lax.broadcasted_iota(jnp.int32, (1, LANES), 1)
    m_a = (lane < A_HEAD_DIM).astype(BF16)
    m_b = 1.0 - m_a

    def head_sum(x):
        hi, lo = _split_bf16(x)
        return jnp.dot(jnp.concatenate([hi, lo], axis=1), ones2, preferred_element_type=F32)

    def stack(z):
        return jnp.concatenate([z * m_a, z * m_b], axis=0)

    def unstack(z):
        return z[:CHUNK] + z[CHUNK:]

    pairs = range(N_PAIRS)
    items = [(p, ch) for p in pairs for ch in range(n_chunks)]
    each = lambda fn: {it: fn(it) for it in items}
    pcols = lambda p: slice(p * LANES, (p + 1) * LANES)

    def chunk_operands(it):
        p, ch = it
        sec = lambda j: ops_ref[ch * CHUNK:(ch + 1) * CHUNK,
                                j * A_WIDTH + p * LANES:j * A_WIDTH + (p + 1) * LANES]
        ends_t = [stack(sec(j)).astype(F32).T for j in (4, 5)]
        return dict(cts=stack(sec(0)), rt=sec(1), bt=sec(2), kt=sec(3), vs=stack(sec(6)),
                    decay=eye * decay_ref[ch:ch + 1, pcols(p)],
                    ends_t=jnp.concatenate(ends_t, axis=1))

    ops = each(chunk_operands)
    g = each(lambda it: _dot_nt(
        jnp.concatenate([ops[it]["cts"], stack(ops[it]["rt"])], axis=0),
        jnp.concatenate([ops[it]["bt"], ops[it]["bt"], ops[it]["kt"], ops[it]["kt"]], axis=0)))
    nb = each(lambda it: jnp.where(strict_bd, g[it][:PAIR, :PAIR], 0.0))
    nk = each(lambda it: jnp.where(strict_bd, g[it][:PAIR, PAIR:], 0.0))
    m_bk = each(lambda it: jnp.concatenate(
        [jnp.where(incl_bd, g[it][PAIR:, :PAIR], 0.0),
         jnp.where(incl_bd, g[it][PAIR:, PAIR:], 0.0)], axis=1))
    t_inv = each(lambda it: eye + nb[it])
    npow = each(lambda it: _dot(nb[it], nb[it]))
    nkv = each(lambda it: _dot(nk[it], ops[it]["vs"]))
    for _ in range(int(math.log2(CHUNK)) - 2):
        both = each(lambda it: _dot(npow[it], jnp.concatenate([npow[it], t_inv[it]], axis=1)))
        npow = each(lambda it: both[it][:, :PAIR])
        t_inv = each(lambda it: t_inv[it] + both[it][:, PAIR:])
    last_term = each(lambda it: _dot(npow[it], t_inv[it]))
    t_inv = each(lambda it: t_inv[it] + last_term[it])
    w12 = each(lambda it: _dot(t_inv[it], jnp.concatenate([ops[it]["cts"], nkv[it]], axis=1)))
    big = each(lambda it: _dot(
        jnp.concatenate([ops[it]["ends_t"], m_bk[it]], axis=0),
        jnp.concatenate([w12[it], jnp.concatenate([zeros_sq, ops[it]["vs"]], axis=1)], axis=0)))
    gt = each(lambda it: (big[it][:PAIR, :PAIR] + ops[it]["decay"]).astype(BF16))
    q1 = each(lambda it: ops[it]["rt"].astype(F32) + unstack(big[it][PAIR:, :PAIR]))

    h = {p: state_ref[p] for p in pairs}
    h_at = {}
    for ch in range(n_chunks):
        for p in pairs:
            it = (p, ch)
            h_at[it] = h[p]
            h_hi, h_lo = _split_bf16(h[p])
            h[p] = jnp.dot(jnp.concatenate([gt[it], gt[it]], axis=1),
                           jnp.concatenate([h_hi, h_lo], axis=0),
                           preferred_element_type=F32) + big[it][:PAIR, PAIR:]
    for p in pairs:
        state_ref[p] = h[p]
    y_it = each(lambda it: _dot(q1[it], h_at[it]) + unstack(big[it][PAIR:, PAIR:]))

    y = {p: jnp.concatenate([y_it[(p, ch)] for ch in range(n_chunks)], axis=0) for p in pairs}
    mean = {p: head_sum(y[p]) * (1.0 / A_HEAD_DIM) for p in pairs}
    d = {p: y[p] - mean[p] for p in pairs}
    var = {p: head_sum(d[p] * d[p]) * (1.0 / A_HEAD_DIM) for p in pairs}
    for p in pairs:
        bonus = ops_ref[:, 7 * A_WIDTH + p * LANES:7 * A_WIDTH + (p + 1) * LANES].astype(F32)
        y_n = d[p] * lax.rsqrt(var[p] + GN_EPS) * gg_ref[:, pcols(p)] + gb_ref[:, pcols(p)] + bonus
        out_ref[:, pcols(p)] = (y_n * sa_ref[:, pcols(p)].astype(F32)).astype(out_ref.dtype)


def _rwkv(ops, decay, silu_a, gn_gain, gn_bias, bsz, seq, tt):
    nt = seq // tt
    const = lambda shape: pl.BlockSpec(shape, lambda b, t: (0,) * len(shape))
    rows = lambda r, n: pl.BlockSpec((r, n), lambda b, t: (b * nt + t, 0))
    return pl.pallas_call(
        _rwkv_kernel,
        grid=(bsz, nt),
        in_specs=[rows(tt, RWKV_SECTIONS * A_WIDTH), rows(F32_TILE_ROWS, A_WIDTH), rows(tt, A_WIDTH),
                  const((1, A_WIDTH)), const((1, A_WIDTH))],
        out_specs=rows(tt, A_WIDTH),
        out_shape=jax.ShapeDtypeStruct((bsz * seq, A_WIDTH), BF16),
        scratch_shapes=[pltpu.VMEM((N_PAIRS, LANES, LANES), F32)],
        compiler_params=pltpu.CompilerParams(
            dimension_semantics=("arbitrary", "arbitrary"), vmem_limit_bytes=VMEM_LIMIT),
        name="rwkv7",
    )(ops, decay, silu_a, gn_gain, gn_bias)


QK_SCALE = math.log2(math.e) / math.sqrt(QK_NOPE_DIM + QK_ROPE_DIM)


def _mla_prep_kernel(lat_ref, pos_ref, invf_ref, gq_ref, gkv_ref, wqt_ref, wk_ref, wvt_ref,
                     qt_ref, k_ref, vt_ref):
    lat = lat_ref[...]
    tm = lat.shape[0]
    pos = pos_ref[0].astype(F32)
    ang = invf_ref[...] * pos
    pad = jnp.zeros((LANES - QK_NOPE_DIM - QK_ROPE_DIM, tm), F32)
    cos_q = jnp.concatenate([jnp.ones((QK_NOPE_DIM, tm), F32), jnp.cos(ang), pad], axis=0)
    sin_t = jnp.concatenate([jnp.zeros((QK_NOPE_DIM, tm), F32), jnp.sin(ang), pad], axis=0)

    cq_t = jnp.concatenate([lat[:, j * LANES:(j + 1) * LANES].T
                            for j in range(Q_LORA_RANK // LANES)], axis=0)
    nq_t = cq_t * lax.rsqrt(jnp.mean(cq_t * cq_t, axis=0, keepdims=True) + NORM_EPS) * gq_ref[...]
    q_raw = _dot(wqt_ref[...], nq_t)
    for h in range(B_HEADS):
        sl = slice(h * LANES, (h + 1) * LANES)
        blk = q_raw[sl]
        rot = blk * cos_q + pltpu.roll(blk, LANES - QK_ROPE_DIM, 0) * sin_t
        qt_ref[0, sl, :] = (rot * QK_SCALE).astype(qt_ref.dtype)

    ckv = lat[:, Q_LORA_RANK:Q_LORA_RANK + KV_LORA_RANK]
    nkv = ckv * lax.rsqrt(jnp.mean(ckv * ckv, axis=-1, keepdims=True) + NORM_EPS) * gkv_ref[...]
    vt_ref[0] = _dot(wvt_ref[...], nkv.T).astype(vt_ref.dtype)
    k_nope = _dot(nkv, wk_ref[...])
    lane = lax.broadcasted_iota(jnp.int32, (1, LANES), 1)
    rope_lane = (lane >= QK_NOPE_DIM) & (lane < QK_NOPE_DIM + QK_ROPE_DIM)
    pe = lat[:, Q_LORA_RANK + KV_LORA_RANK:]
    k_pe = pe * jnp.where(rope_lane, cos_q.T, 0.0) + pltpu.roll(pe, LANES - QK_ROPE_DIM, 1) * sin_t.T
    for h in range(B_HEADS):
        sl = slice(h * LANES, (h + 1) * LANES)
        k_ref[:, sl] = (k_nope[:, sl] + k_pe).astype(k_ref.dtype)


def _mla_prep(lat, pos3, invf, g_q, g_kv, wqt_p, wk_p, wvt_p, bsz, seq, tm):
    t = lat.shape[0]
    nt = seq // tm
    const = lambda shape: pl.BlockSpec(shape, lambda i: (0,) * len(shape))
    return pl.pallas_call(
        _mla_prep_kernel,
        grid=(t // tm,),
        in_specs=[
            pl.BlockSpec((tm, LAT_COLS), lambda i: (i, 0)),
            pl.BlockSpec((1, 1, tm), lambda i: (i, 0, 0)),
            const((QK_ROPE_DIM, 1)), const((Q_LORA_RANK, 1)), const((1, KV_LORA_RANK)),
            const((B_HEADS * LANES, Q_LORA_RANK)), const((KV_LORA_RANK, B_HEADS * LANES)),
            const((B_WIDTH, KV_LORA_RANK)),
        ],
        out_specs=[
            pl.BlockSpec((1, B_HEADS * LANES, tm), lambda i: (i // nt, 0, i % nt)),
            pl.BlockSpec((tm, B_HEADS * LANES), lambda i: (i, 0)),
            pl.BlockSpec((1, B_WIDTH, tm), lambda i: (i // nt, 0, i % nt)),
        ],
        out_shape=[
            jax.ShapeDtypeStruct((bsz, B_HEADS * LANES, seq), BF16),
            jax.ShapeDtypeStruct((t, B_HEADS * LANES), BF16),
            jax.ShapeDtypeStruct((bsz, B_WIDTH, seq), BF16),
        ],
        compiler_params=pltpu.CompilerParams(
            dimension_semantics=("arbitrary",), vmem_limit_bytes=VMEM_LIMIT),
        name="mla_prep",
    )(lat, pos3, invf, g_q, g_kv, wqt_p, wk_p, wvt_p)


NEG_BIG = -1e30
ATT_TK = 512
ATT_TQ = 2048
ATT_CB = 256


def _attn_kernel(qt_ref, k_ref, vt_ref, sb_ref, out_ref, s_ref):
    tq = qt_ref.shape[2]
    tk, cb = ATT_TK, ATT_CB
    i = pl.program_id(2)
    q_t = (qt_ref[0, :LANES, :], qt_ref[0, LANES:, :])
    ones_rows = jnp.ones((BF16_TILE_ROWS, tk), BF16)

    def units(q0):
        return [(h, c) for h in range(2) for c in range(q0, tq, cb)]

    def score_unit(j, h, c):
        start = pl.multiple_of(j * tk, tk)
        return jnp.dot(k_ref[0, pl.ds(start, tk), h * LANES:(h + 1) * LANES], q_t[h][:, c:c + cb],
                       preferred_element_type=F32)

    def tile(j, carry, get_s, prefetch, q0=0):
        start = pl.multiple_of(j * tk, tk)
        vb = vt_ref[0, :, pl.ds(start, tk)]
        v_aug = [jnp.concatenate([vb[h * V_HEAD_DIM:(h + 1) * V_HEAD_DIM, :], ones_rows], axis=0)
                 for h in range(2)]
        cols = [[tuple(x[:, :q0] for x in carry[h])] if q0 else [] for h in range(2)]
        for u, (h, c) in enumerate(units(q0)):
            prefetch(u)
            s_u = get_s(u, h, c)
            m, l, acc = [x[:, c:c + cb] for x in carry[h]]
            m_new = jnp.maximum(m, jnp.max(s_u, axis=0, keepdims=True))
            alpha = jnp.exp2(m - m_new)
            p_u = jnp.exp2(s_u - m_new).astype(BF16)
            pv = jnp.dot(v_aug[h], p_u, preferred_element_type=F32)
            cols[h].append((m_new, l * alpha + pv[V_HEAD_DIM:V_HEAD_DIM + 1],
                            acc * alpha + pv[:V_HEAD_DIM]))
        return tuple(tuple(jnp.concatenate(parts, axis=1) if len(parts) > 1 else parts[0]
                           for parts in zip(*cols[h])) for h in range(2))

    init = tuple((jnp.full((1, tq), NEG_BIG, F32), jnp.zeros((1, tq), F32),
                  jnp.zeros((V_HEAD_DIM, tq), F32)) for _ in range(2))
    n_full = i * (tq // tk)
    all_units = units(0)

    def visible_tile(j, slot, carry):
        def prefetch(u):
            h, c = all_units[u]
            s_ref[1 - slot, u] = score_unit(j + 1, h, c)
        return tile(j, carry, lambda u, h, c: s_ref[slot, u], prefetch)

    def two_tiles(jj, carry):
        return visible_tile(2 * jj + 1, 1, visible_tile(2 * jj, 0, carry))

    assert (tq // tk) % 2 == 0
    for u, (h, c) in enumerate(all_units):
        s_ref[0, u] = score_unit(0, h, c)
    carry = lax.fori_loop(0, n_full // 2, two_tiles, init)

    key = lax.broadcasted_iota(jnp.int32, (tk, cb), 0)
    query = lax.broadcasted_iota(jnp.int32, (tk, cb), 1)
    ahead = {}
    for d in range(tq // tk):
        q0 = d * tk
        here, ahead = ahead, {}
        nxt = units(q0 + tk) if d + 1 < tq // tk else []

        def prefetch(u, d=d, nxt=nxt, ahead=ahead):
            if u < len(nxt):
                ahead[nxt[u]] = score_unit(n_full + d + 1, *nxt[u])

        def get_s(u, h, c, d=d, q0=q0, here=here):
            s_u = s_ref[0, u] if d == 0 else here[h, c]
            on_diagonal = q0 <= c < q0 + tk
            return jnp.where(key <= query + (c - q0), s_u, NEG_BIG) if on_diagonal else s_u

        carry = tile(n_full + d, carry, get_s, prefetch, q0)
    o_t = jnp.concatenate([acc / l for (_, l, acc) in carry], axis=0)
    out_ref[0] = (o_t.T * sb_ref[0].astype(F32)).astype(out_ref.dtype)


def _attention(q_t, k, v_t, silu_b, tq):
    bsz, seq, _ = k.shape
    return pl.pallas_call(
        _attn_kernel,
        grid=(bsz, B_HEADS // 2, seq // tq),
        in_specs=[
            pl.BlockSpec((1, 2 * LANES, tq), lambda b, p, i: (b, p, i)),
            pl.BlockSpec((1, seq, 2 * LANES), lambda b, p, i: (b, 0, p)),
            pl.BlockSpec((1, 2 * V_HEAD_DIM, seq), lambda b, p, i: (b, p, 0)),
            pl.BlockSpec((1, tq, LANES), lambda b, p, i: (b, i, p)),
        ],
        out_specs=pl.BlockSpec((1, tq, LANES), lambda b, p, i: (b, i, p)),
        out_shape=jax.ShapeDtypeStruct((bsz, seq, B_WIDTH), BF16),
        scratch_shapes=[pltpu.VMEM((2, 2 * (tq // ATT_CB), ATT_TK, ATT_CB), F32)],
        compiler_params=pltpu.CompilerParams(
            dimension_semantics=("arbitrary", "arbitrary", "arbitrary"),
            vmem_limit_bytes=VMEM_LIMIT),
        name="mla_attn",
    )(q_t, k, v_t, silu_b)


def _out_kernel(ya_ref, yb_ref, gate_ref, x_ref, woa_ref, wob_ref, wo_ref, gp_ref, out_ref):
    y_a = jnp.dot(ya_ref[...], woa_ref[...], preferred_element_type=F32)
    y_b = jnp.dot(yb_ref[...], wob_ref[...], preferred_element_type=F32)
    gates = gate_ref[...].astype(F32)
    merged = gates[:, :D_MODEL] * y_a + gates[:, D_MODEL:] * y_b
    o = _dot(merged, wo_ref[...])
    ms = jnp.mean(o * o, axis=-1, keepdims=True)
    out_ref[...] = x_ref[...] + o * lax.rsqrt(ms + NORM_EPS) * gp_ref[...]


def _out_proj(ya, yb, gates, x2, woa, wob, wo, g_post, tm):
    t = x2.shape[0]
    const = lambda shape: pl.BlockSpec(shape, lambda i: (0,) * len(shape))
    rows = lambda n: pl.BlockSpec((tm, n), lambda i: (i, 0))
    return pl.pallas_call(
        _out_kernel,
        grid=(t // tm,),
        in_specs=[
            rows(A_WIDTH), rows(B_WIDTH), rows(2 * D_MODEL), rows(D_MODEL),
            const((A_WIDTH, D_MODEL)), const((B_WIDTH, D_MODEL)),
            const((D_MODEL, D_MODEL)), const((1, D_MODEL)),
        ],
        out_specs=rows(D_MODEL),
        out_shape=jax.ShapeDtypeStruct((t, D_MODEL), F32),
        compiler_params=pltpu.CompilerParams(
            dimension_semantics=("arbitrary",), vmem_limit_bytes=VMEM_LIMIT),
        name="out_proj",
    )(ya, yb, gates, x2, woa, wob, wo, g_post)


def _rope_partner(w_pe):
    half = QK_ROPE_DIM // 2
    return jnp.concatenate([-w_pe[..., half:], w_pe[..., :half]], axis=-1)


def _prep_weights(w_in, w_decay_up, w_iclr_up, w_uq, w_ukv):
    o = np.cumsum([0, SHIFT_COLS, A_WIDTH, Q_LORA_RANK, KV_LORA_RANK, QK_ROPE_DIM, B_WIDTH])
    w_pe = w_in[:, o[4]:o[5]]
    kpe_blk = jnp.concatenate(
        [jnp.zeros((D_MODEL, QK_NOPE_DIM), F32), w_pe, _rope_partner(w_pe)], axis=-1)
    w_parts = (jnp.concatenate([w_in[:, :o[4]].astype(BF16), kpe_blk.astype(BF16)], axis=-1),
               w_in[:, o[5]:].astype(BF16))
    zeros_lora = jnp.zeros((DECAY_LORA, A_WIDTH), F32)
    wd_p = jnp.concatenate([w_decay_up, zeros_lora], axis=0).astype(BF16)
    wa_p = jnp.concatenate([zeros_lora, w_iclr_up], axis=0).astype(BF16)
    wq = w_uq.reshape(Q_LORA_RANK, B_HEADS, QK_NOPE_DIM + QK_ROPE_DIM)
    wqt_p = jnp.concatenate(
        [wq, _rope_partner(wq[..., QK_NOPE_DIM:])], axis=-1).reshape(Q_LORA_RANK, -1).T.astype(BF16)
    wkv = w_ukv.reshape(KV_LORA_RANK, B_HEADS, QK_NOPE_DIM + V_HEAD_DIM)
    wk_p = jnp.concatenate(
        [wkv[..., :QK_NOPE_DIM], jnp.zeros_like(wkv[..., :QK_NOPE_DIM])],
        axis=-1).reshape(KV_LORA_RANK, -1).astype(BF16)
    wvt_p = wkv[..., QK_NOPE_DIM:].reshape(KV_LORA_RANK, -1).T.astype(BF16)
    return w_parts, wd_p, wa_p, wqt_p, wk_p, wvt_p


def kernel(x, positions, g_pre, w_in, b_gate, mu_shift, w0, w_decay_up, a0, w_iclr_up, k_k, k_a,
           r_k, gn_gain, gn_bias, w_out_a, g_q, w_uq, g_kv, w_ukv, w_out_b, w_o, g_post):
    bsz, seq, _ = x.shape
    t = bsz * seq
    tm = TM_ROWS
    row = lambda a: a.reshape(1, -1).astype(F32)
    w_parts, wd_p, wa_p, wqt_p, wk_p, wvt_p = _prep_weights(w_in, w_decay_up, w_iclr_up, w_uq, w_ukv)
    x2 = x.reshape(t, D_MODEL)

    assert TM_INPROJ == RWKV_TT
    rwkv_ops, decay, silu_a, lat, silu_b, gates = _inproj(
        x2, row(g_pre), w_parts, row(b_gate), row(mu_shift),
        [row(w0), row(a0), row(k_k), row(k_a), row(r_k)], wd_p, wa_p, seq, TM_INPROJ)

    ya = _rwkv(rwkv_ops, decay, silu_a, row(gn_gain), row(gn_bias), bsz, seq, RWKV_TT)

    freq = ROPE_THETA ** (-(np.arange(QK_ROPE_DIM) % (QK_ROPE_DIM // 2)) * 2.0 / QK_ROPE_DIM)
    invf = jnp.asarray(freq.reshape(QK_ROPE_DIM, 1), F32)
    pos3 = positions.reshape(t // tm, 1, tm)
    q_t, k, v_t = _mla_prep(lat, pos3, invf, g_q.reshape(-1, 1).astype(F32), row(g_kv),
                            wqt_p, wk_p, wvt_p, bsz, seq, tm)
    yb = _attention(q_t, k.reshape(bsz, seq, -1), v_t, silu_b.reshape(bsz, seq, -1), ATT_TQ)

    out = _out_proj(ya, yb.reshape(t, B_WIDTH), gates, x2, w_out_a.astype(BF16),
                    w_out_b.astype(BF16), w_o.astype(BF16), row(g_post), tm)
    return out.reshape(bsz, seq, D_MODEL)
```

```python
import functools
import math

import jax
import jax.numpy as jnp
import numpy as np
from jax import lax
from jax.experimental import pallas as pl
from jax.experimental.pallas import tpu as pltpu

D_MODEL = 1024
A_HEADS = 8
A_HEAD_DIM = 64
A_WIDTH = A_HEADS * A_HEAD_DIM
DECAY_LORA = 64
ICLR_LORA = 64
DECAY_SCALE = 0.6065306597
GN_EPS = 64e-5
B_HEADS = 8
QK_NOPE_DIM = 64
QK_ROPE_DIM = 32
V_HEAD_DIM = 64
Q_LORA_RANK = 256
KV_LORA_RANK = 128
B_WIDTH = B_HEADS * V_HEAD_DIM
ROPE_THETA = 10000.0
NORM_EPS = 1e-6
SHIFT_COLS = 3 * A_WIDTH + DECAY_LORA + ICLR_LORA

LANES = 128
PAIR = 2 * A_HEAD_DIM
N_PAIRS = A_HEADS // 2
CHUNK = 64
LAT_COLS = 512
PROJ_COLS = SHIFT_COLS + A_WIDTH + LAT_COLS + B_WIDTH + 2 * D_MODEL
LO_COLS = SHIFT_COLS + A_WIDTH + LAT_COLS
VMEM_LIMIT = 52 * 1024 * 1024
TM_INPROJ = 256
TM_ROWS = 1024
RWKV_TT = 256

F32 = jnp.float32
BF16 = jnp.bfloat16
F32_TILE_ROWS = 8
BF16_TILE_ROWS = 16


def _dot(a, b):
    return jnp.dot(a.astype(BF16), b.astype(BF16), preferred_element_type=F32)


def _dot_nt(a, b):
    return lax.dot_general(a.astype(BF16), b.astype(BF16), (((1,), (1,)), ((), ())),
                           preferred_element_type=F32)


def _sigmoid(x):
    return 0.5 * jnp.tanh(0.5 * x) + 0.5


RWKV_SECTIONS = 8


def _inproj_kernel(tiles_per_seq, x_ref, g_ref, wlo_ref, whi_ref, bg_ref, mu_ref, w0_ref, a0_ref, kk_ref, ka_ref,
                   rk_ref, wd_ref, wa_ref,
                   ops_ref, decay_ref, sa_ref, lat_ref, sb_ref, gate_ref, last_ref):
    @pl.when(pl.program_id(0) % tiles_per_seq == 0)
    def _():
        last_ref[...] = jnp.zeros_like(last_ref)

    x = x_ref[...]
    tm = x.shape[0]
    ms = jnp.mean(x * x, axis=-1, keepdims=True)
    u = (x * lax.rsqrt(ms + NORM_EPS) * g_ref[...]).astype(BF16)
    silu = lambda z: z * _sigmoid(z)
    first_row = lax.broadcasted_iota(jnp.int32, (tm, 1), 0) == 0

    def token_shift(feat):
        prev = jnp.where(first_row, last_ref[0:1, :], pltpu.roll(feat, 1, 0))
        last_ref[0:1, :] = feat[tm - 1:tm, :]
        return feat + mu_ref[...] * (prev - feat)

    first_head = lax.broadcasted_iota(jnp.int32, (1, LANES), 1) < A_HEAD_DIM
    pos_in_chunk = lax.broadcasted_iota(jnp.int32, (tm, 1), 0) % CHUNK

    def head_sum(z):
        out = []
        for p in range(N_PAIRS):
            blk = z[:, p * LANES:(p + 1) * LANES]
            s_a = jnp.sum(jnp.where(first_head, blk, 0.0), axis=1, keepdims=True)
            s_b = jnp.sum(jnp.where(first_head, 0.0, blk), axis=1, keepdims=True)
            out.append(jnp.where(first_head, s_a, s_b))
        return jnp.concatenate(out, axis=1)

    tok_r = lax.broadcasted_iota(jnp.int32, (tm, tm), 0)
    tok_c = lax.broadcasted_iota(jnp.int32, (tm, tm), 1)
    prefix = ((tok_c <= tok_r) & (tok_c // CHUNK == tok_r // CHUNK)).astype(BF16)

    def chunk_cumsum(z):
        hi = z.astype(BF16)
        lo = (z - hi.astype(F32)).astype(BF16)
        return (jnp.dot(prefix, hi, preferred_element_type=F32)
                + jnp.dot(prefix, lo, preferred_element_type=F32))

    def rwkv_operands(feat):
        f = token_shift(feat)
        r, k, v = (f[:, j * A_WIDTH:(j + 1) * A_WIDTH] for j in range(3))
        la = f[:, 3 * A_WIDTH:]
        w_logit = w0_ref[...] + jnp.dot(jnp.tanh(la).astype(BF16), wd_ref[...],
                                        preferred_element_type=F32)
        a_logit = a0_ref[...] + jnp.dot(la.astype(BF16), wa_ref[...], preferred_element_type=F32)
        yield
        a = _sigmoid(a_logit)
        lw = -DECAY_SCALE * _sigmoid(w_logit)
        kk = k * kk_ref[...]
        k2 = k * (1.0 + (a - 1.0) * ka_ref[...])
        kk = kk / jnp.maximum(jnp.sqrt(head_sum(kk * kk)), 1e-12)
        yield
        bonus = head_sum(r * k2 * rk_ref[...]) * v
        b = kk * a
        yield
        cum = chunk_cumsum(lw)
        ends = [cum[c + CHUNK - 1:c + CHUNK, :] for c in range(0, tm, CHUNK)]
        last = jnp.concatenate([jnp.broadcast_to(e, (CHUNK, A_WIDTH)) for e in ends], axis=0)
        yield
        p_inv = jnp.exp(-cum)
        to_end = jnp.exp(last - cum)
        sections = (lambda: -kk * jnp.exp(cum - lw), lambda: r * jnp.exp(cum), lambda: b * p_inv,
                    lambda: k2 * p_inv, lambda: b * to_end, lambda: k2 * to_end, lambda: v,
                    lambda: bonus)
        assert len(sections) == RWKV_SECTIONS
        for j, sec in enumerate(sections):
            ops_ref[:, j * A_WIDTH:(j + 1) * A_WIDTH] = sec().astype(ops_ref.dtype)
            if j % 3 == 2:
                yield
        pad = jnp.zeros((decay_ref.shape[0] - len(ends), A_WIDTH), F32)
        decay_ref[...] = jnp.concatenate([jnp.exp(e) for e in ends] + [pad], axis=0)

    def store_silu(ref):
        def store(y, c):
            ref[...] = silu(y).astype(ref.dtype)
        return store

    def store_latents(y, c):
        lat_ref[...] = y

    def store_gates(y, c):
        gate_ref[:, c:c + A_WIDTH] = _sigmoid(y + bg_ref[:, c:c + A_WIDTH]).astype(gate_ref.dtype)

    assert A_WIDTH == LAT_COLS == B_WIDTH
    gate_off = SHIFT_COLS + A_WIDTH + LAT_COLS + B_WIDTH
    jobs = [(SHIFT_COLS, store_silu(sa_ref), 0),
            (SHIFT_COLS + A_WIDTH, store_latents, 0),
            (SHIFT_COLS + A_WIDTH + LAT_COLS, store_silu(sb_ref), 0)]
    jobs += [(gate_off + c, store_gates, c) for c in range(0, 2 * D_MODEL, A_WIDTH)]
    pieces = ((0, LO_COLS, wlo_ref), (LO_COLS, PROJ_COLS, whi_ref))

    def project(off, n):
        outs = [jnp.dot(u, ref[:, max(off, lo) - lo:min(off + n, hi) - lo],
                        preferred_element_type=F32)
                for lo, hi, ref in pieces if max(off, lo) < min(off + n, hi)]
        return outs[0] if len(outs) == 1 else jnp.concatenate(outs, axis=1)

    phases = rwkv_operands(project(0, SHIFT_COLS))
    for off, store, c in jobs:
        next(phases, None)
        store(project(off, A_WIDTH), c)
    for _ in phases:
        pass


def _inproj(x2, g_pre, w_parts, b_gate, mu, rwkv_rows, wd_p, wa_p, seq, tm):
    t = x2.shape[0]
    assert tm % CHUNK == 0 and tm // CHUNK <= F32_TILE_ROWS
    const = lambda shape: pl.BlockSpec(shape, lambda i: (0,) * len(shape))
    outs = ((A_WIDTH, BF16), (LAT_COLS, F32), (B_WIDTH, BF16), (2 * D_MODEL, BF16))
    return pl.pallas_call(
        functools.partial(_inproj_kernel, seq // tm),
        grid=(t // tm,),
        in_specs=[
            pl.BlockSpec((tm, D_MODEL), lambda i: (i, 0)),
            const((1, D_MODEL))] + [const(w.shape) for w in w_parts] + [const((1, 2 * D_MODEL)),
            const((1, SHIFT_COLS))] + [const((1, A_WIDTH))] * 5 + [const((LANES, A_WIDTH))] * 2,
        out_specs=[pl.BlockSpec((tm, RWKV_SECTIONS * A_WIDTH), lambda i: (i, 0)),
                   pl.BlockSpec((F32_TILE_ROWS, A_WIDTH), lambda i: (i, 0))]
        + [pl.BlockSpec((tm, n), lambda i: (i, 0)) for n, _ in outs],
        out_shape=[jax.ShapeDtypeStruct((t, RWKV_SECTIONS * A_WIDTH), BF16),
                   jax.ShapeDtypeStruct((t // tm * F32_TILE_ROWS, A_WIDTH), F32)]
        + [jax.ShapeDtypeStruct((t, n), dt) for n, dt in outs],
        scratch_shapes=[pltpu.VMEM((8, SHIFT_COLS), F32)],
        compiler_params=pltpu.CompilerParams(
            dimension_semantics=("arbitrary",), vmem_limit_bytes=VMEM_LIMIT),
        name="inproj",
    )(x2, g_pre, *w_parts, b_gate, mu, *rwkv_rows, wd_p, wa_p)


def _split_bf16(x):
    hi = x.astype(BF16)
    return hi, (x - hi.astype(F32)).astype(BF16)


def _rwkv_kernel(ops_ref, decay_ref, sa_ref, gg_ref, gb_ref, out_ref, state_ref):
    tt = ops_ref.shape[0]
    n_chunks = tt // CHUNK

    @pl.when(pl.program_id(1) == 0)
    def _():
        state_ref[...] = jnp.zeros_like(state_ref)

    ri = lax.broadcasted_iota(jnp.int32, (LANES, LANES), 0)
    ci = lax.broadcasted_iota(jnp.int32, (LANES, LANES), 1)
    same_head = (ri // A_HEAD_DIM) == (ci // A_HEAD_DIM)
    strict_bd = same_head & (ci < ri)
    incl_bd = same_head & (ci <= ri)
    eye = (ri == ci).astype(F32)
    ones2 = jnp.concatenate([same_head, same_head], axis=0).astype(BF16)
    zeros_sq = jnp.zeros((LANES, LANES), F32)
    lane = lax.broadcasted_iota(jnp.int32, (1, LANES), 1)
    m_a = (lane < A_HEAD_DIM).astype(BF16)
    m_b = 1.0 - m_a

    def head_sum(x):
        hi, lo = _split_bf16(x)
        return jnp.dot(jnp.concatenate([hi, lo], axis=1), ones2, preferred_element_type=F32)

    def stack(z):
        return jnp.concatenate([z * m_a, z * m_b], axis=0)

    def unstack(z):
        return z[:CHUNK] + z[CHUNK:]

    pairs = range(N_PAIRS)
    items = [(p, ch) for p in pairs for ch in range(n_chunks)]
    each = lambda fn: {it: fn(it) for it in items}
    pcols = lambda p: slice(p * LANES, (p + 1) * LANES)

    def chunk_operands(it):
        p, ch = it
        sec = lambda j: ops_ref[ch * CHUNK:(ch + 1) * CHUNK,
                                j * A_WIDTH + p * LANES:j * A_WIDTH + (p + 1) * LANES]
        ends_t = [stack(sec(j)).astype(F32).T for j in (4, 5)]
        return dict(cts=stack(sec(0)), rt=sec(1), bt=sec(2), kt=sec(3), vs=stack(sec(6)),
                    decay=eye * decay_ref[ch:ch + 1, pcols(p)],
                    ends_t=jnp.concatenate(ends_t, axis=1))

    ops = each(chunk_operands)
    g = each(lambda it: _dot_nt(
        jnp.concatenate([ops[it]["cts"], stack(ops[it]["rt"])], axis=0),
        jnp.concatenate([ops[it]["bt"], ops[it]["bt"], ops[it]["kt"], ops[it]["kt"]], axis=0)))
    nb = each(lambda it: jnp.where(strict_bd, g[it][:PAIR, :PAIR], 0.0))
    nk = each(lambda it: jnp.where(strict_bd, g[it][:PAIR, PAIR:], 0.0))
    m_bk = each(lambda it: jnp.concatenate(
        [jnp.where(incl_bd, g[it][PAIR:, :PAIR], 0.0),
         jnp.where(incl_bd, g[it][PAIR:, PAIR:], 0.0)], axis=1))
    t_inv = each(lambda it: eye + nb[it])
    npow = each(lambda it: _dot(nb[it], nb[it]))
    nkv = each(lambda it: _dot(nk[it], ops[it]["vs"]))
    for _ in range(int(math.log2(CHUNK)) - 2):
        both = each(lambda it: _dot(npow[it], jnp.concatenate([npow[it], t_inv[it]], axis=1)))
        npow = each(lambda it: both[it][:, :PAIR])
        t_inv = each(lambda it: t_inv[it] + both[it][:, PAIR:])
    last_term = each(lambda it: _dot(npow[it], t_inv[it]))
    t_inv = each(lambda it: t_inv[it] + last_term[it])
    w12 = each(lambda it: _dot(t_inv[it], jnp.concatenate([ops[it]["cts"], nkv[it]], axis=1)))
    big = each(lambda it: _dot(
        jnp.concatenate([ops[it]["ends_t"], m_bk[it]], axis=0),
        jnp.concatenate([w12[it], jnp.concatenate([zeros_sq, ops[it]["vs"]], axis=1)], axis=0)))
    gt = each(lambda it: (big[it][:PAIR, :PAIR] + ops[it]["decay"]).astype(BF16))
    q1 = each(lambda it: ops[it]["rt"].astype(F32) + unstack(big[it][PAIR:, :PAIR]))

    h = {p: state_ref[p] for p in pairs}
    h_at = {}
    for ch in range(n_chunks):
        for p in pairs:
            it = (p, ch)
            h_at[it] = h[p]
            h_hi, h_lo = _split_bf16(h[p])
            h[p] = jnp.dot(jnp.concatenate([gt[it], gt[it]], axis=1),
                           jnp.concatenate([h_hi, h_lo], axis=0),
                           preferred_element_type=F32) + big[it][:PAIR, PAIR:]
    for p in pairs:
        state_ref[p] = h[p]
    y_it = each(lambda it: _dot(q1[it], h_at[it]) + unstack(big[it][PAIR:, PAIR:]))

    y = {p: jnp.concatenate([y_it[(p, ch)] for ch in range(n_chunks)], axis=0) for p in pairs}
    mean = {p: head_sum(y[p]) * (1.0 / A_HEAD_DIM) for p in pairs}
    d = {p: y[p] - mean[p] for p in pairs}
    var = {p: head_sum(d[p] * d[p]) * (1.0 / A_HEAD_DIM) for p in pairs}
    for p in pairs:
        bonus = ops_ref[:, 7 * A_WIDTH + p * LANES:7 * A_WIDTH + (p + 1) * LANES].astype(F32)
        y_n = d[p] * lax.rsqrt(var[p] + GN_EPS) * gg_ref[:, pcols(p)] + gb_ref[:, pcols(p)] + bonus
        out_ref[:, pcols(p)] = (y_n * sa_ref[:, pcols(p)].astype(F32)).astype(out_ref.dtype)


def _rwkv(ops, decay, silu_a, gn_gain, gn_bias, bsz, seq, tt):
    nt = seq // tt
    const = lambda shape: pl.BlockSpec(shape, lambda b, t: (0,) * len(shape))
    rows = lambda r, n: pl.BlockSpec((r, n), lambda b, t: (b * nt + t, 0))
    return pl.pallas_call(
        _rwkv_kernel,
        grid=(bsz, nt),
        in_specs=[rows(tt, RWKV_SECTIONS * A_WIDTH), rows(F32_TILE_ROWS, A_WIDTH), rows(tt, A_WIDTH),
                  const((1, A_WIDTH)), const((1, A_WIDTH))],
        out_specs=rows(tt, A_WIDTH),
        out_shape=jax.ShapeDtypeStruct((bsz * seq, A_WIDTH), BF16),
        scratch_shapes=[pltpu.VMEM((N_PAIRS, LANES, LANES), F32)],
        compiler_params=pltpu.CompilerParams(
            dimension_semantics=("arbitrary", "arbitrary"), vmem_limit_bytes=VMEM_LIMIT),
        name="rwkv7",
    )(ops, decay, silu_a, gn_gain, gn_bias)


QK_SCALE = math.log2(math.e) / math.sqrt(QK_NOPE_DIM + QK_ROPE_DIM)


def _mla_prep_kernel(lat_ref, pos_ref, invf_ref, gq_ref, gkv_ref, wqt_ref, wk_ref, wvt_ref,
                     qt_ref, k_ref, vt_ref):
    lat = lat_ref[...]
    tm = lat.shape[0]
    pos = pos_ref[0].astype(F32)
    ang = invf_ref[...] * pos
    pad = jnp.zeros((LANES - QK_NOPE_DIM - QK_ROPE_DIM, tm), F32)
    cos_q = jnp.concatenate([jnp.ones((QK_NOPE_DIM, tm), F32), jnp.cos(ang), pad], axis=0)
    sin_t = jnp.concatenate([jnp.zeros((QK_NOPE_DIM, tm), F32), jnp.sin(ang), pad], axis=0)

    cq_t = jnp.concatenate([lat[:, j * LANES:(j + 1) * LANES].T
                            for j in range(Q_LORA_RANK // LANES)], axis=0)
    nq_t = cq_t * lax.rsqrt(jnp.mean(cq_t * cq_t, axis=0, keepdims=True) + NORM_EPS) * gq_ref[...]
    q_raw = _dot(wqt_ref[...], nq_t)
    for h in range(B_HEADS):
        sl = slice(h * LANES, (h + 1) * LANES)
        blk = q_raw[sl]
        rot = blk * cos_q + pltpu.roll(blk, LANES - QK_ROPE_DIM, 0) * sin_t
        qt_ref[0, sl, :] = (rot * QK_SCALE).astype(qt_ref.dtype)

    ckv = lat[:, Q_LORA_RANK:Q_LORA_RANK + KV_LORA_RANK]
    nkv = ckv * lax.rsqrt(jnp.mean(ckv * ckv, axis=-1, keepdims=True) + NORM_EPS) * gkv_ref[...]
    vt_ref[0] = _dot(wvt_ref[...], nkv.T).astype(vt_ref.dtype)
    k_nope = _dot(nkv, wk_ref[...])
    lane = lax.broadcasted_iota(jnp.int32, (1, LANES), 1)
    rope_lane = (lane >= QK_NOPE_DIM) & (lane < QK_NOPE_DIM + QK_ROPE_DIM)
    pe = lat[:, Q_LORA_RANK + KV_LORA_RANK:]
    k_pe = pe * jnp.where(rope_lane, cos_q.T, 0.0) + pltpu.roll(pe, LANES - QK_ROPE_DIM, 1) * sin_t.T
    for h in range(B_HEADS):
        sl = slice(h * LANES, (h + 1) * LANES)
        k_ref[:, sl] = (k_nope[:, sl] + k_pe).astype(k_ref.dtype)


def _mla_prep(lat, pos3, invf, g_q, g_kv, wqt_p, wk_p, wvt_p, bsz, seq, tm):
    t = lat.shape[0]
    nt = seq // tm
    const = lambda shape: pl.BlockSpec(shape, lambda i: (0,) * len(shape))
    return pl.pallas_call(
        _mla_prep_kernel,
        grid=(t // tm,),
        in_specs=[
            pl.BlockSpec((tm, LAT_COLS), lambda i: (i, 0)),
            pl.BlockSpec((1, 1, tm), lambda i: (i, 0, 0)),
            const((QK_ROPE_DIM, 1)), const((Q_LORA_RANK, 1)), const((1, KV_LORA_RANK)),
            const((B_HEADS * LANES, Q_LORA_RANK)), const((KV_LORA_RANK, B_HEADS * LANES)),
            const((B_WIDTH, KV_LORA_RANK)),
        ],
        out_specs=[
            pl.BlockSpec((1, B_HEADS * LANES, tm), lambda i: (i // nt, 0, i % nt)),
            pl.BlockSpec((tm, B_HEADS * LANES), lambda i: (i, 0)),
            pl.BlockSpec((1, B_WIDTH, tm), lambda i: (i // nt, 0, i % nt)),
        ],
        out_shape=[
            jax.ShapeDtypeStruct((bsz, B_HEADS * LANES, seq), BF16),
            jax.ShapeDtypeStruct((t, B_HEADS * LANES), BF16),
            jax.ShapeDtypeStruct((bsz, B_WIDTH, seq), BF16),
        ],
        compiler_params=pltpu.CompilerParams(
            dimension_semantics=("arbitrary",), vmem_limit_bytes=VMEM_LIMIT),
        name="mla_prep",
    )(lat, pos3, invf, g_q, g_kv, wqt_p, wk_p, wvt_p)


NEG_BIG = -1e30
ATT_TK = 256
ATT_TQ = 2048
ATT_CB = 256


def _attn_kernel(qt_ref, k_ref, vt_ref, sb_ref, out_ref, s_ref):
    tq = qt_ref.shape[2]
    tk, cb = ATT_TK, ATT_CB
    i = pl.program_id(2)
    q_t = (qt_ref[0, :LANES, :], qt_ref[0, LANES:, :])
    ones_rows = jnp.ones((BF16_TILE_ROWS, tk), BF16)

    def units(q0):
        return [(h, c) for h in range(2) for c in range(q0, tq, cb)]

    def score_unit(j, h, c):
        start = pl.multiple_of(j * tk, tk)
        return jnp.dot(k_ref[0, pl.ds(start, tk), h * LANES:(h + 1) * LANES], q_t[h][:, c:c + cb],
                       preferred_element_type=F32)

    def tile(j, carry, get_s, prefetch, q0=0):
        start = pl.multiple_of(j * tk, tk)
        vb = vt_ref[0, :, pl.ds(start, tk)]
        v_aug = [jnp.concatenate([vb[h * V_HEAD_DIM:(h + 1) * V_HEAD_DIM, :], ones_rows], axis=0)
                 for h in range(2)]
        cols = [[tuple(x[:, :q0] for x in carry[h])] if q0 else [] for h in range(2)]
        for u, (h, c) in enumerate(units(q0)):
            prefetch(u)
            s_u = get_s(u, h, c)
            m, l, acc = [x[:, c:c + cb] for x in carry[h]]
            m_new = jnp.maximum(m, jnp.max(s_u, axis=0, keepdims=True))
            alpha = jnp.exp2(m - m_new)
            p_u = jnp.exp2(s_u - m_new).astype(BF16)
            pv = jnp.dot(v_aug[h], p_u, preferred_element_type=F32)
            cols[h].append((m_new, l * alpha + pv[V_HEAD_DIM:V_HEAD_DIM + 1],
                            acc * alpha + pv[:V_HEAD_DIM]))
        return tuple(tuple(jnp.concatenate(parts, axis=1) if len(parts) > 1 else parts[0]
                           for parts in zip(*cols[h])) for h in range(2))

    init = tuple((jnp.full((1, tq), NEG_BIG, F32), jnp.zeros((1, tq), F32),
                  jnp.zeros((V_HEAD_DIM, tq), F32)) for _ in range(2))
    n_full = i * (tq // tk)
    all_units = units(0)

    def visible_tile(j, slot, carry):
        def prefetch(u):
            h, c = all_units[u]
            s_ref[1 - slot, u] = score_unit(j + 1, h, c)
        return tile(j, carry, lambda u, h, c: s_ref[slot, u], prefetch)

    def two_tiles(jj, carry):
        return visible_tile(2 * jj + 1, 1, visible_tile(2 * jj, 0, carry))

    assert (tq // tk) % 2 == 0
    for u, (h, c) in enumerate(all_units):
        s_ref[0, u] = score_unit(0, h, c)
    carry = lax.fori_loop(0, n_full // 2, two_tiles, init)

    key = lax.broadcasted_iota(jnp.int32, (tk, cb), 0)
    query = lax.broadcasted_iota(jnp.int32, (tk, cb), 1)
    ahead = {}
    for d in range(tq // tk):
        q0 = d * tk
        here, ahead = ahead, {}
        nxt = units(q0 + tk) if d + 1 < tq // tk else []

        def prefetch(u, d=d, nxt=nxt, ahead=ahead):
            if u < len(nxt):
                ahead[nxt[u]] = score_unit(n_full + d + 1, *nxt[u])

        def get_s(u, h, c, d=d, q0=q0, here=here):
            s_u = s_ref[0, u] if d == 0 else here[h, c]
            on_diagonal = q0 <= c < q0 + tk
            return jnp.where(key <= query + (c - q0), s_u, NEG_BIG) if on_diagonal else s_u

        carry = tile(n_full + d, carry, get_s, prefetch, q0)
    o_t = jnp.concatenate([acc / l for (_, l, acc) in carry], axis=0)
    out_ref[0] = (o_t.T * sb_ref[0].astype(F32)).astype(out_ref.dtype)


def _attention(q_t, k, v_t, silu_b, tq):
    bsz, seq, _ = k.shape
    return pl.pallas_call(
        _attn_kernel,
        grid=(bsz, B_HEADS // 2, seq // tq),
        in_specs=[
            pl.BlockSpec((1, 2 * LANES, tq), lambda b, p, i: (b, p, i)),
            pl.BlockSpec((1, seq, 2 * LANES), lambda b, p, i: (b, 0, p)),
            pl.BlockSpec((1, 2 * V_HEAD_DIM, seq), lambda b, p, i: (b, p, 0)),
            pl.BlockSpec((1, tq, LANES), lambda b, p, i: (b, i, p)),
        ],
        out_specs=pl.BlockSpec((1, tq, LANES), lambda b, p, i: (b, i, p)),
        out_shape=jax.ShapeDtypeStruct((bsz, seq, B_WIDTH), BF16),
        scratch_shapes=[pltpu.VMEM((2, 2 * (tq // ATT_CB), ATT_TK, ATT_CB), F32)],
        compiler_params=pltpu.CompilerParams(
            dimension_semantics=("arbitrary", "arbitrary", "arbitrary"),
            vmem_limit_bytes=VMEM_LIMIT),
        name="mla_attn",
    )(q_t, k, v_t, silu_b)


def _out_kernel(ya_ref, yb_ref, gate_ref, x_ref, woa_ref, wob_ref, wo_ref, gp_ref, out_ref):
    y_a = jnp.dot(ya_ref[...], woa_ref[...], preferred_element_type=F32)
    y_b = jnp.dot(yb_ref[...], wob_ref[...], preferred_element_type=F32)
    gates = gate_ref[...].astype(F32)
    merged = gates[:, :D_MODEL] * y_a + gates[:, D_MODEL:] * y_b
    o = _dot(merged, wo_ref[...])
    ms = jnp.mean(o * o, axis=-1, keepdims=True)
    out_ref[...] = x_ref[...] + o * lax.rsqrt(ms + NORM_EPS) * gp_ref[...]


def _out_proj(ya, yb, gates, x2, woa, wob, wo, g_post, tm):
    t = x2.shape[0]
    const = lambda shape: pl.BlockSpec(shape, lambda i: (0,) * len(shape))
    rows = lambda n: pl.BlockSpec((tm, n), lambda i: (i, 0))
    return pl.pallas_call(
        _out_kernel,
        grid=(t // tm,),
        in_specs=[
            rows(A_WIDTH), rows(B_WIDTH), rows(2 * D_MODEL), rows(D_MODEL),
            const((A_WIDTH, D_MODEL)), const((B_WIDTH, D_MODEL)),
            const((D_MODEL, D_MODEL)), const((1, D_MODEL)),
        ],
        out_specs=rows(D_MODEL),
        out_shape=jax.ShapeDtypeStruct((t, D_MODEL), F32),
        compiler_params=pltpu.CompilerParams(
            dimension_semantics=("arbitrary",), vmem_limit_bytes=VMEM_LIMIT),
        name="out_proj",
    )(ya, yb, gates, x2, woa, wob, wo, g_post)


def _rope_partner(w_pe):
    half = QK_ROPE_DIM // 2
    return jnp.concatenate([-w_pe[..., half:], w_pe[..., :half]], axis=-1)


def _prep_weights(w_in, w_decay_up, w_iclr_up, w_uq, w_ukv):
    o = np.cumsum([0, SHIFT_COLS, A_WIDTH, Q_LORA_RANK, KV_LORA_RANK, QK_ROPE_DIM, B_WIDTH])
    w_pe = w_in[:, o[4]:o[5]]
    kpe_blk = jnp.concatenate(
        [jnp.zeros((D_MODEL, QK_NOPE_DIM), F32), w_pe, _rope_partner(w_pe)], axis=-1)
    w_parts = (jnp.concatenate([w_in[:, :o[4]].astype(BF16), kpe_blk.astype(BF16)], axis=-1),
               w_in[:, o[5]:].astype(BF16))
    zeros_lora = jnp.zeros((DECAY_LORA, A_WIDTH), F32)
    wd_p = jnp.concatenate([w_decay_up, zeros_lora], axis=0).astype(BF16)
    wa_p = jnp.concatenate([zeros_lora, w_iclr_up], axis=0).astype(BF16)
    wq = w_uq.reshape(Q_LORA_RANK, B_HEADS, QK_NOPE_DIM + QK_ROPE_DIM)
    wqt_p = jnp.concatenate(
        [wq, _rope_partner(wq[..., QK_NOPE_DIM:])], axis=-1).reshape(Q_LORA_RANK, -1).T.astype(BF16)
    wkv = w_ukv.reshape(KV_LORA_RANK, B_HEADS, QK_NOPE_DIM + V_HEAD_DIM)
    wk_p = jnp.concatenate(
        [wkv[..., :QK_NOPE_DIM], jnp.zeros_like(wkv[..., :QK_NOPE_DIM])],
        axis=-1).reshape(KV_LORA_RANK, -1).astype(BF16)
    wvt_p = wkv[..., QK_NOPE_DIM:].reshape(KV_LORA_RANK, -1).T.astype(BF16)
    return w_parts, wd_p, wa_p, wqt_p, wk_p, wvt_p


def kernel(x, positions, g_pre, w_in, b_gate, mu_shift, w0, w_decay_up, a0, w_iclr_up, k_k, k_a,
           r_k, gn_gain, gn_bias, w_out_a, g_q, w_uq, g_kv, w_ukv, w_out_b, w_o, g_post):
    bsz, seq, _ = x.shape
    t = bsz * seq
    tm = TM_ROWS
    row = lambda a: a.reshape(1, -1).astype(F32)
    w_parts, wd_p, wa_p, wqt_p, wk_p, wvt_p = _prep_weights(w_in, w_decay_up, w_iclr_up, w_uq, w_ukv)
    x2 = x.reshape(t, D_MODEL)

    assert TM_INPROJ == RWKV_TT
    rwkv_ops, decay, silu_a, lat, silu_b, gates = _inproj(
        x2, row(g_pre), w_parts, row(b_gate), row(mu_shift),
        [row(w0), row(a0), row(k_k), row(k_a), row(r_k)], wd_p, wa_p, seq, TM_INPROJ)

    ya = _rwkv(rwkv_ops, decay, silu_a, row(gn_gain), row(gn_bias), bsz, seq, RWKV_TT)

    freq = ROPE_THETA ** (-(np.arange(QK_ROPE_DIM) % (QK_ROPE_DIM // 2)) * 2.0 / QK_ROPE_DIM)
    invf = jnp.asarray(freq.reshape(QK_ROPE_DIM, 1), F32)
    pos3 = positions.reshape(t // tm, 1, tm)
    q_t, k, v_t = _mla_prep(lat, pos3, invf, g_q.reshape(-1, 1).astype(F32), row(g_kv),
                            wqt_p, wk_p, wvt_p, bsz, seq, tm)
    yb = _attention(q_t, k.reshape(bsz, seq, -1), v_t, silu_b.reshape(bsz, seq, -1), ATT_TQ)

    out = _out_proj(ya, yb.reshape(t, B_WIDTH), gates, x2, w_out_a.astype(BF16),
                    w_out_b.astype(BF16), w_o.astype(BF16), row(g_post), tm)
    return out.reshape(bsz, seq, D_MODEL)
```

```python
import functools
import math

import jax
import jax.numpy as jnp
import numpy as np
from jax import lax
from jax.experimental import pallas as pl
from jax.experimental.pallas import tpu as pltpu

D_MODEL = 1024
A_HEADS = 8
A_HEAD_DIM = 64
A_WIDTH = A_HEADS * A_HEAD_DIM
DECAY_LORA = 64
ICLR_LORA = 64
DECAY_SCALE = 0.6065306597
GN_EPS = 64e-5
B_HEADS = 8
QK_NOPE_DIM = 64
QK_ROPE_DIM = 32
V_HEAD_DIM = 64
Q_LORA_RANK = 256
KV_LORA_RANK = 128
B_WIDTH = B_HEADS * V_HEAD_DIM
ROPE_THETA = 10000.0
NORM_EPS = 1e-6
SHIFT_COLS = 3 * A_WIDTH + DECAY_LORA + ICLR_LORA

LANES = 128
PAIR = 2 * A_HEAD_DIM
N_PAIRS = A_HEADS // 2
CHUNK = 64
LAT_COLS = 512
PROJ_COLS = SHIFT_COLS + A_WIDTH + LAT_COLS + B_WIDTH + 2 * D_MODEL
LO_COLS = SHIFT_COLS + A_WIDTH + LAT_COLS
VMEM_LIMIT = 52 * 1024 * 1024
TM_INPROJ = 256
TM_ROWS = 1024
RWKV_TT = 256

F32 = jnp.float32
BF16 = jnp.bfloat16
F32_TILE_ROWS = 8
BF16_TILE_ROWS = 16


def _dot(a, b):
    return jnp.dot(a.astype(BF16), b.astype(BF16), preferred_element_type=F32)


def _dot_nt(a, b):
    return lax.dot_general(a.astype(BF16), b.astype(BF16), (((1,), (1,)), ((), ())),
                           preferred_element_type=F32)


def _sigmoid(x):
    return 0.5 * jnp.tanh(0.5 * x) + 0.5


RWKV_SECTIONS = 8


def _inproj_kernel(tiles_per_seq, x_ref, g_ref, wlo_ref, whi_ref, bg_ref, mu_ref, w0_ref, a0_ref, kk_ref, ka_ref,
                   rk_ref, wd_ref, wa_ref,
                   ops_ref, decay_ref, sa_ref, lat_ref, sb_ref, gate_ref, last_ref):
    @pl.when(pl.program_id(0) % tiles_per_seq == 0)
    def _():
        last_ref[...] = jnp.zeros_like(last_ref)

    x = x_ref[...]
    tm = x.shape[0]
    u = (x * g_ref[...]).astype(BF16)
    inv_rms = lax.rsqrt(jnp.mean(x * x, axis=-1, keepdims=True) + NORM_EPS)
    silu = lambda z: z * _sigmoid(z)
    first_row = lax.broadcasted_iota(jnp.int32, (tm, 1), 0) == 0

    def token_shift(feat):
        prev = jnp.where(first_row, last_ref[0:1, :], pltpu.roll(feat, 1, 0))
        last_ref[0:1, :] = feat[tm - 1:tm, :]
        return feat + mu_ref[...] * (prev - feat)

    first_head = lax.broadcasted_iota(jnp.int32, (1, LANES), 1) < A_HEAD_DIM
    pos_in_chunk = lax.broadcasted_iota(jnp.int32, (tm, 1), 0) % CHUNK

    def head_sum(z):
        out = []
        for p in range(N_PAIRS):
            blk = z[:, p * LANES:(p + 1) * LANES]
            s_a = jnp.sum(jnp.where(first_head, blk, 0.0), axis=1, keepdims=True)
            s_b = jnp.sum(jnp.where(first_head, 0.0, blk), axis=1, keepdims=True)
            out.append(jnp.where(first_head, s_a, s_b))
        return jnp.concatenate(out, axis=1)

    tok_r = lax.broadcasted_iota(jnp.int32, (tm, tm), 0)
    tok_c = lax.broadcasted_iota(jnp.int32, (tm, tm), 1)
    prefix = ((tok_c <= tok_r) & (tok_c // CHUNK == tok_r // CHUNK)).astype(BF16)

    def chunk_cumsum(z):
        hi = z.astype(BF16)
        lo = (z - hi.astype(F32)).astype(BF16)
        return (jnp.dot(prefix, hi, preferred_element_type=F32)
                + jnp.dot(prefix, lo, preferred_element_type=F32))

    def rwkv_operands(feat):
        f = token_shift(feat)
        r, k, v = (f[:, j * A_WIDTH:(j + 1) * A_WIDTH] for j in range(3))
        la = f[:, 3 * A_WIDTH:]
        w_logit = w0_ref[...] + jnp.dot(jnp.tanh(la).astype(BF16), wd_ref[...],
                                        preferred_element_type=F32)
        a_logit = a0_ref[...] + jnp.dot(la.astype(BF16), wa_ref[...], preferred_element_type=F32)
        yield
        a = _sigmoid(a_logit)
        lw = -DECAY_SCALE * _sigmoid(w_logit)
        kk = k * kk_ref[...]
        k2 = k * (1.0 + (a - 1.0) * ka_ref[...])
        kk = kk / jnp.maximum(jnp.sqrt(head_sum(kk * kk)), 1e-12)
        yield
        bonus = head_sum(r * k2 * rk_ref[...]) * v
        b = kk * a
        yield
        cum = chunk_cumsum(lw)
        ends = [cum[c + CHUNK - 1:c + CHUNK, :] for c in range(0, tm, CHUNK)]
        last = jnp.concatenate([jnp.broadcast_to(e, (CHUNK, A_WIDTH)) for e in ends], axis=0)
        yield
        p_inv = jnp.exp(-cum)
        to_end = jnp.exp(last - cum)
        sections = (lambda: -kk * jnp.exp(cum - lw), lambda: r * jnp.exp(cum), lambda: b * p_inv,
                    lambda: k2 * p_inv, lambda: b * to_end, lambda: k2 * to_end, lambda: v,
                    lambda: bonus)
        assert len(sections) == RWKV_SECTIONS
        for j, sec in enumerate(sections):
            ops_ref[:, j * A_WIDTH:(j + 1) * A_WIDTH] = sec().astype(ops_ref.dtype)
            if j % 3 == 2:
                yield
        pad = jnp.zeros((decay_ref.shape[0] - len(ends), A_WIDTH), F32)
        decay_ref[...] = jnp.concatenate([jnp.exp(e) for e in ends] + [pad], axis=0)

    def store_silu(ref):
        def store(y, c):
            ref[...] = silu(y).astype(ref.dtype)
        return store

    def store_latents(y, c):
        lat_ref[...] = y

    def store_gates(y, c):
        gate_ref[:, c:c + A_WIDTH] = _sigmoid(y + bg_ref[:, c:c + A_WIDTH]).astype(gate_ref.dtype)

    assert A_WIDTH == LAT_COLS == B_WIDTH
    gate_off = SHIFT_COLS + A_WIDTH + LAT_COLS + B_WIDTH
    jobs = [(SHIFT_COLS, store_silu(sa_ref), 0),
            (SHIFT_COLS + A_WIDTH, store_latents, 0),
            (SHIFT_COLS + A_WIDTH + LAT_COLS, store_silu(sb_ref), 0)]
    jobs += [(gate_off + c, store_gates, c) for c in range(0, 2 * D_MODEL, A_WIDTH)]
    pieces = ((0, LO_COLS, wlo_ref), (LO_COLS, PROJ_COLS, whi_ref))

    def project(off, n):
        outs = [jnp.dot(u, ref[:, max(off, lo) - lo:min(off + n, hi) - lo],
                        preferred_element_type=F32)
                for lo, hi, ref in pieces if max(off, lo) < min(off + n, hi)]
        return outs[0] if len(outs) == 1 else jnp.concatenate(outs, axis=1)

    phases = rwkv_operands(project(0, SHIFT_COLS) * inv_rms)
    for off, store, c in jobs:
        next(phases, None)
        store(project(off, A_WIDTH) * inv_rms, c)
    for _ in phases:
        pass


def _inproj(x2, g_pre, w_parts, b_gate, mu, rwkv_rows, wd_p, wa_p, seq, tm):
    t = x2.shape[0]
    assert tm % CHUNK == 0 and tm // CHUNK <= F32_TILE_ROWS
    const = lambda shape: pl.BlockSpec(shape, lambda i: (0,) * len(shape))
    outs = ((A_WIDTH, BF16), (LAT_COLS, F32), (B_WIDTH, BF16), (2 * D_MODEL, BF16))
    return pl.pallas_call(
        functools.partial(_inproj_kernel, seq // tm),
        grid=(t // tm,),
        in_specs=[
            pl.BlockSpec((tm, D_MODEL), lambda i: (i, 0)),
            const((1, D_MODEL))] + [const(w.shape) for w in w_parts] + [const((1, 2 * D_MODEL)),
            const((1, SHIFT_COLS))] + [const((1, A_WIDTH))] * 5 + [const((LANES, A_WIDTH))] * 2,
        out_specs=[pl.BlockSpec((tm, RWKV_SECTIONS * A_WIDTH), lambda i: (i, 0)),
                   pl.BlockSpec((F32_TILE_ROWS, A_WIDTH), lambda i: (i, 0))]
        + [pl.BlockSpec((tm, n), lambda i: (i, 0)) for n, _ in outs],
        out_shape=[jax.ShapeDtypeStruct((t, RWKV_SECTIONS * A_WIDTH), BF16),
                   jax.ShapeDtypeStruct((t // tm * F32_TILE_ROWS, A_WIDTH), F32)]
        + [jax.ShapeDtypeStruct((t, n), dt) for n, dt in outs],
        scratch_shapes=[pltpu.VMEM((8, SHIFT_COLS), F32)],
        compiler_params=pltpu.CompilerParams(
            dimension_semantics=("arbitrary",), vmem_limit_bytes=VMEM_LIMIT),
        name="inproj",
    )(x2, g_pre, *w_parts, b_gate, mu, *rwkv_rows, wd_p, wa_p)


def _split_bf16(x):
    hi = x.astype(BF16)
    return hi, (x - hi.astype(F32)).astype(BF16)


def _rwkv_kernel(ops_ref, decay_ref, sa_ref, gg_ref, gb_ref, out_ref, state_ref):
    tt = ops_ref.shape[0]
    n_chunks = tt // CHUNK

    @pl.when(pl.program_id(1) == 0)
    def _():
        state_ref[...] = jnp.zeros_like(state_ref)

    ri = lax.broadcasted_iota(jnp.int32, (LANES, LANES), 0)
    ci = lax.broadcasted_iota(jnp.int32, (LANES, LANES), 1)
    same_head = (ri // A_HEAD_DIM) == (ci // A_HEAD_DIM)
    strict_bd = same_head & (ci < ri)
    incl_bd = same_head & (ci <= ri)
    eye = (ri == ci).astype(F32)
    ones2 = jnp.concatenate([same_head, same_head], axis=0).astype(BF16)
    zeros_sq = jnp.zeros((LANES, LANES), F32)
    lane = lax.broadcasted_iota(jnp.int32, (1, LANES), 1)
    m_a = (lane < A_HEAD_DIM).astype(BF16)
    m_b = 1.0 - m_a

    def head_sum(x):
        hi, lo = _split_bf16(x)
        return jnp.dot(jnp.concatenate([hi, lo], axis=1), ones2, preferred_element_type=F32)

    def stack(z):
        return jnp.concatenate([z * m_a, z * m_b], axis=0)

    def unstack(z):
        return z[:CHUNK] + z[CHUNK:]

    pairs = range(N_PAIRS)
    items = [(p, ch) for p in pairs for ch in range(n_chunks)]
    each = lambda fn: {it: fn(it) for it in items}
    pcols = lambda p: slice(p * LANES, (p + 1) * LANES)

    def chunk_operands(it):
        p, ch = it
        sec = lambda j: ops_ref[ch * CHUNK:(ch + 1) * CHUNK,
                                j * A_WIDTH + p * LANES:j * A_WIDTH + (p + 1) * LANES]
        ends_t = [stack(sec(j)).astype(F32).T for j in (4, 5)]
        return dict(cts=stack(sec(0)), rt=sec(1), bt=sec(2), kt=sec(3), vs=stack(sec(6)),
                    decay=eye * decay_ref[ch:ch + 1, pcols(p)],
                    ends_t=jnp.concatenate(ends_t, axis=1))

    ops = each(chunk_operands)
    g = each(lambda it: _dot_nt(
        jnp.concatenate([ops[it]["cts"], stack(ops[it]["rt"])], axis=0),
        jnp.concatenate([ops[it]["bt"], ops[it]["bt"], ops[it]["kt"], ops[it]["kt"]], axis=0)))
    nb = each(lambda it: jnp.where(strict_bd, g[it][:PAIR, :PAIR], 0.0))
    nk = each(lambda it: jnp.where(strict_bd, g[it][:PAIR, PAIR:], 0.0))
    m_bk = each(lambda it: jnp.concatenate(
        [jnp.where(incl_bd, g[it][PAIR:, :PAIR], 0.0),
         jnp.where(incl_bd, g[it][PAIR:, PAIR:], 0.0)], axis=1))
    t_inv = each(lambda it: eye + nb[it])
    npow = each(lambda it: _dot(nb[it], nb[it]))
    nkv = each(lambda it: _dot(nk[it], ops[it]["vs"]))
    for _ in range(int(math.log2(CHUNK)) - 2):
        both = each(lambda it: _dot(npow[it], jnp.concatenate([npow[it], t_inv[it]], axis=1)))
        npow = each(lambda it: both[it][:, :PAIR])
        t_inv = each(lambda it: t_inv[it] + both[it][:, PAIR:])
    last_term = each(lambda it: _dot(npow[it], t_inv[it]))
    t_inv = each(lambda it: t_inv[it] + last_term[it])
    w12 = each(lambda it: _dot(t_inv[it], jnp.concatenate([ops[it]["cts"], nkv[it]], axis=1)))
    big = each(lambda it: _dot(
        jnp.concatenate([ops[it]["ends_t"], m_bk[it]], axis=0),
        jnp.concatenate([w12[it], jnp.concatenate([zeros_sq, ops[it]["vs"]], axis=1)], axis=0)))
    gt = each(lambda it: (big[it][:PAIR, :PAIR] + ops[it]["decay"]).astype(BF16))
    q1 = each(lambda it: ops[it]["rt"].astype(F32) + unstack(big[it][PAIR:, :PAIR]))

    h = {p: state_ref[p] for p in pairs}
    h_at = {}
    for ch in range(n_chunks):
        for p in pairs:
            it = (p, ch)
            h_at[it] = h[p]
            h_hi, h_lo = _split_bf16(h[p])
            h[p] = jnp.dot(jnp.concatenate([gt[it], gt[it]], axis=1),
                           jnp.concatenate([h_hi, h_lo], axis=0),
                           preferred_element_type=F32) + big[it][:PAIR, PAIR:]
    for p in pairs:
        state_ref[p] = h[p]
    y_it = each(lambda it: _dot(q1[it], h_at[it]) + unstack(big[it][PAIR:, PAIR:]))

    y = {p: jnp.concatenate([y_it[(p, ch)] for ch in range(n_chunks)], axis=0) for p in pairs}
    mean = {p: head_sum(y[p]) * (1.0 / A_HEAD_DIM) for p in pairs}
    d = {p: y[p] - mean[p] for p in pairs}
    var = {p: head_sum(d[p] * d[p]) * (1.0 / A_HEAD_DIM) for p in pairs}
    for p in pairs:
        bonus = ops_ref[:, 7 * A_WIDTH + p * LANES:7 * A_WIDTH + (p + 1) * LANES].astype(F32)
        y_n = d[p] * lax.rsqrt(var[p] + GN_EPS) * gg_ref[:, pcols(p)] + gb_ref[:, pcols(p)] + bonus
        out_ref[:, pcols(p)] = (y_n * sa_ref[:, pcols(p)].astype(F32)).astype(out_ref.dtype)


def _rwkv(ops, decay, silu_a, gn_gain, gn_bias, bsz, seq, tt):
    nt = seq // tt
    const = lambda shape: pl.BlockSpec(shape, lambda b, t: (0,) * len(shape))
    rows = lambda r, n: pl.BlockSpec((r, n), lambda b, t: (b * nt + t, 0))
    return pl.pallas_call(
        _rwkv_kernel,
        grid=(bsz, nt),
        in_specs=[rows(tt, RWKV_SECTIONS * A_WIDTH), rows(F32_TILE_ROWS, A_WIDTH), rows(tt, A_WIDTH),
                  const((1, A_WIDTH)), const((1, A_WIDTH))],
        out_specs=rows(tt, A_WIDTH),
        out_shape=jax.ShapeDtypeStruct((bsz * seq, A_WIDTH), BF16),
        scratch_shapes=[pltpu.VMEM((N_PAIRS, LANES, LANES), F32)],
        compiler_params=pltpu.CompilerParams(
            dimension_semantics=("arbitrary", "arbitrary"), vmem_limit_bytes=VMEM_LIMIT),
        name="rwkv7",
    )(ops, decay, silu_a, gn_gain, gn_bias)


QK_SCALE = math.log2(math.e) / math.sqrt(QK_NOPE_DIM + QK_ROPE_DIM)


def _mla_prep_kernel(lat_ref, pos_ref, invf_ref, gq_ref, gkv_ref, wqt_ref, wk_ref, wvt_ref,
                     qt_ref, k_ref, vt_ref):
    lat = lat_ref[...]
    tm = lat.shape[0]
    pos = pos_ref[0].astype(F32)
    ang = invf_ref[...] * pos
    pad = jnp.zeros((LANES - QK_NOPE_DIM - QK_ROPE_DIM, tm), F32)
    cos_q = jnp.concatenate([jnp.ones((QK_NOPE_DIM, tm), F32), jnp.cos(ang), pad], axis=0)
    sin_t = jnp.concatenate([jnp.zeros((QK_NOPE_DIM, tm), F32), jnp.sin(ang), pad], axis=0)

    cq_t = jnp.concatenate([lat[:, j * LANES:(j + 1) * LANES].T
                            for j in range(Q_LORA_RANK // LANES)], axis=0)
    nq_t = cq_t * lax.rsqrt(jnp.mean(cq_t * cq_t, axis=0, keepdims=True) + NORM_EPS) * gq_ref[...]
    q_raw = _dot(wqt_ref[...], nq_t)
    for h in range(B_HEADS):
        sl = slice(h * LANES, (h + 1) * LANES)
        blk = q_raw[sl]
        rot = blk * cos_q + pltpu.roll(blk, LANES - QK_ROPE_DIM, 0) * sin_t
        qt_ref[0, sl, :] = (rot * QK_SCALE).astype(qt_ref.dtype)

    ckv = lat[:, Q_LORA_RANK:Q_LORA_RANK + KV_LORA_RANK]
    nkv = ckv * lax.rsqrt(jnp.mean(ckv * ckv, axis=-1, keepdims=True) + NORM_EPS) * gkv_ref[...]
    vt_ref[0] = _dot(wvt_ref[...], nkv.T).astype(vt_ref.dtype)
    k_nope = _dot(nkv, wk_ref[...])
    lane = lax.broadcasted_iota(jnp.int32, (1, LANES), 1)
    rope_lane = (lane >= QK_NOPE_DIM) & (lane < QK_NOPE_DIM + QK_ROPE_DIM)
    pe = lat[:, Q_LORA_RANK + KV_LORA_RANK:]
    k_pe = pe * jnp.where(rope_lane, cos_q.T, 0.0) + pltpu.roll(pe, LANES - QK_ROPE_DIM, 1) * sin_t.T
    for h in range(B_HEADS):
        sl = slice(h * LANES, (h + 1) * LANES)
        k_ref[:, sl] = (k_nope[:, sl] + k_pe).astype(k_ref.dtype)


def _mla_prep(lat, pos3, invf, g_q, g_kv, wqt_p, wk_p, wvt_p, bsz, seq, tm):
    t = lat.shape[0]
    nt = seq // tm
    const = lambda shape: pl.BlockSpec(shape, lambda i: (0,) * len(shape))
    return pl.pallas_call(
        _mla_prep_kernel,
        grid=(t // tm,),
        in_specs=[
            pl.BlockSpec((tm, LAT_COLS), lambda i: (i, 0)),
            pl.BlockSpec((1, 1, tm), lambda i: (i, 0, 0)),
            const((QK_ROPE_DIM, 1)), const((Q_LORA_RANK, 1)), const((1, KV_LORA_RANK)),
            const((B_HEADS * LANES, Q_LORA_RANK)), const((KV_LORA_RANK, B_HEADS * LANES)),
            const((B_WIDTH, KV_LORA_RANK)),
        ],
        out_specs=[
            pl.BlockSpec((1, B_HEADS * LANES, tm), lambda i: (i // nt, 0, i % nt)),
            pl.BlockSpec((tm, B_HEADS * LANES), lambda i: (i, 0)),
            pl.BlockSpec((1, B_WIDTH, tm), lambda i: (i // nt, 0, i % nt)),
        ],
        out_shape=[
            jax.ShapeDtypeStruct((bsz, B_HEADS * LANES, seq), BF16),
            jax.ShapeDtypeStruct((t, B_HEADS * LANES), BF16),
            jax.ShapeDtypeStruct((bsz, B_WIDTH, seq), BF16),
        ],
        compiler_params=pltpu.CompilerParams(
            dimension_semantics=("arbitrary",), vmem_limit_bytes=VMEM_LIMIT),
        name="mla_prep",
    )(lat, pos3, invf, g_q, g_kv, wqt_p, wk_p, wvt_p)


NEG_BIG = -1e30
ATT_TK = 256
ATT_TQ = 2048
ATT_CB = 256


def _attn_kernel(qt_ref, k_ref, vt_ref, sb_ref, out_ref, s_ref):
    tq = qt_ref.shape[2]
    tk, cb = ATT_TK, ATT_CB
    i = pl.program_id(2)
    q_t = (qt_ref[0, :LANES, :], qt_ref[0, LANES:, :])
    ones_rows = jnp.ones((BF16_TILE_ROWS, tk), BF16)

    def units(q0):
        return [(h, c) for h in range(2) for c in range(q0, tq, cb)]

    def score_unit(j, h, c):
        start = pl.multiple_of(j * tk, tk)
        return jnp.dot(k_ref[0, pl.ds(start, tk), h * LANES:(h + 1) * LANES], q_t[h][:, c:c + cb],
                       preferred_element_type=F32)

    def tile(j, carry, get_s, prefetch, q0=0):
        start = pl.multiple_of(j * tk, tk)
        vb = vt_ref[0, :, pl.ds(start, tk)]
        v_aug = [jnp.concatenate([vb[h * V_HEAD_DIM:(h + 1) * V_HEAD_DIM, :], ones_rows], axis=0)
                 for h in range(2)]
        cols = [[tuple(x[:, :q0] for x in carry[h])] if q0 else [] for h in range(2)]
        for u, (h, c) in enumerate(units(q0)):
            prefetch(u)
            s_u = get_s(u, h, c)
            m, l, acc = [x[:, c:c + cb] for x in carry[h]]
            m_new = jnp.maximum(m, jnp.max(s_u, axis=0, keepdims=True))
            alpha = jnp.exp2(m - m_new)
            p_u = jnp.exp2(s_u - m_new).astype(BF16)
            pv = jnp.dot(v_aug[h], p_u, preferred_element_type=F32)
            cols[h].append((m_new, l * alpha + pv[V_HEAD_DIM:V_HEAD_DIM + 1],
                            acc * alpha + pv[:V_HEAD_DIM]))
        return tuple(tuple(jnp.concatenate(parts, axis=1) if len(parts) > 1 else parts[0]
                           for parts in zip(*cols[h])) for h in range(2))

    init = tuple((jnp.full((1, tq), NEG_BIG, F32), jnp.zeros((1, tq), F32),
                  jnp.zeros((V_HEAD_DIM, tq), F32)) for _ in range(2))
    n_full = i * (tq // tk)
    all_units = units(0)

    def visible_tile(j, slot, carry):
        def prefetch(u):
            h, c = all_units[u]
            s_ref[1 - slot, u] = score_unit(j + 1, h, c)
        return tile(j, carry, lambda u, h, c: s_ref[slot, u], prefetch)

    def two_tiles(jj, carry):
        return visible_tile(2 * jj + 1, 1, visible_tile(2 * jj, 0, carry))

    assert (tq // tk) % 2 == 0
    for u, (h, c) in enumerate(all_units):
        s_ref[0, u] = score_unit(0, h, c)
    carry = lax.fori_loop(0, n_full // 2, two_tiles, init)

    key = lax.broadcasted_iota(jnp.int32, (tk, cb), 0)
    query = lax.broadcasted_iota(jnp.int32, (tk, cb), 1)
    ahead = {}
    for d in range(tq // tk):
        q0 = d * tk
        here, ahead = ahead, {}
        nxt = units(q0 + tk) if d + 1 < tq // tk else []

        def prefetch(u, d=d, nxt=nxt, ahead=ahead):
            if u < len(nxt):
                ahead[nxt[u]] = score_unit(n_full + d + 1, *nxt[u])

        def get_s(u, h, c, d=d, q0=q0, here=here):
            s_u = s_ref[0, u] if d == 0 else here[h, c]
            on_diagonal = q0 <= c < q0 + tk
            return jnp.where(key <= query + (c - q0), s_u, NEG_BIG) if on_diagonal else s_u

        carry = tile(n_full + d, carry, get_s, prefetch, q0)
    o_t = jnp.concatenate([acc / l for (_, l, acc) in carry], axis=0)
    out_ref[0] = (o_t.T * sb_ref[0].astype(F32)).astype(out_ref.dtype)


def _attention(q_t, k, v_t, silu_b, tq):
    bsz, seq, _ = k.shape
    return pl.pallas_call(
        _attn_kernel,
        grid=(bsz, B_HEADS // 2, seq // tq),
        in_specs=[
            pl.BlockSpec((1, 2 * LANES, tq), lambda b, p, i: (b, p, i)),
            pl.BlockSpec((1, seq, 2 * LANES), lambda b, p, i: (b, 0, p)),
            pl.BlockSpec((1, 2 * V_HEAD_DIM, seq), lambda b, p, i: (b, p, 0)),
            pl.BlockSpec((1, tq, LANES), lambda b, p, i: (b, i, p)),
        ],
        out_specs=pl.BlockSpec((1, tq, LANES), lambda b, p, i: (b, i, p)),
        out_shape=jax.ShapeDtypeStruct((bsz, seq, B_WIDTH), BF16),
        scratch_shapes=[pltpu.VMEM((2, 2 * (tq // ATT_CB), ATT_TK, ATT_CB), F32)],
        compiler_params=pltpu.CompilerParams(
            dimension_semantics=("arbitrary", "arbitrary", "arbitrary"),
            vmem_limit_bytes=VMEM_LIMIT),
        name="mla_attn",
    )(q_t, k, v_t, silu_b)


def _out_kernel(ya_ref, yb_ref, gate_ref, x_ref, woa_ref, wob_ref, wo_ref, gp_ref, out_ref):
    y_a = jnp.dot(ya_ref[...], woa_ref[...], preferred_element_type=F32)
    y_b = jnp.dot(yb_ref[...], wob_ref[...], preferred_element_type=F32)
    gates = gate_ref[...].astype(F32)
    merged = gates[:, :D_MODEL] * y_a + gates[:, D_MODEL:] * y_b
    o = _dot(merged, wo_ref[...])
    ms = jnp.mean(o * o, axis=-1, keepdims=True)
    out_ref[...] = x_ref[...] + o * lax.rsqrt(ms + NORM_EPS) * gp_ref[...]


def _out_proj(ya, yb, gates, x2, woa, wob, wo, g_post, tm):
    t = x2.shape[0]
    const = lambda shape: pl.BlockSpec(shape, lambda i: (0,) * len(shape))
    rows = lambda n: pl.BlockSpec((tm, n), lambda i: (i, 0))
    return pl.pallas_call(
        _out_kernel,
        grid=(t // tm,),
        in_specs=[
            rows(A_WIDTH), rows(B_WIDTH), rows(2 * D_MODEL), rows(D_MODEL),
            const((A_WIDTH, D_MODEL)), const((B_WIDTH, D_MODEL)),
            const((D_MODEL, D_MODEL)), const((1, D_MODEL)),
        ],
        out_specs=rows(D_MODEL),
        out_shape=jax.ShapeDtypeStruct((t, D_MODEL), F32),
        compiler_params=pltpu.CompilerParams(
            dimension_semantics=("arbitrary",), vmem_limit_bytes=VMEM_LIMIT),
        name="out_proj",
    )(ya, yb, gates, x2, woa, wob, wo, g_post)


def _rope_partner(w_pe):
    half = QK_ROPE_DIM // 2
    return jnp.concatenate([-w_pe[..., half:], w_pe[..., :half]], axis=-1)


def _prep_weights(w_in, w_decay_up, w_iclr_up, w_uq, w_ukv):
    o = np.cumsum([0, SHIFT_COLS, A_WIDTH, Q_LORA_RANK, KV_LORA_RANK, QK_ROPE_DIM, B_WIDTH])
    w_pe = w_in[:, o[4]:o[5]]
    kpe_blk = jnp.concatenate(
        [jnp.zeros((D_MODEL, QK_NOPE_DIM), F32), w_pe, _rope_partner(w_pe)], axis=-1)
    w_parts = (jnp.concatenate([w_in[:, :o[4]].astype(BF16), kpe_blk.astype(BF16)], axis=-1),
               w_in[:, o[5]:].astype(BF16))
    zeros_lora = jnp.zeros((DECAY_LORA, A_WIDTH), F32)
    wd_p = jnp.concatenate([w_decay_up, zeros_lora], axis=0).astype(BF16)
    wa_p = jnp.concatenate([zeros_lora, w_iclr_up], axis=0).astype(BF16)
    wq = w_uq.reshape(Q_LORA_RANK, B_HEADS, QK_NOPE_DIM + QK_ROPE_DIM)
    wqt_p = jnp.concatenate(
        [wq, _rope_partner(wq[..., QK_NOPE_DIM:])], axis=-1).reshape(Q_LORA_RANK, -1).T.astype(BF16)
    wkv = w_ukv.reshape(KV_LORA_RANK, B_HEADS, QK_NOPE_DIM + V_HEAD_DIM)
    wk_p = jnp.concatenate(
        [wkv[..., :QK_NOPE_DIM], jnp.zeros_like(wkv[..., :QK_NOPE_DIM])],
        axis=-1).reshape(KV_LORA_RANK, -1).astype(BF16)
    wvt_p = wkv[..., QK_NOPE_DIM:].reshape(KV_LORA_RANK, -1).T.astype(BF16)
    return w_parts, wd_p, wa_p, wqt_p, wk_p, wvt_p


def kernel(x, positions, g_pre, w_in, b_gate, mu_shift, w0, w_decay_up, a0, w_iclr_up, k_k, k_a,
           r_k, gn_gain, gn_bias, w_out_a, g_q, w_uq, g_kv, w_ukv, w_out_b, w_o, g_post):
    bsz, seq, _ = x.shape
    t = bsz * seq
    tm = TM_ROWS
    row = lambda a: a.reshape(1, -1).astype(F32)
    w_parts, wd_p, wa_p, wqt_p, wk_p, wvt_p = _prep_weights(w_in, w_decay_up, w_iclr_up, w_uq, w_ukv)
    x2 = x.reshape(t, D_MODEL)

    assert TM_INPROJ == RWKV_TT
    rwkv_ops, decay, silu_a, lat, silu_b, gates = _inproj(
        x2, row(g_pre), w_parts, row(b_gate), row(mu_shift),
        [row(w0), row(a0), row(k_k), row(k_a), row(r_k)], wd_p, wa_p, seq, TM_INPROJ)

    ya = _rwkv(rwkv_ops, decay, silu_a, row(gn_gain), row(gn_bias), bsz, seq, RWKV_TT)

    freq = ROPE_THETA ** (-(np.arange(QK_ROPE_DIM) % (QK_ROPE_DIM // 2)) * 2.0 / QK_ROPE_DIM)
    invf = jnp.asarray(freq.reshape(QK_ROPE_DIM, 1), F32)
    pos3 = positions.reshape(t // tm, 1, tm)
    q_t, k, v_t = _mla_prep(lat, pos3, invf, g_q.reshape(-1, 1).astype(F32), row(g_kv),
                            wqt_p, wk_p, wvt_p, bsz, seq, tm)
    yb = _attention(q_t, k.reshape(bsz, seq, -1), v_t, silu_b.reshape(bsz, seq, -1), ATT_TQ)

    out = _out_proj(ya, yb.reshape(t, B_WIDTH), gates, x2, w_out_a.astype(BF16),
                    w_out_b.astype(BF16), w_o.astype(BF16), row(g_post), tm)
    return out.reshape(bsz, seq, D_MODEL)
```
